```python
import math
import jax, jax.numpy as jnp
from jax import lax
import numpy as np

D_MODEL = 2048
BATCH = 2
SEQ = 4096
DEPTH = 4

N_MIXERS = 2
N_ATTN_LAYERS = (DEPTH + 1) // 2
N_SSM_LAYERS = DEPTH // 2
EPS = 1e-6
ROPE_THETA = 10000.0

ATTN_HEADS = 16
ATTN_HEAD_DIM = D_MODEL // ATTN_HEADS // 2
ATTN_V_DIM = 2 * ATTN_HEAD_DIM
ATTN_QK_WIDTH = ATTN_HEADS * 2 * ATTN_HEAD_DIM
ATTN_WIDTH = ATTN_HEADS * ATTN_V_DIM
ATTN_IN = 2 * ATTN_QK_WIDTH + 2 * ATTN_WIDTH
Q_BLOCK = 128

SSM_EXPAND = 2
SSM_D_INNER = SSM_EXPAND * D_MODEL
SSM_HEAD_DIM = 64
SSM_HEADS = SSM_D_INNER // SSM_HEAD_DIM
SSM_GROUPS = 8
SSM_STATE = 128
SSM_CONV = 4
SSM_CHUNK = 128
SSM_CONV_DIM = SSM_D_INNER + 2 * SSM_GROUPS * SSM_STATE
SSM_IN = SSM_D_INNER + SSM_CONV_DIM + SSM_HEADS

kernel_name = "hybrid_diffattn_ssd_adaln_trunk"


def rms_norm(x, w, eps=EPS):
    xf = x.astype(jnp.float32)
    y = xf * lax.rsqrt(jnp.mean(xf * xf, axis=-1, keepdims=True) + eps)
    return (y * w.astype(jnp.float32)).astype(x.dtype)


def rope(x, positions):
    d = x.shape[-1]
    inv_freq = ROPE_THETA ** (-jnp.arange(0, d, 2, dtype=jnp.float32) / d)
    ang = positions.astype(jnp.float32)[..., None] * inv_freq
    cos = jnp.cos(ang)[:, :, None, None, :]
    sin = jnp.sin(ang)[:, :, None, None, :]
    x1, x2 = jnp.split(x.astype(jnp.float32), 2, axis=-1)
    return jnp.concatenate([x1 * cos - x2 * sin, x2 * cos + x1 * sin], axis=-1).astype(x.dtype)


def diff_attention(h, positions, w_in, q_norm, k_norm, lq1, lk1, lq2, lk2,
                   subln_w, w_out, lambda_init):
    b, s, _ = h.shape
    H, dh, dv = ATTN_HEADS, ATTN_HEAD_DIM, ATTN_V_DIM
    nb = s // Q_BLOCK
    proj = h @ w_in
    q, k, v, g = jnp.split(
        proj, [ATTN_QK_WIDTH, 2 * ATTN_QK_WIDTH, 2 * ATTN_QK_WIDTH + ATTN_WIDTH], axis=-1)
    q = rope(rms_norm(q.reshape(b, s, H, 2, dh), q_norm), positions)
    k = rope(rms_norm(k.reshape(b, s, H, 2, dh), k_norm), positions)
    v = v.reshape(b, s, H, dv)
    lam = (jnp.exp(jnp.sum(lq1.astype(jnp.float32) * lk1.astype(jnp.float32)))
           - jnp.exp(jnp.sum(lq2.astype(jnp.float32) * lk2.astype(jnp.float32)))
           + lambda_init)
    qb = q.reshape(b, nb, Q_BLOCK, H, 2, dh).transpose(1, 0, 3, 4, 2, 5)
    kt = k.transpose(0, 2, 3, 1, 4)
    vt = v.transpose(0, 2, 1, 3)
    key_idx = jnp.arange(s)
    scale = dh ** -0.5

    def block(args):
        i, qi = args
        sc = jnp.einsum('bhmqd,bhmkd->bhmqk', qi, kt).astype(jnp.float32) * scale
        q_idx = i * Q_BLOCK + jnp.arange(Q_BLOCK)
        causal = key_idx[None, :] <= q_idx[:, None]
        p = jax.nn.softmax(jnp.where(causal, sc, -jnp.inf), axis=-1)
        pd = p[:, :, 0] - lam * p[:, :, 1]
        return jnp.einsum('bhqk,bhkv->bhqv', pd.astype(vt.dtype), vt)

    o = lax.map(block, (jnp.arange(nb), qb))
    o = o.transpose(1, 0, 3, 2, 4).reshape(b, s, H, dv)
    o = rms_norm(o, subln_w) * (1.0 - lambda_init)
    o = o.reshape(b, s, ATTN_WIDTH) * jax.nn.silu(g)
    return (o @ w_out).astype(h.dtype)


def causal_depthwise_conv(x, w, bias):
    y = lax.conv_general_dilated(
        x, w[:, None, :].astype(x.dtype), window_strides=(1,), padding=[(SSM_CONV - 1, 0)],
        dimension_numbers=('NWC', 'WIO', 'NWC'), feature_group_count=x.shape[-1])
    return y + bias


def ssd_mixer(h, w_in, conv_w, conv_b, dt_bias, A_log, D_skip, norm_w, w_out):
    b, s, _ = h.shape
    G, R, P, N, L = SSM_GROUPS, SSM_HEADS // SSM_GROUPS, SSM_HEAD_DIM, SSM_STATE, SSM_CHUNK
    nc = s // L
    proj = h @ w_in
    z, xbc, dt = jnp.split(proj, [SSM_D_INNER, SSM_D_INNER + SSM_CONV_DIM], axis=-1)
    xbc = jax.nn.silu(causal_depthwise_conv(xbc, conv_w, conv_b))
    xs, Bm, Cm = jnp.split(xbc, [SSM_D_INNER, SSM_D_INNER + G * N], axis=-1)
    dt = jax.nn.softplus((dt + dt_bias).astype(jnp.float32))
    A = -jnp.exp(A_log.astype(jnp.float32))
    xs = xs.reshape(b, nc, L, G, R, P)
    Bm = Bm.reshape(b, nc, L, G, N)
    Cm = Cm.reshape(b, nc, L, G, N)
    dt_c = dt.reshape(b, nc, L, G, R)
    xdt = xs * dt_c[..., None]
    dA_cs = jnp.cumsum((dt_c * A.reshape(G, R)).transpose(0, 3, 4, 1, 2), axis=-1)
    t_idx = jnp.arange(L)
    causal = t_idx[:, None] >= t_idx[None, :]
    seg = dA_cs[..., :, None] - dA_cs[..., None, :]
    decay = jnp.exp(jnp.where(causal, seg, -jnp.inf))
    CB = jnp.einsum('bclgn,bcsgn->bgcls', Cm, Bm)
    y_diag = jnp.einsum('bgcls,bgrcls,bcsgrp->bclgrp', CB, decay, xdt)
    decay_states = jnp.exp(dA_cs[..., -1:] - dA_cs)
    states = jnp.einsum('bclgn,bgrcl,bclgrp->cbgrpn', Bm, decay_states, xdt)
    chunk_decay = jnp.exp(dA_cs[..., -1]).transpose(3, 0, 1, 2)

    def step(hstate, inp):
        st, dec = inp
        return hstate * dec[..., None, None] + st, hstate

    h0 = jnp.zeros(states.shape[1:], states.dtype)
    _, prev = lax.scan(step, h0, (states, chunk_decay))
    y_off = jnp.einsum('bclgn,cbgrpn,bgrcl->bclgrp', Cm, prev, jnp.exp(dA_cs))
    y = y_diag + y_off + xs * D_skip.reshape(G, R)[:, :, None]
    y = y.reshape(b, s, G, R * P) * jax.nn.silu(z).reshape(b, s, G, R * P)
    y = rms_norm(y, norm_w.reshape(G, R * P))
    return (y.reshape(b, s, SSM_D_INNER) @ w_out).astype(h.dtype)


def setup_inputs(seed: int = 0) -> dict:
    key = jax.random.key(seed)
    ks = jax.random.split(key, 24)
    f32 = jnp.float32
    nA, nS = N_ATTN_LAYERS, N_SSM_LAYERS
    x = jax.random.normal(ks[0], (BATCH, SEQ, D_MODEL), f32)
    c = jax.random.normal(ks[1], (BATCH, D_MODEL), f32)
    offset = jax.random.randint(ks[2], (BATCH, 1), 0, SEQ, dtype=jnp.int32)
    positions = (jnp.arange(SEQ, dtype=jnp.int32)[None, :] + offset).astype(jnp.int32)
    norm_w = 1.0 + 0.02 * jax.random.normal(ks[3], (DEPTH, D_MODEL), f32)
    ada_w = 0.5 * D_MODEL ** -0.5 * jax.random.normal(ks[4], (DEPTH, D_MODEL, 3 * D_MODEL), f32)
    ada_b = 0.02 * jax.random.normal(ks[5], (DEPTH, 3 * D_MODEL), f32)
    attn_w_in = D_MODEL ** -0.5 * jax.random.normal(ks[6], (nA, D_MODEL, ATTN_IN), f32)
    attn_q_norm = 1.0 + 0.02 * jax.random.normal(ks[7], (nA, ATTN_HEAD_DIM), f32)
    attn_k_norm = 1.0 + 0.02 * jax.random.normal(ks[8], (nA, ATTN_HEAD_DIM), f32)
    attn_lambda_q1 = 0.1 * jax.random.normal(ks[9], (nA, ATTN_HEAD_DIM), f32)
    attn_lambda_k1 = 0.1 * jax.random.normal(ks[10], (nA, ATTN_HEAD_DIM), f32)
    attn_lambda_q2 = 0.1 * jax.random.normal(ks[11], (nA, ATTN_HEAD_DIM), f32)
    attn_lambda_k2 = 0.1 * jax.random.normal(ks[12], (nA, ATTN_HEAD_DIM), f32)
    attn_subln_w = 1.0 + 0.02 * jax.random.normal(ks[13], (nA, ATTN_V_DIM), f32)
    attn_w_out = ATTN_WIDTH ** -0.5 * jax.random.normal(ks[14], (nA, ATTN_WIDTH, D_MODEL), f32)
    ssm_w_in = D_MODEL ** -0.5 * jax.random.normal(ks[15], (nS, D_MODEL, SSM_IN), f32)
    ssm_conv_w = SSM_CONV ** -0.5 * jax.random.normal(ks[16], (nS, SSM_CONV, SSM_CONV_DIM), f32)
    ssm_conv_b = 0.02 * jax.random.normal(ks[17], (nS, SSM_CONV_DIM), f32)
    u = jax.random.uniform(ks[18], (nS, SSM_HEADS), f32)
    dt0 = jnp.exp(u * (math.log(0.1) - math.log(0.001)) + math.log(0.001))
    ssm_dt_bias = dt0 + jnp.log(-jnp.expm1(-dt0))
    ssm_A_log = jnp.log(jax.random.uniform(ks[19], (nS, SSM_HEADS), f32, 1.0, 16.0))
    ssm_D = 1.0 + 0.02 * jax.random.normal(ks[20], (nS, SSM_HEADS), f32)
    ssm_norm_w = 1.0 + 0.02 * jax.random.normal(ks[21], (nS, SSM_D_INNER), f32)
    ssm_w_out = SSM_D_INNER ** -0.5 * jax.random.normal(ks[22], (nS, SSM_D_INNER, D_MODEL), f32)
    return {"x": x, "c": c, "positions": positions, "norm_w": norm_w,
            "ada_w": ada_w, "ada_b": ada_b,
            "attn_w_in": attn_w_in, "attn_q_norm": attn_q_norm, "attn_k_norm": attn_k_norm,
            "attn_lambda_q1": attn_lambda_q1, "attn_lambda_k1": attn_lambda_k1,
            "attn_lambda_q2": attn_lambda_q2, "attn_lambda_k2": attn_lambda_k2,
            "attn_subln_w": attn_subln_w, "attn_w_out": attn_w_out,
            "ssm_w_in": ssm_w_in, "ssm_conv_w": ssm_conv_w, "ssm_conv_b": ssm_conv_b,
            "ssm_dt_bias": ssm_dt_bias, "ssm_A_log": ssm_A_log, "ssm_D": ssm_D,
            "ssm_norm_w": ssm_norm_w, "ssm_w_out": ssm_w_out}


def reference(x, c, positions, norm_w, ada_w, ada_b,
              attn_w_in, attn_q_norm, attn_k_norm,
              attn_lambda_q1, attn_lambda_k1, attn_lambda_q2, attn_lambda_k2,
              attn_subln_w, attn_w_out,
              ssm_w_in, ssm_conv_w, ssm_conv_b, ssm_dt_bias, ssm_A_log, ssm_D,
              ssm_norm_w, ssm_w_out):
    cond = jax.nn.silu(c)
    for layer in range(DEPTH):
        mod = cond @ ada_w[layer] + ada_b[layer]
        shift, scale, gate = jnp.split(mod[:, None, :], 3, axis=-1)
        h = rms_norm(x, norm_w[layer]) * (1.0 + scale) + shift
        j = layer // N_MIXERS
        if layer % N_MIXERS == 0:
            lambda_init = 0.8 - 0.6 * math.exp(-0.3 * layer)
            out = diff_attention(h, positions, attn_w_in[j], attn_q_norm[j], attn_k_norm[j],
                                 attn_lambda_q1[j], attn_lambda_k1[j],
                                 attn_lambda_q2[j], attn_lambda_k2[j],
                                 attn_subln_w[j], attn_w_out[j], lambda_init)
        else:
            out = ssd_mixer(h, ssm_w_in[j], ssm_conv_w[j], ssm_conv_b[j], ssm_dt_bias[j],
                            ssm_A_log[j], ssm_D[j], ssm_norm_w[j], ssm_w_out[j])
        x = x + gate * out
    return x
```

```python
import functools
import math

import jax
import jax.numpy as jnp
from jax import lax
from jax.experimental import pallas as pl
from jax.experimental.pallas import tpu as pltpu

EPS = 1e-6
ROPE_THETA = 10000.0
LANES = 128
HEAD_DIM = 64
SSM_HEAD_DIM = 64
SSM_STATE = 128
SSM_CHUNK = 128
SSM_CONV = 4
CONV_HALO = 8
VMEM_LIMIT_BYTES = 56 * 1024 * 1024
NEG_BIG = -1e30

f32 = jnp.float32
bf16 = jnp.bfloat16


def _params(*sem):
    return pltpu.CompilerParams(dimension_semantics=sem, vmem_limit_bytes=VMEM_LIMIT_BYTES)


def _silu(x):
    return x * jax.nn.sigmoid(x)


def _tile(n, target):
    if n <= target:
        return n
    best = None
    for cand in range(LANES, target + 1, LANES):
        if n % cand == 0:
            best = cand
    assert best is not None, (n, target)
    return best


def _ada_kernel(c_ref, w_ref, b_ref, o_ref):
    cond = _silu(c_ref[...])
    o_ref[0] = jnp.dot(cond, w_ref[0], preferred_element_type=f32) + b_ref[0]


def _ada_mod(c_pad, ada_w, ada_b, tn=1024):
    depth, d, n = ada_w.shape
    rows = c_pad.shape[0]
    tn = _tile(n, tn)
    return pl.pallas_call(
        _ada_kernel,
        grid=(depth, n // tn),
        in_specs=[pl.BlockSpec((rows, d), lambda l, j: (0, 0)),
                  pl.BlockSpec((1, d, tn), lambda l, j: (l, 0, j)),
                  pl.BlockSpec((1, 1, tn), lambda l, j: (l, 0, j))],
        out_specs=pl.BlockSpec((1, rows, tn), lambda l, j: (l, 0, j)),
        out_shape=jax.ShapeDtypeStruct((depth, rows, n), f32),
        compiler_params=_params("arbitrary", "arbitrary"),
        name="ada_mod",
    )(c_pad, ada_w, ada_b.reshape(depth, 1, n))


def _normproj_kernel(x_ref, nw_ref, sc_ref, sh_ref, w_ref, o_ref, h_ref):
    @pl.when(pl.program_id(1) == 0)
    def _():
        x = x_ref[...]
        ms = jnp.mean(x * x, axis=-1, keepdims=True)
        y = x * lax.rsqrt(ms + EPS) * nw_ref[...]
        h_ref[...] = (y * (1.0 + sc_ref[0]) + sh_ref[0]).astype(h_ref.dtype)

    o_ref[...] = jnp.dot(h_ref[...], w_ref[...], preferred_element_type=f32).astype(o_ref.dtype)


def _norm_proj(x2, nw, scale, shift, w, seq, out_dtype, tm, tn):
    t, d = x2.shape
    n = w.shape[1]
    tiles_per_seq = seq // tm
    return pl.pallas_call(
        _normproj_kernel,
        grid=(t // tm, n // tn),
        in_specs=[pl.BlockSpec((tm, d), lambda i, j: (i, 0)),
                  pl.BlockSpec((1, d), lambda i, j: (0, 0)),
                  pl.BlockSpec((1, 1, d), lambda i, j: (i // tiles_per_seq, 0, 0)),
                  pl.BlockSpec((1, 1, d), lambda i, j: (i // tiles_per_seq, 0, 0)),
                  pl.BlockSpec((d, tn), lambda i, j: (0, j))],
        out_specs=pl.BlockSpec((tm, tn), lambda i, j: (i, j)),
        out_shape=jax.ShapeDtypeStruct((t, n), out_dtype),
        scratch_shapes=[pltpu.VMEM((tm, d), bf16)],
        compiler_params=_params("arbitrary", "arbitrary"),
        name="norm_proj",
    )(x2, nw, scale, shift, w)


def _outproj_kernel(a_ref, w_ref, x_ref, g_ref, o_ref):
    acc = jnp.dot(a_ref[...], w_ref[...], preferred_element_type=f32)
    o_ref[...] = x_ref[...] + g_ref[0] * acc


def _out_proj(a, w, x2, gate, seq, tm, tn):
    t, k = a.shape
    n = w.shape[1]
    tiles_per_seq = seq // tm
    return pl.pallas_call(
        _outproj_kernel,
        grid=(t // tm, n // tn),
        in_specs=[pl.BlockSpec((tm, k), lambda i, j: (i, 0)),
                  pl.BlockSpec((k, tn), lambda i, j: (0, j)),
                  pl.BlockSpec((tm, tn), lambda i, j: (i, j)),
                  pl.BlockSpec((1, 1, tn), lambda i, j: (i // tiles_per_seq, 0, j))],
        out_specs=pl.BlockSpec((tm, tn), lambda i, j: (i, j)),
        out_shape=jax.ShapeDtypeStruct((t, n), f32),
        compiler_params=_params("arbitrary", "arbitrary"),
        name="out_proj",
    )(a, w, x2, gate)


def _rope_table_kernel(pos_ref, freq_ref, cos_ref, sin_ref):
    ang = pos_ref[...].astype(f32) * freq_ref[...]
    lane = lax.broadcasted_iota(jnp.int32, ang.shape, 1)
    first_half = (lane & (HEAD_DIM // 2)) == 0
    s = jnp.sin(ang)
    cos_ref[...] = jnp.cos(ang)
    sin_ref[...] = jnp.where(first_half, -s, s)


def _rope_tables(pos_col, freq_row, tm=1024):
    t = pos_col.shape[0]
    tm = min(tm, t)
    return pl.pallas_call(
        _rope_table_kernel,
        grid=(t // tm,),
        in_specs=[pl.BlockSpec((tm, 1), lambda i: (i, 0)),
                  pl.BlockSpec((1, LANES), lambda i: (0, 0))],
        out_specs=[pl.BlockSpec((tm, LANES), lambda i: (i, 0)),
                   pl.BlockSpec((tm, LANES), lambda i: (i, 0))],
        out_shape=[jax.ShapeDtypeStruct((t, LANES), f32)] * 2,
        compiler_params=_params("arbitrary"),
        name="rope_table",
    )(pos_col, freq_row)


def _qkprep_kernel(p_ref, w_ref, cos_ref, sin_ref, o_ref):
    width = p_ref.shape[1]
    r = lax.broadcasted_iota(jnp.int32, (LANES, LANES), 0) // HEAD_DIM
    c = lax.broadcasted_iota(jnp.int32, (LANES, LANES), 1) // HEAD_DIM
    group_mean = jnp.where(r == c, 1.0 / HEAD_DIM, 0.0).astype(bf16)
    cos = cos_ref[...]
    sin = sin_ref[...]
    lane = lax.broadcasted_iota(jnp.int32, cos.shape, 1)
    first_half = (lane & (HEAD_DIM // 2)) == 0
    for j in range(width // LANES):
        sl = slice(j * LANES, (j + 1) * LANES)
        x = p_ref[:, sl].astype(f32)
        ms = jnp.dot((x * x).astype(bf16), group_mean, preferred_element_type=f32)
        y = x * lax.rsqrt(ms + EPS) * w_ref[0, :, sl]
        partner = jnp.where(first_half,
                            pltpu.roll(y, LANES - HEAD_DIM // 2, 1),
                            pltpu.roll(y, HEAD_DIM // 2, 1))
        o_ref[:, sl] = (y * cos + partner * sin).astype(o_ref.dtype)


def _qk_prep(proj, qk_w, cos_t, sin_t, width, tm=512):
    t = proj.shape[0]
    tm = min(tm, t)
    return pl.pallas_call(
        _qkprep_kernel,
        grid=(t // tm, 2),
        in_specs=[pl.BlockSpec((tm, width), lambda i, j: (i, j)),
                  pl.BlockSpec((1, 1, width), lambda i, j: (j, 0, 0)),
                  pl.BlockSpec((tm, LANES), lambda i, j: (i, 0)),
                  pl.BlockSpec((tm, LANES), lambda i, j: (i, 0))],
        out_specs=pl.BlockSpec((tm, width), lambda i, j: (i, j)),
        out_shape=jax.ShapeDtypeStruct((t, 2 * width), bf16),
        compiler_params=_params("arbitrary", "arbitrary"),
        name="qk_prep",
    )(proj, qk_w, cos_t, sin_t)


def _attn_kernel(q_ref, k_ref, v_ref, g_ref, lq1_ref, lk1_ref, lq2_ref, lk2_ref, sw_ref, o_ref,
                 *, blk, lambda_init):
    i = pl.program_id(2)
    lam = (jnp.exp(jnp.sum(lq1_ref[...] * lk1_ref[...], axis=-1, keepdims=True))
           - jnp.exp(jnp.sum(lq2_ref[...] * lk2_ref[...], axis=-1, keepdims=True))
           + lambda_init)

    q = q_ref[...]
    lane = lax.broadcasted_iota(jnp.int32, q.shape, 1)
    zero = jnp.zeros_like(q)
    qq = jnp.concatenate([jnp.where(lane < HEAD_DIM, q, zero),
                          jnp.where(lane >= HEAD_DIM, q, zero)], axis=0)

    def step(j, carry, masked):
        m, l, acc = carry
        start = pl.multiple_of(j * blk, blk)
        kb = k_ref[pl.ds(start, blk), :]
        vb = v_ref[pl.ds(start, blk), :]
        s = lax.dot_general(qq, kb, (((1,), (1,)), ((), ())), preferred_element_type=f32)
        if masked:
            row = lax.broadcasted_iota(jnp.int32, s.shape, 0) & (blk - 1)
            col = lax.broadcasted_iota(jnp.int32, s.shape, 1)
            s = jnp.where(col <= row, s, NEG_BIG)
        m_new = jnp.maximum(m, jnp.max(s, axis=-1, keepdims=True))
        alpha = jnp.exp(m - m_new)
        p = jnp.exp(s - m_new)
        l_new = alpha * l + jnp.sum(p, axis=-1, keepdims=True)
        acc_new = alpha * acc + jnp.dot(p.astype(bf16), vb, preferred_element_type=f32)
        return m_new, l_new, acc_new

    init = (jnp.full((2 * blk, 1), NEG_BIG, f32),
            jnp.zeros((2 * blk, 1), f32),
            jnp.zeros((2 * blk, LANES), f32))
    carry = lax.fori_loop(0, i, lambda j, cr: step(j, cr, False), init)
    m, l, acc = step(i, carry, True)

    o = acc / l
    od = o[:blk] - lam * o[blk:]
    ms = jnp.mean(od * od, axis=-1, keepdims=True)
    y = od * lax.rsqrt(ms + EPS) * sw_ref[...] * (1.0 - lambda_init)
    g = g_ref[...].astype(f32)
    o_ref[...] = (y * _silu(g)).astype(o_ref.dtype)


def _diff_attn(qk, proj, lq1, lk1, lq2, lk2, subln_w, *, batch, seq, heads, lambda_init, blk=256):
    t = qk.shape[0]
    blk = min(blk, seq)
    nq = seq // blk
    vec = lambda a: a.reshape(1, -1).astype(f32)
    small = pl.BlockSpec((1, HEAD_DIM), lambda b, h, i: (0, 0))
    kern = functools.partial(_attn_kernel, blk=blk, lambda_init=lambda_init)
    return pl.pallas_call(
        kern,
        grid=(batch, heads, nq),
        in_specs=[pl.BlockSpec((blk, LANES), lambda b, h, i: (b * nq + i, h)),
                  pl.BlockSpec((seq, LANES), lambda b, h, i: (b, heads + h)),
                  pl.BlockSpec((seq, LANES), lambda b, h, i: (b, 2 * heads + h)),
                  pl.BlockSpec((blk, LANES), lambda b, h, i: (b * nq + i, 3 * heads + h)),
                  small, small, small, small,
                  pl.BlockSpec((1, LANES), lambda b, h, i: (0, 0))],
        out_specs=pl.BlockSpec((blk, LANES), lambda b, h, i: (b * nq + i, h)),
        out_shape=jax.ShapeDtypeStruct((t, heads * LANES), bf16),
        compiler_params=_params("arbitrary", "arbitrary", "arbitrary"),
        name="diff_attn",
    )(qk, qk, proj, proj, vec(lq1), vec(lk1), vec(lq2), vec(lk2), vec(subln_w))


def _expand_heads(a, heads_per_group):
    rows = a.shape[0]
    lane = lax.broadcasted_iota(jnp.int32, (rows, LANES), 1)
    pieces = []
    for j in range(heads_per_group // 2):
        lo = jnp.broadcast_to(a[:, 2 * j:2 * j + 1], (rows, LANES))
        hi = jnp.broadcast_to(a[:, 2 * j + 1:2 * j + 2], (rows, LANES))
        pieces.append(jnp.where(lane < SSM_HEAD_DIM, lo, hi))
    return jnp.concatenate(pieces, axis=1)


def _ssd_kernel(dt_ref, dtb_ref, alog_ref, z_ref, xs_ref, b_ref, c_ref,
                wx_ref, wb_ref, wc_ref, bx_ref, bb_ref, bc_ref, dskip_ref, nw_ref,
                o_ref,
                state_ref, halo_ref, buf_ref, cs_ref, dtc_ref, cst_ref, dtt_ref,
                *, groups, rheads):
    c = pl.program_id(1)
    g = pl.program_id(2)
    L = SSM_CHUNK
    xw = rheads * SSM_HEAD_DIM

    @pl.when(g == 0)
    def _():
        raw = dt_ref[...] + dtb_ref[...]
        dt = jnp.maximum(raw, 0.0) + jnp.log(1.0 + jnp.exp(-jnp.abs(raw)))
        da_t = (dt * (-jnp.exp(alog_ref[...]))).T
        k_idx = lax.broadcasted_iota(jnp.int32, (L, L), 0)
        l_idx = lax.broadcasted_iota(jnp.int32, (L, L), 1)
        upper = jnp.where(k_idx <= l_idx, 1.0, 0.0).astype(f32)
        cs_t = jnp.dot(da_t, upper, precision=lax.Precision.HIGHEST,
                       preferred_element_type=f32)
        cst_ref[...] = cs_t
        dtt_ref[...] = dt.T
        cs = cs_t.T
        for gg in range(groups):
            shift = (LANES - gg * rheads) % LANES
            cs_ref[gg] = pltpu.roll(cs, shift, 1) if shift else cs
            dtc_ref[gg] = pltpu.roll(dt, shift, 1) if shift else dt

    @pl.when(c == 0)
    def _():
        state_ref[g] = jnp.zeros(state_ref.shape[1:], f32)
        halo_ref[g] = jnp.zeros(halo_ref.shape[1:], f32)

    xin = jnp.concatenate([xs_ref[...], b_ref[...], c_ref[...]], axis=1).astype(f32)
    buf_ref[0:CONV_HALO, :] = halo_ref[g]
    buf_ref[CONV_HALO:CONV_HALO + L, :] = xin
    halo_ref[g] = xin[L - CONV_HALO:, :]
    w = jnp.concatenate([wx_ref[...], wb_ref[...], wc_ref[...]], axis=1)
    bias = jnp.concatenate([bx_ref[...], bb_ref[...], bc_ref[...]], axis=1)
    acc = jnp.broadcast_to(bias, xin.shape)
    for k in range(SSM_CONV):
        off = CONV_HALO - (SSM_CONV - 1) + k
        acc = acc + w[k:k + 1, :] * buf_ref[off:off + L, :]
    xbc = _silu(acc)
    xs = xbc[:, :xw]
    bm = xbc[:, xw:xw + SSM_STATE]
    cm = xbc[:, xw + SSM_STATE:]

    row0 = pl.multiple_of(g * rheads, rheads)
    cs_t = cst_ref[pl.ds(row0, rheads), :]
    dt_t = dtt_ref[pl.ds(row0, rheads), :]
    cs_c = cs_ref[g]
    dt_c = dtc_ref[g]

    cb = lax.dot_general(cm.astype(bf16), bm.astype(bf16), (((1,), (1,)), ((), ())),
                         preferred_element_type=f32)
    l_idx = lax.broadcasted_iota(jnp.int32, (L, L), 0)
    s_idx = lax.broadcasted_iota(jnp.int32, (L, L), 1)
    causal = l_idx >= s_idx
    lane = lax.broadcasted_iota(jnp.int32, (L, LANES), 1)
    lo_half = lane < SSM_HEAD_DIM

    y_parts = []
    for j in range(rheads // 2):
        xj = xs[:, j * LANES:(j + 1) * LANES]
        x_lo = jnp.where(lo_half, xj, 0.0).astype(bf16)
        x_hi = jnp.where(lo_half, 0.0, xj).astype(bf16)
        yj = None
        for r, xr in ((2 * j, x_lo), (2 * j + 1, x_hi)):
            seg = jnp.broadcast_to(cs_c[:, r:r + 1], (L, L)) - cs_t[r:r + 1, :]
            decay = jnp.exp(jnp.where(causal, seg, -jnp.inf))
            mr = (cb * decay * dt_t[r:r + 1, :]).astype(bf16)
            part = jnp.dot(mr, xr, preferred_element_type=f32)
            yj = part if yj is None else yj + part
        y_parts.append(yj)
    y_diag = jnp.concatenate(y_parts, axis=1)

    prev = state_ref[g]
    cs_last = cs_c[L - 1:L, :]
    e_cs = _expand_heads(jnp.exp(cs_c), rheads)
    y_off = jnp.dot(cm.astype(bf16), prev.astype(bf16), preferred_element_type=f32) * e_cs

    wgt = _expand_heads(jnp.exp(cs_last - cs_c) * dt_c, rheads)
    new = jnp.dot(bm.T.astype(bf16), (xs * wgt).astype(bf16), preferred_element_type=f32)
    chunk_decay = _expand_heads(jnp.broadcast_to(jnp.exp(cs_last), (8, LANES)), rheads)[0:1, :]
    state_ref[g] = prev * chunk_decay + new

    y = y_diag + y_off + xs * dskip_ref[...]
    y = y * _silu(z_ref[...].astype(f32))
    ms = jnp.mean(y * y, axis=-1, keepdims=True)
    o_ref[...] = (y * lax.rsqrt(ms + EPS) * nw_ref[...]).astype(o_ref.dtype)


def _ssd(proj, dt_raw, dt_bias, a_log, conv_w, conv_b, d_skip, norm_w, *, batch, seq, d_inner, groups):
    t = proj.shape[0]
    L = SSM_CHUNK
    nc = seq // L
    heads = d_inner // SSM_HEAD_DIM
    rheads = heads // groups
    xw = rheads * SSM_HEAD_DIM
    n = SSM_STATE
    xs_blk0 = d_inner // xw
    b_blk0 = 2 * d_inner // n
    c_blk0 = b_blk0 + groups
    cb_blk0 = d_inner // n
    cc_blk0 = cb_blk0 + groups
    row = lambda b, c, g: b * nc + c
    pad = lambda a: jnp.pad(a.reshape(1, -1).astype(f32), ((0, 0), (0, LANES - heads)))
    kern = functools.partial(_ssd_kernel, groups=groups, rheads=rheads)
    return pl.pallas_call(
        kern,
        grid=(batch, nc, groups),
        in_specs=[pl.BlockSpec((L, LANES), lambda b, c, g: (row(b, c, g), 0)),
                  pl.BlockSpec((1, LANES), lambda b, c, g: (0, 0)),
                  pl.BlockSpec((1, LANES), lambda b, c, g: (0, 0)),
                  pl.BlockSpec((L, xw), lambda b, c, g: (row(b, c, g), g)),
                  pl.BlockSpec((L, xw), lambda b, c, g: (row(b, c, g), xs_blk0 + g)),
                  pl.BlockSpec((L, n), lambda b, c, g: (row(b, c, g), b_blk0 + g)),
                  pl.BlockSpec((L, n), lambda b, c, g: (row(b, c, g), c_blk0 + g)),
                  pl.BlockSpec((SSM_CONV, xw), lambda b, c, g: (0, g)),
                  pl.BlockSpec((SSM_CONV, n), lambda b, c, g: (0, cb_blk0 + g)),
                  pl.BlockSpec((SSM_CONV, n), lambda b, c, g: (0, cc_blk0 + g)),
                  pl.BlockSpec((1, xw), lambda b, c, g: (0, g)),
                  pl.BlockSpec((1, n), lambda b, c, g: (0, cb_blk0 + g)),
                  pl.BlockSpec((1, n), lambda b, c, g: (0, cc_blk0 + g)),
                  pl.BlockSpec((1, xw), lambda b, c, g: (0, g)),
                  pl.BlockSpec((1, xw), lambda b, c, g: (0, g))],
        out_specs=pl.BlockSpec((L, xw), lambda b, c, g: (row(b, c, g), g)),
        out_shape=jax.ShapeDtypeStruct((t, d_inner), bf16),
        scratch_shapes=[pltpu.VMEM((groups, n, xw), f32),
                        pltpu.VMEM((groups, CONV_HALO, xw + 2 * n), f32),
                        pltpu.VMEM((CONV_HALO + L, xw + 2 * n), f32),
                        pltpu.VMEM((groups, L, LANES), f32),
                        pltpu.VMEM((groups, L, LANES), f32),
                        pltpu.VMEM((LANES, L), f32),
                        pltpu.VMEM((LANES, L), f32)],
        compiler_params=_params("arbitrary", "arbitrary", "arbitrary"),
        name="ssd_chunk",
    )(dt_raw, pad(dt_bias), pad(a_log), proj, proj, proj, proj,
      conv_w, conv_w, conv_w, conv_b.reshape(1, -1), conv_b.reshape(1, -1), conv_b.reshape(1, -1),
      jnp.repeat(d_skip, SSM_HEAD_DIM).reshape(1, -1), norm_w.reshape(1, -1))


def kernel(x, c, positions, norm_w, ada_w, ada_b, attn_w_in, attn_q_norm, attn_k_norm, attn_lambda_q1, attn_lambda_k1, attn_lambda_q2, attn_lambda_k2, attn_subln_w, attn_w_out, ssm_w_in, ssm_conv_w, ssm_conv_b, ssm_dt_bias, ssm_A_log, ssm_D, ssm_norm_w, ssm_w_out):
    batch, seq, d = x.shape
    depth = norm_w.shape[0]
    t = batch * seq
    heads = d // (2 * HEAD_DIM)
    qk_width = heads * 2 * HEAD_DIM
    d_inner = ssm_w_out.shape[1]
    conv_dim = ssm_conv_w.shape[2]
    groups = (conv_dim - d_inner) // (2 * SSM_STATE)
    ssm_heads = ssm_dt_bias.shape[1]
    tm = min(1024, seq)

    c_pad = jnp.pad(c, ((0, 8 - batch), (0, 0)))
    mod = _ada_mod(c_pad, ada_w, ada_b)[:, :batch]
    shift, scale, gate = (mod[:, :, i * d:(i + 1) * d].reshape(depth, batch, 1, d) for i in range(3))

    inv_freq = ROPE_THETA ** (-jnp.arange(0, HEAD_DIM, 2, dtype=f32) / HEAD_DIM)
    freq_row = jnp.tile(inv_freq, LANES // (HEAD_DIM // 2)).reshape(1, LANES)
    cos_t, sin_t = _rope_tables(positions.reshape(t, 1), freq_row)

    x2 = x.reshape(t, d)
    for layer in range(depth):
        j = layer // 2
        nw = norm_w[layer].reshape(1, d)
        if layer % 2 == 0:
            lambda_init = 0.8 - 0.6 * math.exp(-0.3 * layer)
            proj = _norm_proj(x2, nw, scale[layer], shift[layer], attn_w_in[j].astype(bf16),
                              seq, bf16, tm, _tile(attn_w_in.shape[2], 1024))
            rep = qk_width // HEAD_DIM
            qk_w = jnp.stack([jnp.tile(attn_q_norm[j], rep) * HEAD_DIM ** -0.5,
                              jnp.tile(attn_k_norm[j], rep)]).reshape(2, 1, qk_width)
            qk = _qk_prep(proj, qk_w, cos_t, sin_t, qk_width)
            o = _diff_attn(qk, proj, attn_lambda_q1[j], attn_lambda_k1[j], attn_lambda_q2[j],
                           attn_lambda_k2[j], attn_subln_w[j], batch=batch, seq=seq, heads=heads,
                           lambda_init=lambda_init)
            x2 = _out_proj(o, attn_w_out[j].astype(bf16), x2, gate[layer], seq, tm, _tile(d, 1024))
        else:
            main = d_inner + conv_dim
            w_main = ssm_w_in[j][:, :main].astype(bf16)
            w_dt = jnp.pad(ssm_w_in[j][:, main:], ((0, 0), (0, LANES - ssm_heads))).astype(bf16)
            proj = _norm_proj(x2, nw, scale[layer], shift[layer], w_main, seq, bf16, tm, _tile(main, 1024))
            dt_raw = _norm_proj(x2, nw, scale[layer], shift[layer], w_dt, seq, f32, tm, LANES)
            y = _ssd(proj, dt_raw, ssm_dt_bias[j], ssm_A_log[j], ssm_conv_w[j], ssm_conv_b[j],
                     ssm_D[j], ssm_norm_w[j], batch=batch, seq=seq, d_inner=d_inner, groups=groups)
            x2 = _out_proj(y, ssm_w_out[j].astype(bf16), x2, gate[layer], seq, tm, _tile(d, 512))
    return x2.reshape(batch, seq, d)
```

```python
import functools
import math

import jax
import jax.numpy as jnp
from jax import lax
from jax.experimental import pallas as pl
from jax.experimental.pallas import tpu as pltpu

EPS = 1e-6
ROPE_THETA = 10000.0
LANES = 128
HEAD_DIM = 64
SSM_HEAD_DIM = 64
SSM_STATE = 128
SSM_CHUNK = 128
SSM_CONV = 4
CONV_HALO = 8
VMEM_LIMIT_BYTES = 56 * 1024 * 1024
NEG_BIG = -1e30

f32 = jnp.float32
bf16 = jnp.bfloat16


def _params(*sem):
    return pltpu.CompilerParams(dimension_semantics=sem, vmem_limit_bytes=VMEM_LIMIT_BYTES)


def _silu(x):
    return x * jax.nn.sigmoid(x)


def _tile(n, target):
    if n <= target:
        return n
    best = None
    for cand in range(LANES, target + 1, LANES):
        if n % cand == 0:
            best = cand
    assert best is not None, (n, target)
    return best


def _ada_kernel(c_ref, w_ref, b_ref, o_ref):
    cond = _silu(c_ref[...])
    o_ref[0] = jnp.dot(cond, w_ref[0], preferred_element_type=f32) + b_ref[0]


def _ada_mod(c_pad, ada_w, ada_b, tn=1024):
    depth, d, n = ada_w.shape
    rows = c_pad.shape[0]
    tn = _tile(n, tn)
    return pl.pallas_call(
        _ada_kernel,
        grid=(depth, n // tn),
        in_specs=[pl.BlockSpec((rows, d), lambda l, j: (0, 0)),
                  pl.BlockSpec((1, d, tn), lambda l, j: (l, 0, j)),
                  pl.BlockSpec((1, 1, tn), lambda l, j: (l, 0, j))],
        out_specs=pl.BlockSpec((1, rows, tn), lambda l, j: (l, 0, j)),
        out_shape=jax.ShapeDtypeStruct((depth, rows, n), f32),
        compiler_params=_params("arbitrary", "arbitrary"),
        name="ada_mod",
    )(c_pad, ada_w, ada_b.reshape(depth, 1, n))


def _normproj_kernel(x_ref, nw_ref, sc_ref, sh_ref, w_ref, o_ref, h_ref):
    @pl.when(pl.program_id(1) == 0)
    def _():
        x = x_ref[...]
        ms = jnp.mean(x * x, axis=-1, keepdims=True)
        y = x * lax.rsqrt(ms + EPS) * nw_ref[...]
        h_ref[...] = (y * (1.0 + sc_ref[0]) + sh_ref[0]).astype(h_ref.dtype)

    o_ref[...] = jnp.dot(h_ref[...], w_ref[...], preferred_element_type=f32).astype(o_ref.dtype)


def _norm_proj(x2, nw, scale, shift, w, seq, out_dtype, tm, tn):
    t, d = x2.shape
    n = w.shape[1]
    tiles_per_seq = seq // tm
    return pl.pallas_call(
        _normproj_kernel,
        grid=(t // tm, n // tn),
        in_specs=[pl.BlockSpec((tm, d), lambda i, j: (i, 0)),
                  pl.BlockSpec((1, d), lambda i, j: (0, 0)),
                  pl.BlockSpec((1, 1, d), lambda i, j: (i // tiles_per_seq, 0, 0)),
                  pl.BlockSpec((1, 1, d), lambda i, j: (i // tiles_per_seq, 0, 0)),
                  pl.BlockSpec((d, tn), lambda i, j: (0, j))],
        out_specs=pl.BlockSpec((tm, tn), lambda i, j: (i, j)),
        out_shape=jax.ShapeDtypeStruct((t, n), out_dtype),
        scratch_shapes=[pltpu.VMEM((tm, d), bf16)],
        compiler_params=_params("arbitrary", "arbitrary"),
        name="norm_proj",
    )(x2, nw, scale, shift, w)


def _outproj_kernel(a_ref, w_ref, x_ref, g_ref, o_ref):
    acc = jnp.dot(a_ref[...], w_ref[...], preferred_element_type=f32)
    o_ref[...] = x_ref[...] + g_ref[0] * acc


def _out_proj(a, w, x2, gate, seq, tm, tn):
    t, k = a.shape
    n = w.shape[1]
    tiles_per_seq = seq // tm
    return pl.pallas_call(
        _outproj_kernel,
        grid=(t // tm, n // tn),
        in_specs=[pl.BlockSpec((tm, k), lambda i, j: (i, 0)),
                  pl.BlockSpec((k, tn), lambda i, j: (0, j)),
                  pl.BlockSpec((tm, tn), lambda i, j: (i, j)),
                  pl.BlockSpec((1, 1, tn), lambda i, j: (i // tiles_per_seq, 0, j))],
        out_specs=pl.BlockSpec((tm, tn), lambda i, j: (i, j)),
        out_shape=jax.ShapeDtypeStruct((t, n), f32),
        compiler_params=_params("arbitrary", "arbitrary"),
        name="out_proj",
    )(a, w, x2, gate)


def _rope_table_kernel(pos_ref, freq_ref, cos_ref, sin_ref):
    ang = pos_ref[...].astype(f32) * freq_ref[...]
    lane = lax.broadcasted_iota(jnp.int32, ang.shape, 1)
    first_half = (lane & (HEAD_DIM // 2)) == 0
    s = jnp.sin(ang)
    cos_ref[...] = jnp.cos(ang)
    sin_ref[...] = jnp.where(first_half, -s, s)


def _rope_tables(pos_col, freq_row, tm=1024):
    t = pos_col.shape[0]
    tm = min(tm, t)
    return pl.pallas_call(
        _rope_table_kernel,
        grid=(t // tm,),
        in_specs=[pl.BlockSpec((tm, 1), lambda i: (i, 0)),
                  pl.BlockSpec((1, LANES), lambda i: (0, 0))],
        out_specs=[pl.BlockSpec((tm, LANES), lambda i: (i, 0)),
                   pl.BlockSpec((tm, LANES), lambda i: (i, 0))],
        out_shape=[jax.ShapeDtypeStruct((t, LANES), f32)] * 2,
        compiler_params=_params("arbitrary"),
        name="rope_table",
    )(pos_col, freq_row)


def _qkprep_kernel(p_ref, w_ref, cos_ref, sin_ref, o_ref):
    width = p_ref.shape[1]
    r = lax.broadcasted_iota(jnp.int32, (LANES, LANES), 0) // HEAD_DIM
    c = lax.broadcasted_iota(jnp.int32, (LANES, LANES), 1) // HEAD_DIM
    group_mean = jnp.where(r == c, 1.0 / HEAD_DIM, 0.0).astype(bf16)
    cos = cos_ref[...]
    sin = sin_ref[...]
    lane = lax.broadcasted_iota(jnp.int32, cos.shape, 1)
    first_half = (lane & (HEAD_DIM // 2)) == 0
    for j in range(width // LANES):
        sl = slice(j * LANES, (j + 1) * LANES)
        x = p_ref[:, sl].astype(f32)
        ms = jnp.dot((x * x).astype(bf16), group_mean, preferred_element_type=f32)
        y = x * lax.rsqrt(ms + EPS) * w_ref[0, :, sl]
        partner = jnp.where(first_half,
                            pltpu.roll(y, LANES - HEAD_DIM // 2, 1),
                            pltpu.roll(y, HEAD_DIM // 2, 1))
        o_ref[:, sl] = (y * cos + partner * sin).astype(o_ref.dtype)


def _qk_prep(proj, qk_w, cos_t, sin_t, width, tm=512):
    t = proj.shape[0]
    tm = min(tm, t)
    return pl.pallas_call(
        _qkprep_kernel,
        grid=(t // tm, 2),
        in_specs=[pl.BlockSpec((tm, width), lambda i, j: (i, j)),
                  pl.BlockSpec((1, 1, width), lambda i, j: (j, 0, 0)),
                  pl.BlockSpec((tm, LANES), lambda i, j: (i, 0)),
                  pl.BlockSpec((tm, LANES), lambda i, j: (i, 0))],
        out_specs=pl.BlockSpec((tm, width), lambda i, j: (i, j)),
        out_shape=jax.ShapeDtypeStruct((t, 2 * width), bf16),
        compiler_params=_params("arbitrary", "arbitrary"),
        name="qk_prep",
    )(proj, qk_w, cos_t, sin_t)


def _attn_kernel(q_ref, k_ref, v_ref, g_ref, lq1_ref, lk1_ref, lq2_ref, lk2_ref, sw_ref, o_ref,
                 vt_ref, qq_ref, s_ref, acc_ref, *, blk, hp, lambda_init):
    seq = q_ref.shape[0]
    nblk = seq // blk
    lam = (jnp.exp(jnp.sum(lq1_ref[...] * lk1_ref[...], axis=-1, keepdims=True))
           - jnp.exp(jnp.sum(lq2_ref[...] * lk2_ref[...], axis=-1, keepdims=True))
           + lambda_init)
    cols = [slice(hh * LANES, (hh + 1) * LANES) for hh in range(hp)]

    def transpose_v(c, _):
        start = pl.multiple_of(c * blk, blk)
        for hh in range(hp):
            vt_ref[hh, c] = v_ref[pl.ds(start, blk), cols[hh]].astype(f32).T.astype(bf16)
        return 0
    lax.fori_loop(0, nblk, transpose_v, 0)

    d_row = lax.broadcasted_iota(jnp.int32, (LANES, blk), 0)
    key_idx = lax.broadcasted_iota(jnp.int32, (blk, 2 * blk), 0)
    qry_idx = lax.broadcasted_iota(jnp.int32, (blk, 2 * blk), 1) & (blk - 1)
    causal = key_idx <= qry_idx

    def scores(hh, j):
        start = pl.multiple_of(j * blk, blk)
        return jnp.dot(k_ref[pl.ds(start, blk), cols[hh]], qq_ref[hh], preferred_element_type=f32)

    def consume(hh, j, m, l, masked):
        st = s_ref[hh, j & 1]
        if masked:
            st = jnp.where(causal, st, NEG_BIG)
        m_new = jnp.maximum(m, jnp.max(st, axis=0, keepdims=True))
        alpha = jnp.exp2(m - m_new)
        p = jnp.exp2(st - m_new)
        l_new = alpha * l + jnp.sum(p, axis=0, keepdims=True)
        acc_ref[hh] = alpha * acc_ref[hh] + jnp.dot(vt_ref[hh, j], p.astype(bf16),
                                                    preferred_element_type=f32)
        return m_new, l_new

    def q_block(i, _):
        q_start = pl.multiple_of(i * blk, blk)
        for hh in range(hp):
            qt = q_ref[pl.ds(q_start, blk), cols[hh]].astype(f32).T
            qq_ref[hh] = jnp.concatenate([jnp.where(d_row < HEAD_DIM, qt, 0.0),
                                          jnp.where(d_row >= HEAD_DIM, qt, 0.0)], axis=1).astype(bf16)
            acc_ref[hh] = jnp.zeros(acc_ref.shape[1:], f32)
            s_ref[hh, 0] = scores(hh, 0)

        def body(j, carry):
            out = []
            for hh in range(hp):
                out.append(consume(hh, j, *carry[hh], False))
                s_ref[hh, (j + 1) & 1] = scores(hh, j + 1)
            return tuple(out)

        init = tuple((jnp.full((1, 2 * blk), NEG_BIG, f32), jnp.zeros((1, 2 * blk), f32))
                     for _ in range(hp))
        carry = lax.fori_loop(0, i, body, init)

        for hh in range(hp):
            _, l = consume(hh, i, *carry[hh], True)
            ot = acc_ref[hh] / l
            odt = ot[:, :blk] - lam * ot[:, blk:]
            ms = jnp.mean(odt * odt, axis=0, keepdims=True)
            y = (odt * lax.rsqrt(ms + EPS)).T * (sw_ref[...] * (1.0 - lambda_init))
            g = g_ref[pl.ds(q_start, blk), cols[hh]].astype(f32)
            o_ref[pl.ds(q_start, blk), cols[hh]] = (y * _silu(g)).astype(o_ref.dtype)
        return 0

    lax.fori_loop(0, nblk, q_block, 0)


def _diff_attn(qk, proj, lq1, lk1, lq2, lk2, subln_w, *, batch, seq, heads, lambda_init, blk=256, hp=2):
    t = qk.shape[0]
    blk = min(blk, seq)
    hp = math.gcd(hp, heads)
    w = hp * LANES
    ng = heads // hp
    vec = lambda a: a.reshape(1, -1).astype(f32)
    small = pl.BlockSpec((1, HEAD_DIM), lambda b, h: (0, 0))
    kern = functools.partial(_attn_kernel, blk=blk, hp=hp, lambda_init=lambda_init)
    return pl.pallas_call(
        kern,
        grid=(batch, ng),
        in_specs=[pl.BlockSpec((seq, w), lambda b, h: (b, h)),
                  pl.BlockSpec((seq, w), lambda b, h: (b, ng + h)),
                  pl.BlockSpec((seq, w), lambda b, h: (b, 2 * ng + h)),
                  pl.BlockSpec((seq, w), lambda b, h: (b, 3 * ng + h)),
                  small, small, small, small,
                  pl.BlockSpec((1, LANES), lambda b, h: (0, 0))],
        out_specs=pl.BlockSpec((seq, w), lambda b, h: (b, h)),
        out_shape=jax.ShapeDtypeStruct((t, heads * LANES), bf16),
        scratch_shapes=[pltpu.VMEM((hp, seq // blk, LANES, blk), bf16),
                        pltpu.VMEM((hp, LANES, 2 * blk), bf16),
                        pltpu.VMEM((hp, 2, blk, 2 * blk), f32),
                        pltpu.VMEM((hp, LANES, 2 * blk), f32)],
        compiler_params=_params("arbitrary", "arbitrary"),
        name="diff_attn",
    )(qk, qk, proj, proj, vec(lq1), vec(lk1), vec(lq2), vec(lk2), vec(subln_w))


def _expand_heads(a, heads_per_group):
    rows = a.shape[0]
    lane = lax.broadcasted_iota(jnp.int32, (rows, LANES), 1)
    pieces = []
    for j in range(heads_per_group // 2):
        lo = jnp.broadcast_to(a[:, 2 * j:2 * j + 1], (rows, LANES))
        hi = jnp.broadcast_to(a[:, 2 * j + 1:2 * j + 2], (rows, LANES))
        pieces.append(jnp.where(lane < SSM_HEAD_DIM, lo, hi))
    return jnp.concatenate(pieces, axis=1)


def _ssd_kernel(dt_ref, dtb_ref, alog_ref, z_ref, xs_ref, b_ref, c_ref,
                wx_ref, wb_ref, wc_ref, bx_ref, bb_ref, bc_ref, dskip_ref, nw_ref,
                o_ref,
                state_ref, halo_ref, buf_ref, cs_ref, dtc_ref, cst_ref, dtt_ref,
                *, groups, rheads):
    c = pl.program_id(1)
    g = pl.program_id(2)
    L = SSM_CHUNK
    xw = rheads * SSM_HEAD_DIM

    @pl.when(g == 0)
    def _():
        raw = dt_ref[...] + dtb_ref[...]
        dt = jnp.maximum(raw, 0.0) + jnp.log(1.0 + jnp.exp(-jnp.abs(raw)))
        da_t = (dt * (-jnp.exp(alog_ref[...]))).T
        k_idx = lax.broadcasted_iota(jnp.int32, (L, L), 0)
        l_idx = lax.broadcasted_iota(jnp.int32, (L, L), 1)
        upper = jnp.where(k_idx <= l_idx, 1.0, 0.0).astype(f32)
        cs_t = jnp.dot(da_t, upper, precision=lax.Precision.HIGHEST,
                       preferred_element_type=f32)
        cst_ref[...] = cs_t
        dtt_ref[...] = dt.T
        cs = cs_t.T
        for gg in range(groups):
            shift = (LANES - gg * rheads) % LANES
            cs_ref[gg] = pltpu.roll(cs, shift, 1) if shift else cs
            dtc_ref[gg] = pltpu.roll(dt, shift, 1) if shift else dt

    @pl.when(c == 0)
    def _():
        state_ref[g] = jnp.zeros(state_ref.shape[1:], f32)
        halo_ref[g] = jnp.zeros(halo_ref.shape[1:], f32)

    xin = jnp.concatenate([xs_ref[...], b_ref[...], c_ref[...]], axis=1).astype(f32)
    buf_ref[0:CONV_HALO, :] = halo_ref[g]
    buf_ref[CONV_HALO:CONV_HALO + L, :] = xin
    halo_ref[g] = xin[L - CONV_HALO:, :]
    w = jnp.concatenate([wx_ref[...], wb_ref[...], wc_ref[...]], axis=1)
    bias = jnp.concatenate([bx_ref[...], bb_ref[...], bc_ref[...]], axis=1)
    acc = jnp.broadcast_to(bias, xin.shape)
    for k in range(SSM_CONV):
        off = CONV_HALO - (SSM_CONV - 1) + k
        acc = acc + w[k:k + 1, :] * buf_ref[off:off + L, :]
    xbc = _silu(acc)
    xs = xbc[:, :xw]
    bm = xbc[:, xw:xw + SSM_STATE]
    cm = xbc[:, xw + SSM_STATE:]

    row0 = pl.multiple_of(g * rheads, rheads)
    cs_t = cst_ref[pl.ds(row0, rheads), :]
    dt_t = dtt_ref[pl.ds(row0, rheads), :]
    cs_c = cs_ref[g]
    dt_c = dtc_ref[g]

    cb = lax.dot_general(cm.astype(bf16), bm.astype(bf16), (((1,), (1,)), ((), ())),
                         preferred_element_type=f32)
    l_idx = lax.broadcasted_iota(jnp.int32, (L, L), 0)
    s_idx = lax.broadcasted_iota(jnp.int32, (L, L), 1)
    causal = l_idx >= s_idx
    lane = lax.broadcasted_iota(jnp.int32, (L, LANES), 1)
    lo_half = lane < SSM_HEAD_DIM

    y_parts = []
    for j in range(rheads // 2):
        xj = xs[:, j * LANES:(j + 1) * LANES]
        x_lo = jnp.where(lo_half, xj, 0.0).astype(bf16)
        x_hi = jnp.where(lo_half, 0.0, xj).astype(bf16)
        yj = None
        for r, xr in ((2 * j, x_lo), (2 * j + 1, x_hi)):
            seg = jnp.broadcast_to(cs_c[:, r:r + 1], (L, L)) - cs_t[r:r + 1, :]
            decay = jnp.exp(jnp.where(causal, seg, -jnp.inf))
            mr = (cb * decay * dt_t[r:r + 1, :]).astype(bf16)
            part = jnp.dot(mr, xr, preferred_element_type=f32)
            yj = part if yj is None else yj + part
        y_parts.append(yj)
    y_diag = jnp.concatenate(y_parts, axis=1)

    prev = state_ref[g]
    cs_last = cs_c[L - 1:L, :]
    e_cs = _expand_heads(jnp.exp(cs_c), rheads)
    y_off = jnp.dot(cm.astype(bf16), prev.astype(bf16), preferred_element_type=f32) * e_cs

    wgt = _expand_heads(jnp.exp(cs_last - cs_c) * dt_c, rheads)
    new = jnp.dot(bm.T.astype(bf16), (xs * wgt).astype(bf16), preferred_element_type=f32)
    chunk_decay = _expand_heads(jnp.broadcast_to(jnp.exp(cs_last), (8, LANES)), rheads)[0:1, :]
    state_ref[g] = prev * chunk_decay + new

    y = y_diag + y_off + xs * dskip_ref[...]
    y = y * _silu(z_ref[...].astype(f32))
    ms = jnp.mean(y * y, axis=-1, keepdims=True)
    o_ref[...] = (y * lax.rsqrt(ms + EPS) * nw_ref[...]).astype(o_ref.dtype)


def _ssd(proj, dt_raw, dt_bias, a_log, conv_w, conv_b, d_skip, norm_w, *, batch, seq, d_inner, groups):
    t = proj.shape[0]
    L = SSM_CHUNK
    nc = seq // L
    heads = d_inner // SSM_HEAD_DIM
    rheads = heads // groups
    xw = rheads * SSM_HEAD_DIM
    n = SSM_STATE
    xs_blk0 = d_inner // xw
    b_blk0 = 2 * d_inner // n
    c_blk0 = b_blk0 + groups
    cb_blk0 = d_inner // n
    cc_blk0 = cb_blk0 + groups
    row = lambda b, c, g: b * nc + c
    pad = lambda a: jnp.pad(a.reshape(1, -1).astype(f32), ((0, 0), (0, LANES - heads)))
    kern = functools.partial(_ssd_kernel, groups=groups, rheads=rheads)
    return pl.pallas_call(
        kern,
        grid=(batch, nc, groups),
        in_specs=[pl.BlockSpec((L, LANES), lambda b, c, g: (row(b, c, g), 0)),
                  pl.BlockSpec((1, LANES), lambda b, c, g: (0, 0)),
                  pl.BlockSpec((1, LANES), lambda b, c, g: (0, 0)),
                  pl.BlockSpec((L, xw), lambda b, c, g: (row(b, c, g), g)),
                  pl.BlockSpec((L, xw), lambda b, c, g: (row(b, c, g), xs_blk0 + g)),
                  pl.BlockSpec((L, n), lambda b, c, g: (row(b, c, g), b_blk0 + g)),
                  pl.BlockSpec((L, n), lambda b, c, g: (row(b, c, g), c_blk0 + g)),
                  pl.BlockSpec((SSM_CONV, xw), lambda b, c, g: (0, g)),
                  pl.BlockSpec((SSM_CONV, n), lambda b, c, g: (0, cb_blk0 + g)),
                  pl.BlockSpec((SSM_CONV, n), lambda b, c, g: (0, cc_blk0 + g)),
                  pl.BlockSpec((1, xw), lambda b, c, g: (0, g)),
                  pl.BlockSpec((1, n), lambda b, c, g: (0, cb_blk0 + g)),
                  pl.BlockSpec((1, n), lambda b, c, g: (0, cc_blk0 + g)),
                  pl.BlockSpec((1, xw), lambda b, c, g: (0, g)),
                  pl.BlockSpec((1, xw), lambda b, c, g: (0, g))],
        out_specs=pl.BlockSpec((L, xw), lambda b, c, g: (row(b, c, g), g)),
        out_shape=jax.ShapeDtypeStruct((t, d_inner), bf16),
        scratch_shapes=[pltpu.VMEM((groups, n, xw), f32),
                        pltpu.VMEM((groups, CONV_HALO, xw + 2 * n), f32),
                        pltpu.VMEM((CONV_HALO + L, xw + 2 * n), f32),
                        pltpu.VMEM((groups, L, LANES), f32),
                        pltpu.VMEM((groups, L, LANES), f32),
                        pltpu.VMEM((LANES, L), f32),
                        pltpu.VMEM((LANES, L), f32)],
        compiler_params=_params("arbitrary", "arbitrary", "arbitrary"),
        name="ssd_chunk",
    )(dt_raw, pad(dt_bias), pad(a_log), proj, proj, proj, proj,
      conv_w, conv_w, conv_w, conv_b.reshape(1, -1), conv_b.reshape(1, -1), conv_b.reshape(1, -1),
      jnp.repeat(d_skip, SSM_HEAD_DIM).reshape(1, -1), norm_w.reshape(1, -1))


def kernel(x, c, positions, norm_w, ada_w, ada_b, attn_w_in, attn_q_norm, attn_k_norm, attn_lambda_q1, attn_lambda_k1, attn_lambda_q2, attn_lambda_k2, attn_subln_w, attn_w_out, ssm_w_in, ssm_conv_w, ssm_conv_b, ssm_dt_bias, ssm_A_log, ssm_D, ssm_norm_w, ssm_w_out):
    batch, seq, d = x.shape
    depth = norm_w.shape[0]
    t = batch * seq
    heads = d // (2 * HEAD_DIM)
    qk_width = heads * 2 * HEAD_DIM
    d_inner = ssm_w_out.shape[1]
    conv_dim = ssm_conv_w.shape[2]
    groups = (conv_dim - d_inner) // (2 * SSM_STATE)
    ssm_heads = ssm_dt_bias.shape[1]
    tm = min(1024, seq)

    c_pad = jnp.pad(c, ((0, 8 - batch), (0, 0)))
    mod = _ada_mod(c_pad, ada_w, ada_b)[:, :batch]
    shift, scale, gate = (mod[:, :, i * d:(i + 1) * d].reshape(depth, batch, 1, d) for i in range(3))

    inv_freq = ROPE_THETA ** (-jnp.arange(0, HEAD_DIM, 2, dtype=f32) / HEAD_DIM)
    freq_row = jnp.tile(inv_freq, LANES // (HEAD_DIM // 2)).reshape(1, LANES)
    cos_t, sin_t = _rope_tables(positions.reshape(t, 1), freq_row)

    x2 = x.reshape(t, d)
    for layer in range(depth):
        j = layer // 2
        nw = norm_w[layer].reshape(1, d)
        if layer % 2 == 0:
            lambda_init = 0.8 - 0.6 * math.exp(-0.3 * layer)
            proj = _norm_proj(x2, nw, scale[layer], shift[layer], attn_w_in[j].astype(bf16),
                              seq, bf16, tm, _tile(attn_w_in.shape[2], 1024))
            rep = qk_width // HEAD_DIM
            qk_w = jnp.stack([jnp.tile(attn_q_norm[j], rep) * (HEAD_DIM ** -0.5 * math.log2(math.e)),
                              jnp.tile(attn_k_norm[j], rep)]).reshape(2, 1, qk_width)
            qk = _qk_prep(proj, qk_w, cos_t, sin_t, qk_width)
            o = _diff_attn(qk, proj, attn_lambda_q1[j], attn_lambda_k1[j], attn_lambda_q2[j],
                           attn_lambda_k2[j], attn_subln_w[j], batch=batch, seq=seq, heads=heads,
                           lambda_init=lambda_init)
            x2 = _out_proj(o, attn_w_out[j].astype(bf16), x2, gate[layer], seq, tm, _tile(d, 1024))
        else:
            main = d_inner + conv_dim
            w_main = ssm_w_in[j][:, :main].astype(bf16)
            w_dt = jnp.pad(ssm_w_in[j][:, main:], ((0, 0), (0, LANES - ssm_heads))).astype(bf16)
            proj = _norm_proj(x2, nw, scale[layer], shift[layer], w_main, seq, bf16, tm, _tile(main, 1024))
            dt_raw = _norm_proj(x2, nw, scale[layer], shift[layer], w_dt, seq, f32, tm, LANES)
            y = _ssd(proj, dt_raw, ssm_dt_bias[j], ssm_A_log[j], ssm_conv_w[j], ssm_conv_b[j],
                     ssm_D[j], ssm_norm_w[j], batch=batch, seq=seq, d_inner=d_inner, groups=groups)
            x2 = _out_proj(y, ssm_w_out[j].astype(bf16), x2, gate[layer], seq, tm, _tile(d, 512))
    return x2.reshape(batch, seq, d)
```

```python
import functools
import math

import numpy as np
import jax
import jax.numpy as jnp
from jax import lax
from jax.experimental import pallas as pl
from jax.experimental.pallas import tpu as pltpu

EPS = 1e-6
ROPE_THETA = 10000.0
LANES = 128
BF16_ROWS = 16
HEAD_DIM = 64
SSM_HEAD_DIM = 64
SSM_STATE = 128
SSM_CHUNK = 128
SSM_CONV = 4
VMEM_LIMIT_BYTES = 56 * 1024 * 1024
NEG_BIG = -1e30

f32 = jnp.float32
bf16 = jnp.bfloat16


def _params(*sem):
    return pltpu.CompilerParams(dimension_semantics=sem, vmem_limit_bytes=VMEM_LIMIT_BYTES)


def _silu(x):
    return x * jax.nn.sigmoid(x)


def _tile(n, target):
    if n <= target:
        return n
    best = None
    for cand in range(LANES, target + 1, LANES):
        if n % cand == 0:
            best = cand
    assert best is not None, (n, target)
    return best


def _ada_kernel(c_ref, w_ref, b_ref, o_ref):
    cond = _silu(c_ref[...])
    o_ref[0] = jnp.dot(cond, w_ref[0], preferred_element_type=f32) + b_ref[0]


def _ada_mod(c_pad, ada_w, ada_b, tn=1024):
    depth, d, n = ada_w.shape
    rows = c_pad.shape[0]
    tn = _tile(n, tn)
    return pl.pallas_call(
        _ada_kernel,
        grid=(depth, n // tn),
        in_specs=[pl.BlockSpec((rows, d), lambda l, j: (0, 0)),
                  pl.BlockSpec((1, d, tn), lambda l, j: (l, 0, j)),
                  pl.BlockSpec((1, 1, tn), lambda l, j: (l, 0, j))],
        out_specs=pl.BlockSpec((1, rows, tn), lambda l, j: (l, 0, j)),
        out_shape=jax.ShapeDtypeStruct((depth, rows, n), f32),
        compiler_params=_params("arbitrary", "arbitrary"),
        name="ada_mod",
    )(c_pad, ada_w, ada_b.reshape(depth, 1, n))


def _normproj_kernel(x_ref, nw_ref, sc_ref, sh_ref, w_ref, o_ref, h_ref):
    @pl.when(pl.program_id(1) == 0)
    def _():
        x = x_ref[...]
        ms = jnp.mean(x * x, axis=-1, keepdims=True)
        y = x * lax.rsqrt(ms + EPS) * nw_ref[...]
        h_ref[...] = (y * (1.0 + sc_ref[0]) + sh_ref[0]).astype(h_ref.dtype)

    o_ref[...] = jnp.dot(h_ref[...], w_ref[0].astype(bf16),
                         preferred_element_type=f32).astype(o_ref.dtype)


def _norm_proj(x2, nw, scale, shift, w, w_layer, n_out, seq, out_dtype, tm, tn):
    t, d = x2.shape
    tiles_per_seq = seq // tm
    return pl.pallas_call(
        _normproj_kernel,
        grid=(t // tm, n_out // tn),
        in_specs=[pl.BlockSpec((tm, d), lambda i, j: (i, 0)),
                  pl.BlockSpec((1, d), lambda i, j: (0, 0)),
                  pl.BlockSpec((1, 1, d), lambda i, j: (i // tiles_per_seq, 0, 0)),
                  pl.BlockSpec((1, 1, d), lambda i, j: (i // tiles_per_seq, 0, 0)),
                  pl.BlockSpec((1, d, tn), lambda i, j: (w_layer, 0, j))],
        out_specs=pl.BlockSpec((tm, tn), lambda i, j: (i, j)),
        out_shape=jax.ShapeDtypeStruct((t, n_out), out_dtype),
        scratch_shapes=[pltpu.VMEM((tm, d), bf16)],
        compiler_params=_params("arbitrary", "arbitrary"),
        name="norm_proj",
    )(x2, nw, scale, shift, w)


def _outproj_kernel(a_ref, w_ref, x_ref, g_ref, o_ref):
    acc = jnp.dot(a_ref[...], w_ref[0].astype(bf16), preferred_element_type=f32)
    o_ref[...] = x_ref[...] + g_ref[0] * acc


def _out_proj(a, w, w_layer, x2, gate, seq, tm, tn):
    t, k = a.shape
    n = w.shape[2]
    tiles_per_seq = seq // tm
    return pl.pallas_call(
        _outproj_kernel,
        grid=(t // tm, n // tn),
        in_specs=[pl.BlockSpec((tm, k), lambda i, j: (i, 0)),
                  pl.BlockSpec((1, k, tn), lambda i, j: (w_layer, 0, j)),
                  pl.BlockSpec((tm, tn), lambda i, j: (i, j)),
                  pl.BlockSpec((1, 1, tn), lambda i, j: (i // tiles_per_seq, 0, j))],
        out_specs=pl.BlockSpec((tm, tn), lambda i, j: (i, j)),
        out_shape=jax.ShapeDtypeStruct((t, n), f32),
        compiler_params=_params("arbitrary", "arbitrary"),
        name="out_proj",
    )(a, w, x2, gate)


def _rope_table_kernel(pos_ref, freq_ref, cos_ref, sin_ref):
    ang = pos_ref[...].astype(f32) * freq_ref[...]
    lane = lax.broadcasted_iota(jnp.int32, ang.shape, 1)
    first_half = (lane & (HEAD_DIM // 2)) == 0
    s = jnp.sin(ang)
    cos_ref[...] = jnp.cos(ang)
    sin_ref[...] = jnp.where(first_half, -s, s)


def _rope_tables(pos_col, freq_row, tm=1024):
    t = pos_col.shape[0]
    tm = min(tm, t)
    return pl.pallas_call(
        _rope_table_kernel,
        grid=(t // tm,),
        in_specs=[pl.BlockSpec((tm, 1), lambda i: (i, 0)),
                  pl.BlockSpec((1, LANES), lambda i: (0, 0))],
        out_specs=[pl.BlockSpec((tm, LANES), lambda i: (i, 0)),
                   pl.BlockSpec((tm, LANES), lambda i: (i, 0))],
        out_shape=[jax.ShapeDtypeStruct((t, LANES), f32)] * 2,
        compiler_params=_params("arbitrary"),
        name="rope_table",
    )(pos_col, freq_row)


def _qkprep_kernel(p_ref, w_ref, cos_ref, sin_ref, o_ref):
    width = p_ref.shape[1]
    r = lax.broadcasted_iota(jnp.int32, (LANES, LANES), 0) // HEAD_DIM
    c = lax.broadcasted_iota(jnp.int32, (LANES, LANES), 1) // HEAD_DIM
    group_mean = jnp.where(r == c, 1.0 / HEAD_DIM, 0.0).astype(bf16)
    cos = cos_ref[...]
    sin = sin_ref[...]
    lane = lax.broadcasted_iota(jnp.int32, cos.shape, 1)
    first_half = (lane & (HEAD_DIM // 2)) == 0
    for j in range(width // LANES):
        sl = slice(j * LANES, (j + 1) * LANES)
        x = p_ref[:, sl].astype(f32)
        ms = jnp.dot((x * x).astype(bf16), group_mean, preferred_element_type=f32)
        y = x * lax.rsqrt(ms + EPS) * w_ref[0, :, sl]
        partner = jnp.where(first_half,
                            pltpu.roll(y, LANES - HEAD_DIM // 2, 1),
                            pltpu.roll(y, HEAD_DIM // 2, 1))
        o_ref[:, sl] = (y * cos + partner * sin).astype(o_ref.dtype)


def _qk_prep(proj, qk_w, cos_t, sin_t, width, tm=512):
    t = proj.shape[0]
    tm = min(tm, t)
    return pl.pallas_call(
        _qkprep_kernel,
        grid=(t // tm, 2),
        in_specs=[pl.BlockSpec((tm, width), lambda i, j: (i, j)),
                  pl.BlockSpec((1, 1, width), lambda i, j: (j, 0, 0)),
                  pl.BlockSpec((tm, LANES), lambda i, j: (i, 0)),
                  pl.BlockSpec((tm, LANES), lambda i, j: (i, 0))],
        out_specs=pl.BlockSpec((tm, width), lambda i, j: (i, j)),
        out_shape=jax.ShapeDtypeStruct((t, 2 * width), bf16),
        compiler_params=_params("arbitrary", "arbitrary"),
        name="qk_prep",
    )(proj, qk_w, cos_t, sin_t)


def _attn_kernel(q_ref, k_ref, v_ref, g_ref, lq1_ref, lk1_ref, lq2_ref, lk2_ref, sw_ref, bias_ref, o_ref,
                 vt_ref, qq_ref, s_ref, acc_ref, *, bq, bk, hp, lambda_init):
    seq = q_ref.shape[0]
    nslot = bq // bk
    lam = (jnp.exp(jnp.sum(lq1_ref[...] * lk1_ref[...], axis=-1, keepdims=True))
           - jnp.exp(jnp.sum(lq2_ref[...] * lk2_ref[...], axis=-1, keepdims=True))
           + lambda_init)
    cols = [slice(hh * LANES, (hh + 1) * LANES) for hh in range(hp)]

    def transpose_v(c, _):
        start = pl.multiple_of(c * bk, bk)
        for hh in range(hp):
            vt_ref[hh, c] = v_ref[pl.ds(start, bk), cols[hh]].astype(f32).T.astype(bf16)
        return 0
    lax.fori_loop(0, seq // bk, transpose_v, 0)

    d_row = lax.broadcasted_iota(jnp.int32, (LANES, bq), 0)

    def scores(hh, j):
        start = pl.multiple_of(j * bk, bk)
        return jnp.dot(k_ref[pl.ds(start, bk), cols[hh]], qq_ref[hh], preferred_element_type=f32)

    def consume(hh, slot, j, m, l, masked):
        st = s_ref[hh, slot]
        if masked:
            st = st + bias_ref[slot]
        m_new = jnp.maximum(m, jnp.max(st, axis=0, keepdims=True))
        alpha = jnp.exp2(m - m_new)
        p = jnp.exp2(st - m_new)
        l_new = alpha * l + jnp.sum(p, axis=0, keepdims=True)
        acc_ref[hh] = alpha * acc_ref[hh] + jnp.dot(vt_ref[hh, j], p.astype(bf16),
                                                    preferred_element_type=f32)
        return m_new, l_new

    def q_block(i, _):
        q_start = pl.multiple_of(i * bq, bq)
        for hh in range(hp):
            qt = q_ref[pl.ds(q_start, bq), cols[hh]].astype(f32).T
            qq_ref[hh] = jnp.concatenate([jnp.where(d_row < HEAD_DIM, qt, 0.0),
                                          jnp.where(d_row >= HEAD_DIM, qt, 0.0)], axis=1).astype(bf16)
            acc_ref[hh] = jnp.zeros(acc_ref.shape[1:], f32)
            for slot in range(nslot):
                s_ref[hh, slot] = scores(hh, slot)

        def body(jj, carry):
            out = []
            for hh in range(hp):
                m, l = carry[hh]
                for slot in range(nslot):
                    m, l = consume(hh, slot, jj * nslot + slot, m, l, False)
                    s_ref[hh, slot] = scores(hh, (jj + 1) * nslot + slot)
                out.append((m, l))
            return tuple(out)

        init = tuple((jnp.full((1, 2 * bq), NEG_BIG, f32), jnp.zeros((1, 2 * bq), f32))
                     for _ in range(hp))
        carry = lax.fori_loop(0, i, body, init)

        for hh in range(hp):
            m, l = carry[hh]
            for slot in range(nslot):
                m, l = consume(hh, slot, i * nslot + slot, m, l, True)
            ot = acc_ref[hh] / l
            odt = ot[:, :bq] - lam * ot[:, bq:]
            ms = jnp.mean(odt * odt, axis=0, keepdims=True)
            y = (odt * lax.rsqrt(ms + EPS)).T * (sw_ref[...] * (1.0 - lambda_init))
            g = g_ref[pl.ds(q_start, bq), cols[hh]].astype(f32)
            o_ref[pl.ds(q_start, bq), cols[hh]] = (y * _silu(g)).astype(o_ref.dtype)
        return 0

    lax.fori_loop(0, seq // bq, q_block, 0)


def _diff_attn(qk, proj, lq1, lk1, lq2, lk2, subln_w, *, batch, seq, heads, lambda_init,
               bq=512, bk=256, hp=2):
    t = qk.shape[0]
    bq = min(bq, seq)
    bk = min(bk, bq)
    hp = math.gcd(hp, heads)
    nslot = bq // bk
    w = hp * LANES
    ng = heads // hp
    key = np.arange(bq)[:, None]
    qry = np.arange(bq)[None, :]
    tri = np.where(key <= qry, 0.0, NEG_BIG).astype(np.float32)
    bias = jnp.asarray(np.concatenate([tri, tri], axis=1).reshape(nslot, bk, 2 * bq))
    vec = lambda a: a.reshape(1, -1).astype(f32)
    small = pl.BlockSpec((1, HEAD_DIM), lambda b, h: (0, 0))
    kern = functools.partial(_attn_kernel, bq=bq, bk=bk, hp=hp, lambda_init=lambda_init)
    return pl.pallas_call(
        kern,
        grid=(batch, ng),
        in_specs=[pl.BlockSpec((seq, w), lambda b, h: (b, h)),
                  pl.BlockSpec((seq, w), lambda b, h: (b, ng + h)),
                  pl.BlockSpec((seq, w), lambda b, h: (b, 2 * ng + h)),
                  pl.BlockSpec((seq, w), lambda b, h: (b, 3 * ng + h)),
                  small, small, small, small,
                  pl.BlockSpec((1, LANES), lambda b, h: (0, 0)),
                  pl.BlockSpec((nslot, bk, 2 * bq), lambda b, h: (0, 0, 0))],
        out_specs=pl.BlockSpec((seq, w), lambda b, h: (b, h)),
        out_shape=jax.ShapeDtypeStruct((t, heads * LANES), bf16),
        scratch_shapes=[pltpu.VMEM((hp, seq // bk, LANES, bk), bf16),
                        pltpu.VMEM((hp, LANES, 2 * bq), bf16),
                        pltpu.VMEM((hp, nslot, bk, 2 * bq), f32),
                        pltpu.VMEM((hp, LANES, 2 * bq), f32)],
        compiler_params=_params("arbitrary", "arbitrary"),
        name="diff_attn",
    )(qk, qk, proj, proj, vec(lq1), vec(lk1), vec(lq2), vec(lk2), vec(subln_w), bias)


def _ssd_constants(rheads):
    L, P = SSM_CHUNK, SSM_HEAD_DIM
    k = np.arange(LANES)[:, None]
    t_idx = np.arange(L)[:, None]
    j_idx = np.arange(2 * L)[None, :]
    shifts = np.stack([(j_idx == L + t_idx - (SSM_CONV - 1 - tap)) for tap in range(SSM_CONV - 1)])
    col = np.arange(rheads * LANES)[None, :]
    seg = (k < 3 * rheads) & (k % rheads == col // LANES)
    col2 = np.arange(2 * rheads * P)[None, :]
    half = rheads * P
    exp = np.where(col2 < half,
                   (k < 2 * rheads) & (k % rheads == col2 // P),
                   (k >= 2 * rheads) & (k < 4 * rheads) & (k % rheads == (col2 - half) // P))
    upper = (np.arange(L)[:, None] <= np.arange(L)[None, :])
    return (jnp.asarray(shifts, dtype=bf16), jnp.asarray(seg, dtype=bf16), jnp.asarray(exp, dtype=bf16),
            jnp.asarray(upper, dtype=f32))


def _split_bf16(a, pieces):
    out, rem = [], a
    for _ in range(pieces):
        p = rem.astype(bf16).astype(f32)
        out.append(p)
        rem = rem - p
    return out


def _ssd_kernel(dt_ref, dtb_ref, alog_ref, z_ref, xs_ref, b_ref, c_ref,
                wx_ref, wb_ref, wc_ref, bx_ref, bb_ref, bc_ref, dskip_ref, nw_ref,
                shift_ref, segoh_ref, expoh_ref, upper_ref,
                o_ref,
                state_ref, halo_ref, stack_ref, cs_ref, dtc_ref, cst_ref, dtt_ref,
                *, groups, rheads, gp):
    c = pl.program_id(1)
    pg = pl.program_id(2)
    L = SSM_CHUNK
    R = rheads
    xw = rheads * SSM_HEAD_DIM
    n = SSM_STATE
    halo = BF16_ROWS

    @pl.when(pg == 0)
    def _():
        raw = dt_ref[...] + dtb_ref[...]
        dt = jnp.maximum(raw, 0.0) + jnp.log(1.0 + jnp.exp(-jnp.abs(raw)))
        da_t = (dt * (-jnp.exp(alog_ref[...]))).T
        cs_t = jnp.dot(da_t, upper_ref[...], precision=lax.Precision.HIGHEST,
                       preferred_element_type=f32)
        cst_ref[...] = cs_t
        dtt_ref[...] = dt.T
        cs = cs_t.T
        for gg in range(groups):
            shift = (LANES - gg * rheads) % LANES
            cs_ref[gg] = pltpu.roll(cs, shift, 1) if shift else cs
            dtc_ref[gg] = pltpu.roll(dt, shift, 1) if shift else dt

    @pl.when((c == 0) & (pg == 0))
    def _():
        stack_ref[:, 0:L, :] = jnp.zeros((gp, L, stack_ref.shape[2]), bf16)

    @pl.when(c == 0)
    def _():
        state_ref[:, pg] = jnp.zeros((gp,) + state_ref.shape[2:], f32)
        halo_ref[:, pg] = jnp.zeros((gp,) + halo_ref.shape[2:], bf16)

    for gi in range(gp):
        _ssd_group(gi, pg, z_ref, xs_ref, b_ref, c_ref, wx_ref, wb_ref, wc_ref,
                   bx_ref, bb_ref, bc_ref, dskip_ref, nw_ref, shift_ref, segoh_ref, expoh_ref,
                   o_ref, state_ref, halo_ref, stack_ref, cs_ref, dtc_ref, cst_ref, dtt_ref, rheads, gp)


def _ssd_group(gi, pg, z_ref, xs_ref, b_ref, c_ref, wx_ref, wb_ref, wc_ref, bx_ref, bb_ref, bc_ref,
               dskip_ref, nw_ref, shift_ref, segoh_ref, expoh_ref, o_ref,
               state_ref, halo_ref, stack_ref, cs_ref, dtc_ref, cst_ref, dtt_ref, rheads, gp):
    L = SSM_CHUNK
    R = rheads
    xw = rheads * SSM_HEAD_DIM
    n = SSM_STATE
    halo = BF16_ROWS
    g = pg * gp + gi
    xsl = slice(gi * xw, (gi + 1) * xw)
    nsl = slice(gi * n, (gi + 1) * n)

    xin = jnp.concatenate([xs_ref[:, xsl], b_ref[:, nsl], c_ref[:, nsl]], axis=1)
    stack_ref[gi, L - halo:L, :] = halo_ref[gi, pg]
    stack_ref[gi, L:2 * L, :] = xin
    halo_ref[gi, pg] = xin[L - halo:, :]
    stacked = stack_ref[gi]
    w = jnp.concatenate([wx_ref[:, xsl], wb_ref[:, nsl], wc_ref[:, nsl]], axis=1)
    bias = jnp.concatenate([bx_ref[:, xsl], bb_ref[:, nsl], bc_ref[:, nsl]], axis=1)
    acc = bias + w[SSM_CONV - 1:SSM_CONV, :] * xin.astype(f32)
    for tap in range(SSM_CONV - 1):
        acc = acc + w[tap:tap + 1, :] * jnp.dot(shift_ref[tap], stacked, preferred_element_type=f32)
    xbc = _silu(acc)
    xs = xbc[:, :xw]
    bm = xbc[:, xw:xw + SSM_STATE]
    cm = xbc[:, xw + SSM_STATE:]
    cm16 = cm.astype(bf16)

    row0 = pl.multiple_of(g * rheads, rheads)
    cs_t = cst_ref[pl.ds(row0, rheads), :]
    dt_t = dtt_ref[pl.ds(row0, rheads), :]
    cs_c = cs_ref[g]
    dt_c = dtc_ref[g]
    lane = lax.broadcasted_iota(jnp.int32, (L, LANES), 1)

    hi, mid, lo = _split_bf16(cs_c, 3)
    packed = jnp.where(lane < R, hi, jnp.where(lane < 2 * R, pltpu.roll(mid, R, 1), pltpu.roll(lo, 2 * R, 1)))
    seg_col = jnp.dot(packed.astype(bf16), segoh_ref[...], preferred_element_type=f32)

    cb = lax.dot_general(cm16, bm.astype(bf16), (((1,), (1,)), ((), ())),
                         preferred_element_type=f32)
    l_idx = lax.broadcasted_iota(jnp.int32, (L, L), 0)
    s_idx = lax.broadcasted_iota(jnp.int32, (L, L), 1)
    causal = l_idx >= s_idx
    lo_half = lane < SSM_HEAD_DIM

    y_parts = []
    for j in range(rheads // 2):
        xj = xs[:, j * LANES:(j + 1) * LANES]
        x_lo = jnp.where(lo_half, xj, 0.0).astype(bf16)
        x_hi = jnp.where(lo_half, 0.0, xj).astype(bf16)
        yj = None
        for r, xr in ((2 * j, x_lo), (2 * j + 1, x_hi)):
            seg = seg_col[:, r * LANES:(r + 1) * LANES] - cs_t[r:r + 1, :]
            decay = jnp.exp(jnp.where(causal, seg, -jnp.inf))
            mr = (cb * decay * dt_t[r:r + 1, :]).astype(bf16)
            part = jnp.dot(mr, xr, preferred_element_type=f32)
            yj = part if yj is None else yj + part
        y_parts.append(yj)
    y_diag = jnp.concatenate(y_parts, axis=1)

    cs_last = cs_c[L - 1:L, :]
    e_hi, e_lo = _split_bf16(jnp.exp(cs_c), 2)
    w_hi, w_lo = _split_bf16(jnp.exp(cs_last - cs_c) * dt_c, 2)
    packed2 = jnp.where(lane < R, e_hi,
                        jnp.where(lane < 2 * R, pltpu.roll(e_lo, R, 1),
                                  jnp.where(lane < 3 * R, pltpu.roll(w_hi, 2 * R, 1),
                                            pltpu.roll(w_lo, 3 * R, 1))))
    spread = jnp.dot(packed2.astype(bf16), expoh_ref[...], preferred_element_type=f32)
    e_cs = spread[:, :xw]
    wgt = spread[:, xw:]

    prev = state_ref[gi, pg]
    y_off = jnp.dot(cm16, prev.astype(bf16), preferred_element_type=f32) * e_cs
    new = jnp.dot(bm.T.astype(bf16), (xs * wgt).astype(bf16), preferred_element_type=f32)
    state_ref[gi, pg] = prev * e_cs[L - 1:L, :] + new

    y = y_diag + y_off + xs * dskip_ref[:, xsl]
    y = y * _silu(z_ref[:, xsl].astype(f32))
    ms = jnp.mean(y * y, axis=-1, keepdims=True)
    o_ref[:, xsl] = (y * lax.rsqrt(ms + EPS) * nw_ref[:, xsl]).astype(o_ref.dtype)


def _ssd(proj, dt_raw, dt_bias, a_log, conv_w, conv_b, d_skip, norm_w, *, batch, seq, d_inner, groups, gp=2):
    t = proj.shape[0]
    L = SSM_CHUNK
    nc = seq // L
    heads = d_inner // SSM_HEAD_DIM
    rheads = heads // groups
    assert rheads % 2 == 0 and 4 * rheads <= LANES
    gp = math.gcd(gp, groups)
    xw = rheads * SSM_HEAD_DIM
    n = SSM_STATE
    cw = xw + 2 * n
    bw, bn = gp * xw, gp * n
    assert d_inner % bw == 0 and (2 * d_inner) % bn == 0
    xs_blk0 = d_inner // bw
    b_blk0 = 2 * d_inner // bn
    c_blk0 = b_blk0 + groups // gp
    cb_blk0 = d_inner // bn
    cc_blk0 = cb_blk0 + groups // gp
    row = lambda b, c, g: b * nc + c
    pad = lambda a: jnp.pad(a.reshape(1, -1).astype(f32), ((0, 0), (0, LANES - heads)))
    const2 = lambda shape: pl.BlockSpec(shape, lambda b, c, g: (0, 0))
    shifts, seg_oh, exp_oh, upper = _ssd_constants(rheads)
    kern = functools.partial(_ssd_kernel, groups=groups, rheads=rheads, gp=gp)
    return pl.pallas_call(
        kern,
        grid=(batch, nc, groups // gp),
        in_specs=[pl.BlockSpec((L, LANES), lambda b, c, g: (row(b, c, g), 0)),
                  const2((1, LANES)),
                  const2((1, LANES)),
                  pl.BlockSpec((L, bw), lambda b, c, g: (row(b, c, g), g)),
                  pl.BlockSpec((L, bw), lambda b, c, g: (row(b, c, g), xs_blk0 + g)),
                  pl.BlockSpec((L, bn), lambda b, c, g: (row(b, c, g), b_blk0 + g)),
                  pl.BlockSpec((L, bn), lambda b, c, g: (row(b, c, g), c_blk0 + g)),
                  pl.BlockSpec((SSM_CONV, bw), lambda b, c, g: (0, g)),
                  pl.BlockSpec((SSM_CONV, bn), lambda b, c, g: (0, cb_blk0 + g)),
                  pl.BlockSpec((SSM_CONV, bn), lambda b, c, g: (0, cc_blk0 + g)),
                  pl.BlockSpec((1, bw), lambda b, c, g: (0, g)),
                  pl.BlockSpec((1, bn), lambda b, c, g: (0, cb_blk0 + g)),
                  pl.BlockSpec((1, bn), lambda b, c, g: (0, cc_blk0 + g)),
                  pl.BlockSpec((1, bw), lambda b, c, g: (0, g)),
                  pl.BlockSpec((1, bw), lambda b, c, g: (0, g)),
                  pl.BlockSpec((SSM_CONV - 1, L, 2 * L), lambda b, c, g: (0, 0, 0)),
                  const2((LANES, rheads * LANES)),
                  const2((LANES, 2 * xw)),
                  const2((L, L))],
        out_specs=pl.BlockSpec((L, bw), lambda b, c, g: (row(b, c, g), g)),
        out_shape=jax.ShapeDtypeStruct((t, d_inner), bf16),
        scratch_shapes=[pltpu.VMEM((gp, groups // gp, n, xw), f32),
                        pltpu.VMEM((gp, groups // gp, BF16_ROWS, cw), bf16),
                        pltpu.VMEM((gp, 2 * L, cw), bf16),
                        pltpu.VMEM((groups, L, LANES), f32),
                        pltpu.VMEM((groups, L, LANES), f32),
                        pltpu.VMEM((LANES, L), f32),
                        pltpu.VMEM((LANES, L), f32)],
        compiler_params=_params("arbitrary", "arbitrary", "arbitrary"),
        name="ssd_chunk",
    )(dt_raw, pad(dt_bias), pad(a_log), proj, proj, proj, proj,
      conv_w, conv_w, conv_w, conv_b.reshape(1, -1), conv_b.reshape(1, -1), conv_b.reshape(1, -1),
      jnp.repeat(d_skip, SSM_HEAD_DIM).reshape(1, -1), norm_w.reshape(1, -1),
      shifts, seg_oh, exp_oh, upper)


def kernel(x, c, positions, norm_w, ada_w, ada_b, attn_w_in, attn_q_norm, attn_k_norm, attn_lambda_q1, attn_lambda_k1, attn_lambda_q2, attn_lambda_k2, attn_subln_w, attn_w_out, ssm_w_in, ssm_conv_w, ssm_conv_b, ssm_dt_bias, ssm_A_log, ssm_D, ssm_norm_w, ssm_w_out):
    batch, seq, d = x.shape
    depth = norm_w.shape[0]
    t = batch * seq
    heads = d // (2 * HEAD_DIM)
    qk_width = heads * 2 * HEAD_DIM
    d_inner = ssm_w_out.shape[1]
    conv_dim = ssm_conv_w.shape[2]
    groups = (conv_dim - d_inner) // (2 * SSM_STATE)
    ssm_heads = ssm_dt_bias.shape[1]
    tm = min(1024, seq)

    c_pad = jnp.pad(c, ((0, 8 - batch), (0, 0)))
    mod = _ada_mod(c_pad, ada_w, ada_b)[:, :batch]
    shift, scale, gate = (mod[:, :, i * d:(i + 1) * d].reshape(depth, batch, 1, d) for i in range(3))

    inv_freq = ROPE_THETA ** (-jnp.arange(0, HEAD_DIM, 2, dtype=f32) / HEAD_DIM)
    freq_row = jnp.tile(inv_freq, LANES // (HEAD_DIM // 2)).reshape(1, LANES)
    cos_t, sin_t = _rope_tables(positions.reshape(t, 1), freq_row)

    x2 = x.reshape(t, d)
    for layer in range(depth):
        j = layer // 2
        nw = norm_w[layer].reshape(1, d)
        if layer % 2 == 0:
            lambda_init = 0.8 - 0.6 * math.exp(-0.3 * layer)
            n_in = attn_w_in.shape[2]
            proj = _norm_proj(x2, nw, scale[layer], shift[layer], attn_w_in, j, n_in,
                              seq, bf16, tm, _tile(n_in, 512))
            rep = qk_width // HEAD_DIM
            qk_w = jnp.stack([jnp.tile(attn_q_norm[j], rep) * (HEAD_DIM ** -0.5 * math.log2(math.e)),
                              jnp.tile(attn_k_norm[j], rep)]).reshape(2, 1, qk_width)
            qk = _qk_prep(proj, qk_w, cos_t, sin_t, qk_width)
            o = _diff_attn(qk, proj, attn_lambda_q1[j], attn_lambda_k1[j], attn_lambda_q2[j],
                           attn_lambda_k2[j], attn_subln_w[j], batch=batch, seq=seq, heads=heads,
                           lambda_init=lambda_init)
            x2 = _out_proj(o, attn_w_out, j, x2, gate[layer], seq, tm, _tile(d, 512))
        else:
            main = d_inner + conv_dim
            w_dt = jnp.pad(ssm_w_in[j][:, main:], ((0, 0), (0, LANES - ssm_heads)))[None]
            proj = _norm_proj(x2, nw, scale[layer], shift[layer], ssm_w_in, j, main,
                              seq, bf16, tm, _tile(main, 512))
            dt_raw = _norm_proj(x2, nw, scale[layer], shift[layer], w_dt, 0, LANES, seq, f32, tm, LANES)
            y = _ssd(proj, dt_raw, ssm_dt_bias[j], ssm_A_log[j], ssm_conv_w[j], ssm_conv_b[j],
                     ssm_D[j], ssm_norm_w[j], batch=batch, seq=seq, d_inner=d_inner, groups=groups)
            x2 = _out_proj(y, ssm_w_out, j, x2, gate[layer], seq, tm, _tile(d, 256))
    return x2.reshape(batch, seq, d)
```

```python
import functools
import math

import numpy as np
import jax
import jax.numpy as jnp
from jax import lax
from jax.experimental import pallas as pl
from jax.experimental.pallas import tpu as pltpu

EPS = 1e-6
ROPE_THETA = 10000.0
LANES = 128
BF16_ROWS = 16
HEAD_DIM = 64
SSM_HEAD_DIM = 64
SSM_STATE = 128
SSM_CHUNK = 128
SSM_CONV = 4
VMEM_LIMIT_BYTES = 56 * 1024 * 1024
NEG_BIG = -1e30

f32 = jnp.float32
bf16 = jnp.bfloat16


def _params(*sem):
    return pltpu.CompilerParams(dimension_semantics=sem, vmem_limit_bytes=VMEM_LIMIT_BYTES)


def _silu(x):
    return x * jax.nn.sigmoid(x)


def _tile(n, target):
    if n <= target:
        return n
    best = None
    for cand in range(LANES, target + 1, LANES):
        if n % cand == 0:
            best = cand
    assert best is not None, (n, target)
    return best


def _ada_kernel(c_ref, w_ref, b_ref, o_ref):
    cond = _silu(c_ref[...])
    o_ref[0] = jnp.dot(cond, w_ref[0], preferred_element_type=f32) + b_ref[0]


def _ada_mod(c_pad, ada_w, ada_b, tn=1024):
    depth, d, n = ada_w.shape
    rows = c_pad.shape[0]
    tn = _tile(n, tn)
    return pl.pallas_call(
        _ada_kernel,
        grid=(depth, n // tn),
        in_specs=[pl.BlockSpec((rows, d), lambda l, j: (0, 0)),
                  pl.BlockSpec((1, d, tn), lambda l, j: (l, 0, j)),
                  pl.BlockSpec((1, 1, tn), lambda l, j: (l, 0, j))],
        out_specs=pl.BlockSpec((1, rows, tn), lambda l, j: (l, 0, j)),
        out_shape=jax.ShapeDtypeStruct((depth, rows, n), f32),
        compiler_params=_params("arbitrary", "arbitrary"),
        name="ada_mod",
    )(c_pad, ada_w, ada_b.reshape(depth, 1, n))


def _normproj_kernel(x_ref, nw_ref, sc_ref, sh_ref, w_ref, o_ref, h_ref):
    @pl.when(pl.program_id(1) == 0)
    def _():
        x = x_ref[...]
        ms = jnp.mean(x * x, axis=-1, keepdims=True)
        y = x * lax.rsqrt(ms + EPS) * nw_ref[...]
        h_ref[...] = (y * (1.0 + sc_ref[0]) + sh_ref[0]).astype(h_ref.dtype)

    o_ref[...] = jnp.dot(h_ref[...], w_ref[0].astype(bf16),
                         preferred_element_type=f32).astype(o_ref.dtype)


def _norm_proj(x2, nw, scale, shift, w, w_layer, n_out, seq, out_dtype, tm, tn):
    t, d = x2.shape
    tiles_per_seq = seq // tm
    return pl.pallas_call(
        _normproj_kernel,
        grid=(t // tm, n_out // tn),
        in_specs=[pl.BlockSpec((tm, d), lambda i, j: (i, 0)),
                  pl.BlockSpec((1, d), lambda i, j: (0, 0)),
                  pl.BlockSpec((1, 1, d), lambda i, j: (i // tiles_per_seq, 0, 0)),
                  pl.BlockSpec((1, 1, d), lambda i, j: (i // tiles_per_seq, 0, 0)),
                  pl.BlockSpec((1, d, tn), lambda i, j: (w_layer, 0, j))],
        out_specs=pl.BlockSpec((tm, tn), lambda i, j: (i, j)),
        out_shape=jax.ShapeDtypeStruct((t, n_out), out_dtype),
        scratch_shapes=[pltpu.VMEM((tm, d), bf16)],
        compiler_params=_params("arbitrary", "arbitrary"),
        name="norm_proj",
    )(x2, nw, scale, shift, w)


def _outproj_kernel(a_ref, w_ref, x_ref, g_ref, o_ref):
    acc = jnp.dot(a_ref[...], w_ref[0].astype(bf16), preferred_element_type=f32)
    o_ref[...] = x_ref[...] + g_ref[0] * acc


def _out_proj(a, w, w_layer, x2, gate, seq, tm, tn):
    t, k = a.shape
    n = w.shape[2]
    tiles_per_seq = seq // tm
    return pl.pallas_call(
        _outproj_kernel,
        grid=(t // tm, n // tn),
        in_specs=[pl.BlockSpec((tm, k), lambda i, j: (i, 0)),
                  pl.BlockSpec((1, k, tn), lambda i, j: (w_layer, 0, j)),
                  pl.BlockSpec((tm, tn), lambda i, j: (i, j)),
                  pl.BlockSpec((1, 1, tn), lambda i, j: (i // tiles_per_seq, 0, j))],
        out_specs=pl.BlockSpec((tm, tn), lambda i, j: (i, j)),
        out_shape=jax.ShapeDtypeStruct((t, n), f32),
        compiler_params=_params("arbitrary", "arbitrary"),
        name="out_proj",
    )(a, w, x2, gate)


def _rope_table_kernel(pos_ref, freq_ref, cos_ref, sin_ref):
    ang = pos_ref[...].astype(f32) * freq_ref[...]
    lane = lax.broadcasted_iota(jnp.int32, ang.shape, 1)
    first_half = (lane & (HEAD_DIM // 2)) == 0
    s = jnp.sin(ang)
    cos_ref[...] = jnp.cos(ang)
    sin_ref[...] = jnp.where(first_half, -s, s)


def _rope_tables(pos_col, freq_row, tm=1024):
    t = pos_col.shape[0]
    tm = min(tm, t)
    return pl.pallas_call(
        _rope_table_kernel,
        grid=(t // tm,),
        in_specs=[pl.BlockSpec((tm, 1), lambda i: (i, 0)),
                  pl.BlockSpec((1, LANES), lambda i: (0, 0))],
        out_specs=[pl.BlockSpec((tm, LANES), lambda i: (i, 0)),
                   pl.BlockSpec((tm, LANES), lambda i: (i, 0))],
        out_shape=[jax.ShapeDtypeStruct((t, LANES), f32)] * 2,
        compiler_params=_params("arbitrary"),
        name="rope_table",
    )(pos_col, freq_row)


def _qkprep_kernel(p_ref, w_ref, cos_ref, sin_ref, o_ref):
    width = p_ref.shape[1]
    r = lax.broadcasted_iota(jnp.int32, (LANES, LANES), 0) // HEAD_DIM
    c = lax.broadcasted_iota(jnp.int32, (LANES, LANES), 1) // HEAD_DIM
    group_mean = jnp.where(r == c, 1.0 / HEAD_DIM, 0.0).astype(bf16)
    cos = cos_ref[...]
    sin = sin_ref[...]
    lane = lax.broadcasted_iota(jnp.int32, cos.shape, 1)
    first_half = (lane & (HEAD_DIM // 2)) == 0
    for j in range(width // LANES):
        sl = slice(j * LANES, (j + 1) * LANES)
        x = p_ref[:, sl].astype(f32)
        ms = jnp.dot((x * x).astype(bf16), group_mean, preferred_element_type=f32)
        y = x * lax.rsqrt(ms + EPS) * w_ref[0, :, sl]
        partner = jnp.where(first_half,
                            pltpu.roll(y, LANES - HEAD_DIM // 2, 1),
                            pltpu.roll(y, HEAD_DIM // 2, 1))
        o_ref[:, sl] = (y * cos + partner * sin).astype(o_ref.dtype)


def _qk_prep(proj, qk_w, cos_t, sin_t, width, tm=512):
    t = proj.shape[0]
    tm = min(tm, t)
    return pl.pallas_call(
        _qkprep_kernel,
        grid=(t // tm, 2),
        in_specs=[pl.BlockSpec((tm, width), lambda i, j: (i, j)),
                  pl.BlockSpec((1, 1, width), lambda i, j: (j, 0, 0)),
                  pl.BlockSpec((tm, LANES), lambda i, j: (i, 0)),
                  pl.BlockSpec((tm, LANES), lambda i, j: (i, 0))],
        out_specs=pl.BlockSpec((tm, width), lambda i, j: (i, j)),
        out_shape=jax.ShapeDtypeStruct((t, 2 * width), bf16),
        compiler_params=_params("arbitrary", "arbitrary"),
        name="qk_prep",
    )(proj, qk_w, cos_t, sin_t)


def _attn_kernel(q_ref, k_ref, v_ref, g_ref, lq1_ref, lk1_ref, lq2_ref, lk2_ref, sw_ref, bias_ref, o_ref,
                 vt_ref, qq_ref, s_ref, acc_ref, *, bq, bk, hp, lambda_init):
    seq = q_ref.shape[0]
    nslot = bq // bk
    lam = (jnp.exp(jnp.sum(lq1_ref[...] * lk1_ref[...], axis=-1, keepdims=True))
           - jnp.exp(jnp.sum(lq2_ref[...] * lk2_ref[...], axis=-1, keepdims=True))
           + lambda_init)
    cols = [slice(hh * LANES, (hh + 1) * LANES) for hh in range(hp)]

    def transpose_v(c, _):
        start = pl.multiple_of(c * bk, bk)
        for hh in range(hp):
            vt_ref[hh, c] = v_ref[pl.ds(start, bk), cols[hh]].astype(f32).T.astype(bf16)
        return 0
    lax.fori_loop(0, seq // bk, transpose_v, 0)

    d_row = lax.broadcasted_iota(jnp.int32, (LANES, bq), 0)

    def scores(hh, j):
        start = pl.multiple_of(j * bk, bk)
        return jnp.dot(k_ref[pl.ds(start, bk), cols[hh]], qq_ref[hh], preferred_element_type=f32)

    def consume(hh, slot, j, m, l, masked):
        st = s_ref[hh, slot]
        if masked:
            st = st + bias_ref[slot]
        m_new = jnp.maximum(m, jnp.max(st, axis=0, keepdims=True))
        alpha = jnp.exp2(m - m_new)
        p = jnp.exp2(st - m_new)
        l_new = alpha * l + jnp.sum(p, axis=0, keepdims=True)
        acc_ref[hh] = alpha * acc_ref[hh] + jnp.dot(vt_ref[hh, j], p.astype(bf16),
                                                    preferred_element_type=f32)
        return m_new, l_new

    def q_block(i, _):
        q_start = pl.multiple_of(i * bq, bq)
        for hh in range(hp):
            qt = q_ref[pl.ds(q_start, bq), cols[hh]].astype(f32).T
            qq_ref[hh] = jnp.concatenate([jnp.where(d_row < HEAD_DIM, qt, 0.0),
                                          jnp.where(d_row >= HEAD_DIM, qt, 0.0)], axis=1).astype(bf16)
            acc_ref[hh] = jnp.zeros(acc_ref.shape[1:], f32)
            for slot in range(nslot):
                s_ref[hh, slot] = scores(hh, slot)

        def body(jj, carry):
            out = []
            for hh in range(hp):
                m, l = carry[hh]
                for slot in range(nslot):
                    m, l = consume(hh, slot, jj * nslot + slot, m, l, False)
                    s_ref[hh, slot] = scores(hh, (jj + 1) * nslot + slot)
                out.append((m, l))
            return tuple(out)

        init = tuple((jnp.full((1, 2 * bq), NEG_BIG, f32), jnp.zeros((1, 2 * bq), f32))
                     for _ in range(hp))
        carry = lax.fori_loop(0, i, body, init)

        for hh in range(hp):
            m, l = carry[hh]
            for slot in range(nslot):
                m, l = consume(hh, slot, i * nslot + slot, m, l, True)
            ot = acc_ref[hh] / l
            odt = ot[:, :bq] - lam * ot[:, bq:]
            ms = jnp.mean(odt * odt, axis=0, keepdims=True)
            y = (odt * lax.rsqrt(ms + EPS)).T * (sw_ref[...] * (1.0 - lambda_init))
            g = g_ref[pl.ds(q_start, bq), cols[hh]].astype(f32)
            o_ref[pl.ds(q_start, bq), cols[hh]] = (y * _silu(g)).astype(o_ref.dtype)
        return 0

    lax.fori_loop(0, seq // bq, q_block, 0)


def _diff_attn(qk, proj, lq1, lk1, lq2, lk2, subln_w, *, batch, seq, heads, lambda_init,
               bq=512, bk=256, hp=4):
    t = qk.shape[0]
    bq = min(bq, seq)
    bk = min(bk, bq)
    hp = math.gcd(hp, heads)
    nslot = bq // bk
    w = hp * LANES
    ng = heads // hp
    key = np.arange(bq)[:, None]
    qry = np.arange(bq)[None, :]
    tri = np.where(key <= qry, 0.0, NEG_BIG).astype(np.float32)
    bias = jnp.asarray(np.concatenate([tri, tri], axis=1).reshape(nslot, bk, 2 * bq))
    vec = lambda a: a.reshape(1, -1).astype(f32)
    small = pl.BlockSpec((1, HEAD_DIM), lambda b, h: (0, 0))
    single = pl.Buffered(1)
    kern = functools.partial(_attn_kernel, bq=bq, bk=bk, hp=hp, lambda_init=lambda_init)
    return pl.pallas_call(
        kern,
        grid=(batch, ng),
        in_specs=[pl.BlockSpec((seq, w), lambda b, h: (b, h), pipeline_mode=single),
                  pl.BlockSpec((seq, w), lambda b, h: (b, ng + h)),
                  pl.BlockSpec((seq, w), lambda b, h: (b, 2 * ng + h)),
                  pl.BlockSpec((seq, w), lambda b, h: (b, 3 * ng + h), pipeline_mode=single),
                  small, small, small, small,
                  pl.BlockSpec((1, LANES), lambda b, h: (0, 0)),
                  pl.BlockSpec((nslot, bk, 2 * bq), lambda b, h: (0, 0, 0), pipeline_mode=single)],
        out_specs=pl.BlockSpec((seq, w), lambda b, h: (b, h)),
        out_shape=jax.ShapeDtypeStruct((t, heads * LANES), bf16),
        scratch_shapes=[pltpu.VMEM((hp, seq // bk, LANES, bk), bf16),
                        pltpu.VMEM((hp, LANES, 2 * bq), bf16),
                        pltpu.VMEM((hp, nslot, bk, 2 * bq), f32),
                        pltpu.VMEM((hp, LANES, 2 * bq), f32)],
        compiler_params=_params("arbitrary", "arbitrary"),
        name="diff_attn",
    )(qk, qk, proj, proj, vec(lq1), vec(lk1), vec(lq2), vec(lk2), vec(subln_w), bias)


def _ssd_constants(rheads):
    L, P = SSM_CHUNK, SSM_HEAD_DIM
    k = np.arange(LANES)[:, None]
    t_idx = np.arange(L)[:, None]
    j_idx = np.arange(2 * L)[None, :]
    shifts = np.stack([(j_idx == L + t_idx - (SSM_CONV - 1 - tap)) for tap in range(SSM_CONV - 1)])
    col = np.arange(rheads * LANES)[None, :]
    seg = (k < 3 * rheads) & (k % rheads == col // LANES)
    col2 = np.arange(2 * rheads * P)[None, :]
    half = rheads * P
    exp = np.where(col2 < half,
                   (k < 2 * rheads) & (k % rheads == col2 // P),
                   (k >= 2 * rheads) & (k < 4 * rheads) & (k % rheads == (col2 - half) // P))
    upper = (np.arange(L)[:, None] <= np.arange(L)[None, :])
    return (jnp.asarray(shifts, dtype=bf16), jnp.asarray(seg, dtype=bf16), jnp.asarray(exp, dtype=bf16),
            jnp.asarray(upper, dtype=f32))


def _split_bf16(a, pieces):
    out, rem = [], a
    for _ in range(pieces):
        p = rem.astype(bf16).astype(f32)
        out.append(p)
        rem = rem - p
    return out


def _ssd_kernel(dt_ref, dtb_ref, alog_ref, z_ref, xs_ref, b_ref, c_ref,
                wx_ref, wb_ref, wc_ref, bx_ref, bb_ref, bc_ref, dskip_ref, nw_ref,
                shift_ref, segoh_ref, expoh_ref, upper_ref,
                o_ref,
                state_ref, halo_ref, stack_ref, cs_ref, dtc_ref, cst_ref, dtt_ref,
                *, groups, rheads, gp):
    c = pl.program_id(1)
    pg = pl.program_id(2)
    L = SSM_CHUNK
    R = rheads
    xw = rheads * SSM_HEAD_DIM
    n = SSM_STATE
    halo = BF16_ROWS

    @pl.when(pg == 0)
    def _():
        raw = dt_ref[...] + dtb_ref[...]
        dt = jnp.maximum(raw, 0.0) + jnp.log(1.0 + jnp.exp(-jnp.abs(raw)))
        da_t = (dt * (-jnp.exp(alog_ref[...]))).T
        cs_t = jnp.dot(da_t, upper_ref[...], precision=lax.Precision.HIGHEST,
                       preferred_element_type=f32)
        cst_ref[...] = cs_t
        dtt_ref[...] = dt.T
        cs = cs_t.T
        for gg in range(groups):
            shift = (LANES - gg * rheads) % LANES
            cs_ref[gg] = pltpu.roll(cs, shift, 1) if shift else cs
            dtc_ref[gg] = pltpu.roll(dt, shift, 1) if shift else dt

    @pl.when((c == 0) & (pg == 0))
    def _():
        stack_ref[:, 0:L, :] = jnp.zeros((gp, L, stack_ref.shape[2]), bf16)

    @pl.when(c == 0)
    def _():
        state_ref[:, pg] = jnp.zeros((gp,) + state_ref.shape[2:], f32)
        halo_ref[:, pg] = jnp.zeros((gp,) + halo_ref.shape[2:], bf16)

    for gi in range(gp):
        _ssd_group(gi, pg, z_ref, xs_ref, b_ref, c_ref, wx_ref, wb_ref, wc_ref,
                   bx_ref, bb_ref, bc_ref, dskip_ref, nw_ref, shift_ref, segoh_ref, expoh_ref,
                   o_ref, state_ref, halo_ref, stack_ref, cs_ref, dtc_ref, cst_ref, dtt_ref, rheads, gp)


def _ssd_group(gi, pg, z_ref, xs_ref, b_ref, c_ref, wx_ref, wb_ref, wc_ref, bx_ref, bb_ref, bc_ref,
               dskip_ref, nw_ref, shift_ref, segoh_ref, expoh_ref, o_ref,
               state_ref, halo_ref, stack_ref, cs_ref, dtc_ref, cst_ref, dtt_ref, rheads, gp):
    L = SSM_CHUNK
    R = rheads
    xw = rheads * SSM_HEAD_DIM
    n = SSM_STATE
    halo = BF16_ROWS
    g = pg * gp + gi
    xsl = slice(gi * xw, (gi + 1) * xw)
    nsl = slice(gi * n, (gi + 1) * n)

    xin = jnp.concatenate([xs_ref[:, xsl], b_ref[:, nsl], c_ref[:, nsl]], axis=1)
    stack_ref[gi, L - halo:L, :] = halo_ref[gi, pg]
    stack_ref[gi, L:2 * L, :] = xin
    halo_ref[gi, pg] = xin[L - halo:, :]
    stacked = stack_ref[gi]
    w = jnp.concatenate([wx_ref[:, xsl], wb_ref[:, nsl], wc_ref[:, nsl]], axis=1)
    bias = jnp.concatenate([bx_ref[:, xsl], bb_ref[:, nsl], bc_ref[:, nsl]], axis=1)
    acc = bias + w[SSM_CONV - 1:SSM_CONV, :] * xin.astype(f32)
    for tap in range(SSM_CONV - 1):
        acc = acc + w[tap:tap + 1, :] * jnp.dot(shift_ref[tap], stacked, preferred_element_type=f32)
    xbc = _silu(acc)
    xs = xbc[:, :xw]
    bm = xbc[:, xw:xw + SSM_STATE]
    cm = xbc[:, xw + SSM_STATE:]
    cm16 = cm.astype(bf16)

    row0 = pl.multiple_of(g * rheads, rheads)
    cs_t = cst_ref[pl.ds(row0, rheads), :]
    dt_t = dtt_ref[pl.ds(row0, rheads), :]
    cs_c = cs_ref[g]
    dt_c = dtc_ref[g]
    lane = lax.broadcasted_iota(jnp.int32, (L, LANES), 1)

    hi, mid, lo = _split_bf16(cs_c, 3)
    packed = jnp.where(lane < R, hi, jnp.where(lane < 2 * R, pltpu.roll(mid, R, 1), pltpu.roll(lo, 2 * R, 1)))
    seg_col = jnp.dot(packed.astype(bf16), segoh_ref[...], preferred_element_type=f32)

    cb = lax.dot_general(cm16, bm.astype(bf16), (((1,), (1,)), ((), ())),
                         preferred_element_type=f32)
    l_idx = lax.broadcasted_iota(jnp.int32, (L, L), 0)
    s_idx = lax.broadcasted_iota(jnp.int32, (L, L), 1)
    causal = l_idx >= s_idx
    lo_half = lane < SSM_HEAD_DIM

    y_parts = []
    for j in range(rheads // 2):
        xj = xs[:, j * LANES:(j + 1) * LANES]
        x_lo = jnp.where(lo_half, xj, 0.0).astype(bf16)
        x_hi = jnp.where(lo_half, 0.0, xj).astype(bf16)
        yj = None
        for r, xr in ((2 * j, x_lo), (2 * j + 1, x_hi)):
            seg = seg_col[:, r * LANES:(r + 1) * LANES] - cs_t[r:r + 1, :]
            decay = jnp.exp(jnp.where(causal, seg, -jnp.inf))
            mr = (cb * decay * dt_t[r:r + 1, :]).astype(bf16)
            part = jnp.dot(mr, xr, preferred_element_type=f32)
            yj = part if yj is None else yj + part
        y_parts.append(yj)
    y_diag = jnp.concatenate(y_parts, axis=1)

    cs_last = cs_c[L - 1:L, :]
    e_hi, e_lo = _split_bf16(jnp.exp(cs_c), 2)
    w_hi, w_lo = _split_bf16(jnp.exp(cs_last - cs_c) * dt_c, 2)
    packed2 = jnp.where(lane < R, e_hi,
                        jnp.where(lane < 2 * R, pltpu.roll(e_lo, R, 1),
                                  jnp.where(lane < 3 * R, pltpu.roll(w_hi, 2 * R, 1),
                                            pltpu.roll(w_lo, 3 * R, 1))))
    spread = jnp.dot(packed2.astype(bf16), expoh_ref[...], preferred_element_type=f32)
    e_cs = spread[:, :xw]
    wgt = spread[:, xw:]

    prev = state_ref[gi, pg]
    y_off = jnp.dot(cm16, prev.astype(bf16), preferred_element_type=f32) * e_cs
    new = jnp.dot(bm.T.astype(bf16), (xs * wgt).astype(bf16), preferred_element_type=f32)
    state_ref[gi, pg] = prev * e_cs[L - 1:L, :] + new

    y = y_diag + y_off + xs * dskip_ref[:, xsl]
    y = y * _silu(z_ref[:, xsl].astype(f32))
    ms = jnp.mean(y * y, axis=-1, keepdims=True)
    o_ref[:, xsl] = (y * lax.rsqrt(ms + EPS) * nw_ref[:, xsl]).astype(o_ref.dtype)


def _ssd(proj, dt_raw, dt_bias, a_log, conv_w, conv_b, d_skip, norm_w, *, batch, seq, d_inner, groups, gp=2):
    t = proj.shape[0]
    L = SSM_CHUNK
    nc = seq // L
    heads = d_inner // SSM_HEAD_DIM
    rheads = heads // groups
    assert rheads % 2 == 0 and 4 * rheads <= LANES
    gp = math.gcd(gp, groups)
    xw = rheads * SSM_HEAD_DIM
    n = SSM_STATE
    cw = xw + 2 * n
    bw, bn = gp * xw, gp * n
    assert d_inner % bw == 0 and (2 * d_inner) % bn == 0
    xs_blk0 = d_inner // bw
    b_blk0 = 2 * d_inner // bn
    c_blk0 = b_blk0 + groups // gp
    cb_blk0 = d_inner // bn
    cc_blk0 = cb_blk0 + groups // gp
    row = lambda b, c, g: b * nc + c
    pad = lambda a: jnp.pad(a.reshape(1, -1).astype(f32), ((0, 0), (0, LANES - heads)))
    const2 = lambda shape: pl.BlockSpec(shape, lambda b, c, g: (0, 0))
    shifts, seg_oh, exp_oh, upper = _ssd_constants(rheads)
    kern = functools.partial(_ssd_kernel, groups=groups, rheads=rheads, gp=gp)
    return pl.pallas_call(
        kern,
        grid=(batch, nc, groups // gp),
        in_specs=[pl.BlockSpec((L, LANES), lambda b, c, g: (row(b, c, g), 0)),
                  const2((1, LANES)),
                  const2((1, LANES)),
                  pl.BlockSpec((L, bw), lambda b, c, g: (row(b, c, g), g)),
                  pl.BlockSpec((L, bw), lambda b, c, g: (row(b, c, g), xs_blk0 + g)),
                  pl.BlockSpec((L, bn), lambda b, c, g: (row(b, c, g), b_blk0 + g)),
                  pl.BlockSpec((L, bn), lambda b, c, g: (row(b, c, g), c_blk0 + g)),
                  pl.BlockSpec((SSM_CONV, bw), lambda b, c, g: (0, g)),
                  pl.BlockSpec((SSM_CONV, bn), lambda b, c, g: (0, cb_blk0 + g)),
                  pl.BlockSpec((SSM_CONV, bn), lambda b, c, g: (0, cc_blk0 + g)),
                  pl.BlockSpec((1, bw), lambda b, c, g: (0, g)),
                  pl.BlockSpec((1, bn), lambda b, c, g: (0, cb_blk0 + g)),
                  pl.BlockSpec((1, bn), lambda b, c, g: (0, cc_blk0 + g)),
                  pl.BlockSpec((1, bw), lambda b, c, g: (0, g)),
                  pl.BlockSpec((1, bw), lambda b, c, g: (0, g)),
                  pl.BlockSpec((SSM_CONV - 1, L, 2 * L), lambda b, c, g: (0, 0, 0)),
                  const2((LANES, rheads * LANES)),
                  const2((LANES, 2 * xw)),
                  const2((L, L))],
        out_specs=pl.BlockSpec((L, bw), lambda b, c, g: (row(b, c, g), g)),
        out_shape=jax.ShapeDtypeStruct((t, d_inner), bf16),
        scratch_shapes=[pltpu.VMEM((gp, groups // gp, n, xw), f32),
                        pltpu.VMEM((gp, groups // gp, BF16_ROWS, cw), bf16),
                        pltpu.VMEM((gp, 2 * L, cw), bf16),
                        pltpu.VMEM((groups, L, LANES), f32),
                        pltpu.VMEM((groups, L, LANES), f32),
                        pltpu.VMEM((LANES, L), f32),
                        pltpu.VMEM((LANES, L), f32)],
        compiler_params=_params("arbitrary", "arbitrary", "arbitrary"),
        name="ssd_chunk",
    )(dt_raw, pad(dt_bias), pad(a_log), proj, proj, proj, proj,
      conv_w, conv_w, conv_w, conv_b.reshape(1, -1), conv_b.reshape(1, -1), conv_b.reshape(1, -1),
      jnp.repeat(d_skip, SSM_HEAD_DIM).reshape(1, -1), norm_w.reshape(1, -1),
      shifts, seg_oh, exp_oh, upper)


def kernel(x, c, positions, norm_w, ada_w, ada_b, attn_w_in, attn_q_norm, attn_k_norm, attn_lambda_q1, attn_lambda_k1, attn_lambda_q2, attn_lambda_k2, attn_subln_w, attn_w_out, ssm_w_in, ssm_conv_w, ssm_conv_b, ssm_dt_bias, ssm_A_log, ssm_D, ssm_norm_w, ssm_w_out):
    batch, seq, d = x.shape
    depth = norm_w.shape[0]
    t = batch * seq
    heads = d // (2 * HEAD_DIM)
    qk_width = heads * 2 * HEAD_DIM
    d_inner = ssm_w_out.shape[1]
    conv_dim = ssm_conv_w.shape[2]
    groups = (conv_dim - d_inner) // (2 * SSM_STATE)
    ssm_heads = ssm_dt_bias.shape[1]
    tm = min(1024, seq)

    c_pad = jnp.pad(c, ((0, 8 - batch), (0, 0)))
    mod = _ada_mod(c_pad, ada_w, ada_b)[:, :batch]
    shift, scale, gate = (mod[:, :, i * d:(i + 1) * d].reshape(depth, batch, 1, d) for i in range(3))

    inv_freq = ROPE_THETA ** (-jnp.arange(0, HEAD_DIM, 2, dtype=f32) / HEAD_DIM)
    freq_row = jnp.tile(inv_freq, LANES // (HEAD_DIM // 2)).reshape(1, LANES)
    cos_t, sin_t = _rope_tables(positions.reshape(t, 1), freq_row)

    x2 = x.reshape(t, d)
    for layer in range(depth):
        j = layer // 2
        nw = norm_w[layer].reshape(1, d)
        if layer % 2 == 0:
            lambda_init = 0.8 - 0.6 * math.exp(-0.3 * layer)
            n_in = attn_w_in.shape[2]
            proj = _norm_proj(x2, nw, scale[layer], shift[layer], attn_w_in[j].astype(bf16)[None], 0, n_in,
                              seq, bf16, tm, _tile(n_in, 1024))
            rep = qk_width // HEAD_DIM
            qk_w = jnp.stack([jnp.tile(attn_q_norm[j], rep) * (HEAD_DIM ** -0.5 * math.log2(math.e)),
                              jnp.tile(attn_k_norm[j], rep)]).reshape(2, 1, qk_width)
            qk = _qk_prep(proj, qk_w, cos_t, sin_t, qk_width)
            o = _diff_attn(qk, proj, attn_lambda_q1[j], attn_lambda_k1[j], attn_lambda_q2[j],
                           attn_lambda_k2[j], attn_subln_w[j], batch=batch, seq=seq, heads=heads,
                           lambda_init=lambda_init)
            x2 = _out_proj(o, attn_w_out[j].astype(bf16)[None], 0, x2, gate[layer], seq, tm, _tile(d, 1024))
        else:
            main = d_inner + conv_dim
            w_dt = jnp.pad(ssm_w_in[j][:, main:], ((0, 0), (0, LANES - ssm_heads)))[None]
            proj = _norm_proj(x2, nw, scale[layer], shift[layer], ssm_w_in[j][:, :main].astype(bf16)[None], 0,
                              main, seq, bf16, tm, _tile(main, 1024))
            dt_raw = _norm_proj(x2, nw, scale[layer], shift[layer], w_dt, 0, LANES, seq, f32, tm, LANES)
            y = _ssd(proj, dt_raw, ssm_dt_bias[j], ssm_A_log[j], ssm_conv_w[j], ssm_conv_b[j],
                     ssm_D[j], ssm_norm_w[j], batch=batch, seq=seq, d_inner=d_inner, groups=groups)
            x2 = _out_proj(y, ssm_w_out[j].astype(bf16)[None], 0, x2, gate[layer], seq, tm, _tile(d, 512))
    return x2.reshape(batch, seq, d)
```

```python
import functools
import math

import numpy as np
import jax
import jax.numpy as jnp
from jax import lax
from jax.experimental import pallas as pl
from jax.experimental.pallas import tpu as pltpu

EPS = 1e-6
ROPE_THETA = 10000.0
LANES = 128
BF16_ROWS = 16
CONV_HALO = 8
CONV_ROW_CHUNKS = 4
HEAD_DIM = 64
SSM_HEAD_DIM = 64
SSM_STATE = 128
SSM_CHUNK = 128
SSM_CONV = 4
VMEM_LIMIT_BYTES = 56 * 1024 * 1024
NEG_BIG = -1e30

f32 = jnp.float32
bf16 = jnp.bfloat16


def _params(*sem):
    return pltpu.CompilerParams(dimension_semantics=sem, vmem_limit_bytes=VMEM_LIMIT_BYTES)


def _silu(x):
    return x * jax.nn.sigmoid(x)


def _tile(n, target):
    if n <= target:
        return n
    best = None
    for cand in range(LANES, target + 1, LANES):
        if n % cand == 0:
            best = cand
    assert best is not None, (n, target)
    return best


def _ada_kernel(c_ref, w_ref, b_ref, o_ref):
    cond = _silu(c_ref[...])
    o_ref[0] = jnp.dot(cond, w_ref[0], preferred_element_type=f32) + b_ref[0]


def _ada_mod(c_pad, ada_w, ada_b, tn=1024):
    depth, d, n = ada_w.shape
    rows = c_pad.shape[0]
    tn = _tile(n, tn)
    return pl.pallas_call(
        _ada_kernel,
        grid=(depth, n // tn),
        in_specs=[pl.BlockSpec((rows, d), lambda l, j: (0, 0)),
                  pl.BlockSpec((1, d, tn), lambda l, j: (l, 0, j)),
                  pl.BlockSpec((1, 1, tn), lambda l, j: (l, 0, j))],
        out_specs=pl.BlockSpec((1, rows, tn), lambda l, j: (l, 0, j)),
        out_shape=jax.ShapeDtypeStruct((depth, rows, n), f32),
        compiler_params=_params("arbitrary", "arbitrary"),
        name="ada_mod",
    )(c_pad, ada_w, ada_b.reshape(depth, 1, n))


def _normproj_kernel(x_ref, nw_ref, sc_ref, sh_ref, w_ref, o_ref, h_ref):
    @pl.when(pl.program_id(1) == 0)
    def _():
        x = x_ref[...]
        ms = jnp.mean(x * x, axis=-1, keepdims=True)
        y = x * lax.rsqrt(ms + EPS) * nw_ref[...]
        h_ref[...] = (y * (1.0 + sc_ref[0]) + sh_ref[0]).astype(h_ref.dtype)

    o_ref[...] = jnp.dot(h_ref[...], w_ref[0].astype(bf16),
                         preferred_element_type=f32).astype(o_ref.dtype)


def _norm_proj(x2, nw, scale, shift, w, w_layer, n_out, seq, out_dtype, tm, tn):
    t, d = x2.shape
    tiles_per_seq = seq // tm
    return pl.pallas_call(
        _normproj_kernel,
        grid=(t // tm, n_out // tn),
        in_specs=[pl.BlockSpec((tm, d), lambda i, j: (i, 0)),
                  pl.BlockSpec((1, d), lambda i, j: (0, 0)),
                  pl.BlockSpec((1, 1, d), lambda i, j: (i // tiles_per_seq, 0, 0)),
                  pl.BlockSpec((1, 1, d), lambda i, j: (i // tiles_per_seq, 0, 0)),
                  pl.BlockSpec((1, d, tn), lambda i, j: (w_layer, 0, j))],
        out_specs=pl.BlockSpec((tm, tn), lambda i, j: (i, j)),
        out_shape=jax.ShapeDtypeStruct((t, n_out), out_dtype),
        scratch_shapes=[pltpu.VMEM((tm, d), bf16)],
        compiler_params=_params("arbitrary", "arbitrary"),
        name="norm_proj",
    )(x2, nw, scale, shift, w)


def _normproj_conv_kernel(x_ref, nw_ref, sc_ref, sh_ref, w_ref, cw_ref, cb_ref, o_ref, h_ref, halo_ref,
                          *, tiles_per_seq):
    i = pl.program_id(0)
    j = pl.program_id(1)

    @pl.when(j == 0)
    def _():
        x = x_ref[...]
        ms = jnp.mean(x * x, axis=-1, keepdims=True)
        y = x * lax.rsqrt(ms + EPS) * nw_ref[...]
        h_ref[...] = (y * (1.0 + sc_ref[0]) + sh_ref[0]).astype(h_ref.dtype)

    tm, tn = o_ref.shape
    rc = tm // CONV_ROW_CHUNKS
    w = w_ref[0].astype(bf16)
    first = (i % tiles_per_seq) == 0
    prev = jnp.where(first, 0.0, halo_ref[j])
    sub = lax.broadcasted_iota(jnp.int32, (rc // CONV_HALO, CONV_HALO, tn), 1)
    for r in range(CONV_ROW_CHUNKS):
        acc = jnp.dot(h_ref[r * rc:(r + 1) * rc, :], w, preferred_element_type=f32)
        slabs = jnp.concatenate([prev, acc], axis=0).reshape(rc // CONV_HALO + 1, CONV_HALO, tn)
        y = cb_ref[...] + cw_ref[SSM_CONV - 1:SSM_CONV, :] * acc
        for tap in range(SSM_CONV - 1):
            back = SSM_CONV - 1 - tap
            rot = pltpu.roll(slabs, back, 1)
            shifted = jnp.where(sub < back, rot[:-1], rot[1:]).reshape(rc, tn)
            y = y + cw_ref[tap:tap + 1, :] * shifted
        o_ref[r * rc:(r + 1) * rc, :] = (y / (1.0 + jnp.exp(-y))).astype(o_ref.dtype)
        prev = acc[rc - CONV_HALO:, :]
    halo_ref[j] = prev


def _norm_proj_conv(x2, nw, scale, shift, w, conv_w, conv_b, seq, tm, tn):
    t, d = x2.shape
    n = w.shape[2]
    tiles_per_seq = seq // tm
    kern = functools.partial(_normproj_conv_kernel, tiles_per_seq=tiles_per_seq)
    return pl.pallas_call(
        kern,
        grid=(t // tm, n // tn),
        in_specs=[pl.BlockSpec((tm, d), lambda i, j: (i, 0)),
                  pl.BlockSpec((1, d), lambda i, j: (0, 0)),
                  pl.BlockSpec((1, 1, d), lambda i, j: (i // tiles_per_seq, 0, 0)),
                  pl.BlockSpec((1, 1, d), lambda i, j: (i // tiles_per_seq, 0, 0)),
                  pl.BlockSpec((1, d, tn), lambda i, j: (0, 0, j)),
                  pl.BlockSpec((SSM_CONV, tn), lambda i, j: (0, j)),
                  pl.BlockSpec((1, tn), lambda i, j: (0, j))],
        out_specs=pl.BlockSpec((tm, tn), lambda i, j: (i, j)),
        out_shape=jax.ShapeDtypeStruct((t, n), bf16),
        scratch_shapes=[pltpu.VMEM((tm, d), bf16),
                        pltpu.VMEM((n // tn, CONV_HALO, tn), f32)],
        compiler_params=_params("arbitrary", "arbitrary"),
        name="norm_proj_conv",
    )(x2, nw, scale, shift, w, conv_w, conv_b)


def _outproj_kernel(a_ref, w_ref, x_ref, g_ref, o_ref):
    acc = jnp.dot(a_ref[...], w_ref[0].astype(bf16), preferred_element_type=f32)
    o_ref[...] = x_ref[...] + g_ref[0] * acc


def _out_proj(a, w, w_layer, x2, gate, seq, tm, tn):
    t, k = a.shape
    n = w.shape[2]
    tiles_per_seq = seq // tm
    return pl.pallas_call(
        _outproj_kernel,
        grid=(t // tm, n // tn),
        in_specs=[pl.BlockSpec((tm, k), lambda i, j: (i, 0)),
                  pl.BlockSpec((1, k, tn), lambda i, j: (w_layer, 0, j)),
                  pl.BlockSpec((tm, tn), lambda i, j: (i, j)),
                  pl.BlockSpec((1, 1, tn), lambda i, j: (i // tiles_per_seq, 0, j))],
        out_specs=pl.BlockSpec((tm, tn), lambda i, j: (i, j)),
        out_shape=jax.ShapeDtypeStruct((t, n), f32),
        compiler_params=_params("arbitrary", "arbitrary"),
        name="out_proj",
    )(a, w, x2, gate)


def _rope_table_kernel(pos_ref, freq_ref, cos_ref, sin_ref):
    ang = pos_ref[...].astype(f32) * freq_ref[...]
    lane = lax.broadcasted_iota(jnp.int32, ang.shape, 1)
    first_half = (lane & (HEAD_DIM // 2)) == 0
    s = jnp.sin(ang)
    cos_ref[...] = jnp.cos(ang)
    sin_ref[...] = jnp.where(first_half, -s, s)


def _rope_tables(pos_col, freq_row, tm=1024):
    t = pos_col.shape[0]
    tm = min(tm, t)
    return pl.pallas_call(
        _rope_table_kernel,
        grid=(t // tm,),
        in_specs=[pl.BlockSpec((tm, 1), lambda i: (i, 0)),
                  pl.BlockSpec((1, LANES), lambda i: (0, 0))],
        out_specs=[pl.BlockSpec((tm, LANES), lambda i: (i, 0)),
                   pl.BlockSpec((tm, LANES), lambda i: (i, 0))],
        out_shape=[jax.ShapeDtypeStruct((t, LANES), f32)] * 2,
        compiler_params=_params("arbitrary"),
        name="rope_table",
    )(pos_col, freq_row)


def _qkprep_kernel(p_ref, w_ref, cos_ref, sin_ref, o_ref):
    width = p_ref.shape[1]
    r = lax.broadcasted_iota(jnp.int32, (LANES, LANES), 0) // HEAD_DIM
    c = lax.broadcasted_iota(jnp.int32, (LANES, LANES), 1) // HEAD_DIM
    group_mean = jnp.where(r == c, 1.0 / HEAD_DIM, 0.0).astype(bf16)
    cos = cos_ref[...]
    sin = sin_ref[...]
    lane = lax.broadcasted_iota(jnp.int32, cos.shape, 1)
    first_half = (lane & (HEAD_DIM // 2)) == 0
    for j in range(width // LANES):
        sl = slice(j * LANES, (j + 1) * LANES)
        x = p_ref[:, sl].astype(f32)
        ms = jnp.dot((x * x).astype(bf16), group_mean, preferred_element_type=f32)
        y = x * lax.rsqrt(ms + EPS) * w_ref[0, :, sl]
        partner = jnp.where(first_half,
                            pltpu.roll(y, LANES - HEAD_DIM // 2, 1),
                            pltpu.roll(y, HEAD_DIM // 2, 1))
        o_ref[:, sl] = (y * cos + partner * sin).astype(o_ref.dtype)


def _qk_prep(proj, qk_w, cos_t, sin_t, width, tm=512):
    t = proj.shape[0]
    tm = min(tm, t)
    return pl.pallas_call(
        _qkprep_kernel,
        grid=(t // tm, 2),
        in_specs=[pl.BlockSpec((tm, width), lambda i, j: (i, j)),
                  pl.BlockSpec((1, 1, width), lambda i, j: (j, 0, 0)),
                  pl.BlockSpec((tm, LANES), lambda i, j: (i, 0)),
                  pl.BlockSpec((tm, LANES), lambda i, j: (i, 0))],
        out_specs=pl.BlockSpec((tm, width), lambda i, j: (i, j)),
        out_shape=jax.ShapeDtypeStruct((t, 2 * width), bf16),
        compiler_params=_params("arbitrary", "arbitrary"),
        name="qk_prep",
    )(proj, qk_w, cos_t, sin_t)


def _attn_kernel(q_ref, k_ref, v_ref, g_ref, lq1_ref, lk1_ref, lq2_ref, lk2_ref, sw_ref, bias_ref, o_ref,
                 vt_ref, qq_ref, s_ref, acc_ref, *, bq, bk, hp, lambda_init):
    seq = q_ref.shape[0]
    nslot = bq // bk
    lam = (jnp.exp(jnp.sum(lq1_ref[...] * lk1_ref[...], axis=-1, keepdims=True))
           - jnp.exp(jnp.sum(lq2_ref[...] * lk2_ref[...], axis=-1, keepdims=True))
           + lambda_init)
    cols = [slice(hh * LANES, (hh + 1) * LANES) for hh in range(hp)]

    ones_rows = jnp.where(lax.broadcasted_iota(jnp.int32, (BF16_ROWS, bk), 0) == 0, 1.0, 0.0).astype(bf16)

    def transpose_v(c, _):
        start = pl.multiple_of(c * bk, bk)
        for hh in range(hp):
            vt_ref[hh, c, 0:LANES, :] = v_ref[pl.ds(start, bk), cols[hh]].astype(f32).T.astype(bf16)
            vt_ref[hh, c, LANES:, :] = ones_rows
        return 0
    lax.fori_loop(0, seq // bk, transpose_v, 0)

    d_row = lax.broadcasted_iota(jnp.int32, (LANES, bq), 0)

    def scores(hh, j):
        start = pl.multiple_of(j * bk, bk)
        return jnp.dot(k_ref[pl.ds(start, bk), cols[hh]], qq_ref[hh], preferred_element_type=f32)

    def consume(hh, slot, j, m, masked):
        st = s_ref[hh, slot]
        if masked:
            st = st + bias_ref[slot]
        m_new = jnp.maximum(m, jnp.max(st, axis=0, keepdims=True))
        alpha = jnp.exp2(m - m_new)
        p = jnp.exp2(st - m_new)
        acc_ref[hh] = alpha * acc_ref[hh] + jnp.dot(vt_ref[hh, j], p.astype(bf16),
                                                    preferred_element_type=f32)
        return m_new

    def q_block(i, _):
        q_start = pl.multiple_of(i * bq, bq)
        for hh in range(hp):
            qt = q_ref[pl.ds(q_start, bq), cols[hh]].astype(f32).T
            qq_ref[hh] = jnp.concatenate([jnp.where(d_row < HEAD_DIM, qt, 0.0),
                                          jnp.where(d_row >= HEAD_DIM, qt, 0.0)], axis=1).astype(bf16)
            acc_ref[hh] = jnp.zeros(acc_ref.shape[1:], f32)
            for slot in range(nslot):
                s_ref[hh, slot] = scores(hh, slot)

        def body(jj, carry):
            out = []
            for hh in range(hp):
                m = carry[hh]
                for slot in range(nslot):
                    m = consume(hh, slot, jj * nslot + slot, m, False)
                    s_ref[hh, slot] = scores(hh, (jj + 1) * nslot + slot)
                out.append(m)
            return tuple(out)

        init = tuple(jnp.full((1, 2 * bq), NEG_BIG, f32) for _ in range(hp))
        carry = lax.fori_loop(0, i, body, init)

        for hh in range(hp):
            m = carry[hh]
            for slot in range(nslot):
                m = consume(hh, slot, i * nslot + slot, m, True)
            acc = acc_ref[hh]
            ot = acc[:LANES] / acc[LANES:LANES + 1]
            odt = ot[:, :bq] - lam * ot[:, bq:]
            ms = jnp.mean(odt * odt, axis=0, keepdims=True)
            y = (odt * lax.rsqrt(ms + EPS)).T * (sw_ref[...] * (1.0 - lambda_init))
            g = g_ref[pl.ds(q_start, bq), cols[hh]].astype(f32)
            o_ref[pl.ds(q_start, bq), cols[hh]] = (y * _silu(g)).astype(o_ref.dtype)
        return 0

    lax.fori_loop(0, seq // bq, q_block, 0)


def _diff_attn(qk, proj, lq1, lk1, lq2, lk2, subln_w, *, batch, seq, heads, lambda_init,
               bq=512, bk=256, hp=4):
    t = qk.shape[0]
    bq = min(bq, seq)
    bk = min(bk, bq)
    hp = math.gcd(hp, heads)
    nslot = bq // bk
    w = hp * LANES
    ng = heads // hp
    key = np.arange(bq)[:, None]
    qry = np.arange(bq)[None, :]
    tri = np.where(key <= qry, 0.0, NEG_BIG).astype(np.float32)
    bias = jnp.asarray(np.concatenate([tri, tri], axis=1).reshape(nslot, bk, 2 * bq))
    vec = lambda a: a.reshape(1, -1).astype(f32)
    small = pl.BlockSpec((1, HEAD_DIM), lambda b, h: (0, 0))
    single = pl.Buffered(1)
    kern = functools.partial(_attn_kernel, bq=bq, bk=bk, hp=hp, lambda_init=lambda_init)
    return pl.pallas_call(
        kern,
        grid=(batch, ng),
        in_specs=[pl.BlockSpec((seq, w), lambda b, h: (b, h), pipeline_mode=single),
                  pl.BlockSpec((seq, w), lambda b, h: (b, ng + h)),
                  pl.BlockSpec((seq, w), lambda b, h: (b, 2 * ng + h)),
                  pl.BlockSpec((seq, w), lambda b, h: (b, 3 * ng + h), pipeline_mode=single),
                  small, small, small, small,
                  pl.BlockSpec((1, LANES), lambda b, h: (0, 0)),
                  pl.BlockSpec((nslot, bk, 2 * bq), lambda b, h: (0, 0, 0), pipeline_mode=single)],
        out_specs=pl.BlockSpec((seq, w), lambda b, h: (b, h)),
        out_shape=jax.ShapeDtypeStruct((t, heads * LANES), bf16),
        scratch_shapes=[pltpu.VMEM((hp, seq // bk, LANES + BF16_ROWS, bk), bf16),
                        pltpu.VMEM((hp, LANES, 2 * bq), bf16),
                        pltpu.VMEM((hp, nslot, bk, 2 * bq), f32),
                        pltpu.VMEM((hp, LANES + BF16_ROWS, 2 * bq), f32)],
        compiler_params=_params("arbitrary", "arbitrary"),
        name="diff_attn",
    )(qk, qk, proj, proj, vec(lq1), vec(lk1), vec(lq2), vec(lk2), vec(subln_w), bias)


def _ssd_constants(rheads):
    L, P = SSM_CHUNK, SSM_HEAD_DIM
    k = np.arange(LANES)[:, None]
    col = np.arange(rheads * LANES)[None, :]
    seg = (k < 3 * rheads) & (k % rheads == col // LANES)
    col2 = np.arange(2 * rheads * P)[None, :]
    half = rheads * P
    exp = np.where(col2 < half,
                   (k < 2 * rheads) & (k % rheads == col2 // P),
                   (k >= 2 * rheads) & (k < 4 * rheads) & (k % rheads == (col2 - half) // P))
    upper = (np.arange(L)[:, None] <= np.arange(L)[None, :])
    return jnp.asarray(seg, dtype=bf16), jnp.asarray(exp, dtype=bf16), jnp.asarray(upper, dtype=f32)


def _split_bf16(a, pieces):
    out, rem = [], a
    for _ in range(pieces):
        p = rem.astype(bf16).astype(f32)
        out.append(p)
        rem = rem - p
    return out


def _ssd_kernel(dt_ref, dtb_ref, alog_ref, z_ref, xs_ref, b_ref, c_ref, dskip_ref, nw_ref,
                segoh_ref, expoh_ref, upper_ref,
                o_ref,
                state_ref, cs_ref, dtc_ref, cst_ref, dtt_ref,
                *, groups, rheads, gp):
    c = pl.program_id(1)
    pg = pl.program_id(2)

    @pl.when(pg == 0)
    def _():
        raw = dt_ref[...] + dtb_ref[...]
        dt = jnp.maximum(raw, 0.0) + jnp.log(1.0 + jnp.exp(-jnp.abs(raw)))
        da_t = (dt * (-jnp.exp(alog_ref[...]))).T
        cs_t = jnp.dot(da_t, upper_ref[...], precision=lax.Precision.HIGHEST,
                       preferred_element_type=f32)
        cst_ref[...] = cs_t
        dtt_ref[...] = dt.T
        cs = cs_t.T
        for gg in range(groups):
            shift = (LANES - gg * rheads) % LANES
            cs_ref[gg] = pltpu.roll(cs, shift, 1) if shift else cs
            dtc_ref[gg] = pltpu.roll(dt, shift, 1) if shift else dt

    @pl.when(c == 0)
    def _():
        state_ref[:, pg] = jnp.zeros((gp,) + state_ref.shape[2:], f32)

    for gi in range(gp):
        _ssd_group(gi, pg, z_ref, xs_ref, b_ref, c_ref, dskip_ref, nw_ref, segoh_ref, expoh_ref,
                   o_ref, state_ref, cs_ref, dtc_ref, cst_ref, dtt_ref, rheads, gp)


def _ssd_group(gi, pg, z_ref, xs_ref, b_ref, c_ref, dskip_ref, nw_ref, segoh_ref, expoh_ref, o_ref,
               state_ref, cs_ref, dtc_ref, cst_ref, dtt_ref, rheads, gp):
    L = SSM_CHUNK
    R = rheads
    xw = rheads * SSM_HEAD_DIM
    n = SSM_STATE
    g = pg * gp + gi
    xsl = slice(gi * xw, (gi + 1) * xw)
    nsl = slice(gi * n, (gi + 1) * n)

    xs = xs_ref[:, xsl].astype(f32)
    bm16 = b_ref[:, nsl]
    cm16 = c_ref[:, nsl]

    row0 = pl.multiple_of(g * rheads, rheads)
    cs_t = cst_ref[pl.ds(row0, rheads), :]
    dt_t = dtt_ref[pl.ds(row0, rheads), :]
    cs_c = cs_ref[g]
    dt_c = dtc_ref[g]
    lane = lax.broadcasted_iota(jnp.int32, (L, LANES), 1)

    hi, mid, lo = _split_bf16(cs_c, 3)
    packed = jnp.where(lane < R, hi, jnp.where(lane < 2 * R, pltpu.roll(mid, R, 1), pltpu.roll(lo, 2 * R, 1)))
    seg_col = jnp.dot(packed.astype(bf16), segoh_ref[...], preferred_element_type=f32)

    cb = lax.dot_general(cm16, bm16, (((1,), (1,)), ((), ())),
                         preferred_element_type=f32)
    l_idx = lax.broadcasted_iota(jnp.int32, (L, L), 0)
    s_idx = lax.broadcasted_iota(jnp.int32, (L, L), 1)
    causal = l_idx >= s_idx
    lo_half = lane < SSM_HEAD_DIM

    y_parts = []
    for j in range(rheads // 2):
        xj = xs[:, j * LANES:(j + 1) * LANES]
        x_lo = jnp.where(lo_half, xj, 0.0).astype(bf16)
        x_hi = jnp.where(lo_half, 0.0, xj).astype(bf16)
        yj = None
        for r, xr in ((2 * j, x_lo), (2 * j + 1, x_hi)):
            seg = seg_col[:, r * LANES:(r + 1) * LANES] - cs_t[r:r + 1, :]
            decay = jnp.exp(jnp.where(causal, seg, -jnp.inf))
            mr = (cb * decay * dt_t[r:r + 1, :]).astype(bf16)
            part = jnp.dot(mr, xr, preferred_element_type=f32)
            yj = part if yj is None else yj + part
        y_parts.append(yj)
    y_diag = jnp.concatenate(y_parts, axis=1)

    cs_last = cs_c[L - 1:L, :]
    e_hi, e_lo = _split_bf16(jnp.exp(cs_c), 2)
    w_hi, w_lo = _split_bf16(jnp.exp(cs_last - cs_c) * dt_c, 2)
    packed2 = jnp.where(lane < R, e_hi,
                        jnp.where(lane < 2 * R, pltpu.roll(e_lo, R, 1),
                                  jnp.where(lane < 3 * R, pltpu.roll(w_hi, 2 * R, 1),
                                            pltpu.roll(w_lo, 3 * R, 1))))
    spread = jnp.dot(packed2.astype(bf16), expoh_ref[...], preferred_element_type=f32)
    e_cs = spread[:, :xw]
    wgt = spread[:, xw:]

    prev = state_ref[gi, pg]
    y_off = jnp.dot(cm16, prev.astype(bf16), preferred_element_type=f32) * e_cs
    new = jnp.dot(bm16.astype(f32).T.astype(bf16), (xs * wgt).astype(bf16), preferred_element_type=f32)
    state_ref[gi, pg] = prev * e_cs[L - 1:L, :] + new

    y = y_diag + y_off + xs * dskip_ref[:, xsl]
    y = y * _silu(z_ref[:, xsl].astype(f32))
    ms = jnp.mean(y * y, axis=-1, keepdims=True)
    o_ref[:, xsl] = (y * lax.rsqrt(ms + EPS) * nw_ref[:, xsl]).astype(o_ref.dtype)


def _ssd(zproj, xbc, dt_raw, dt_bias, a_log, d_skip, norm_w, *, batch, seq, d_inner, groups, gp=2):
    t = zproj.shape[0]
    L = SSM_CHUNK
    nc = seq // L
    heads = d_inner // SSM_HEAD_DIM
    rheads = heads // groups
    assert rheads % 2 == 0 and 4 * rheads <= LANES
    gp = math.gcd(gp, groups)
    xw = rheads * SSM_HEAD_DIM
    n = SSM_STATE
    bw, bn = gp * xw, gp * n
    assert d_inner % bw == 0 and d_inner % bn == 0
    b_blk0 = d_inner // bn
    c_blk0 = b_blk0 + groups // gp
    row = lambda b, c, g: b * nc + c
    pad = lambda a: jnp.pad(a.reshape(1, -1).astype(f32), ((0, 0), (0, LANES - heads)))
    const2 = lambda shape: pl.BlockSpec(shape, lambda b, c, g: (0, 0))
    seg_oh, exp_oh, upper = _ssd_constants(rheads)
    kern = functools.partial(_ssd_kernel, groups=groups, rheads=rheads, gp=gp)
    return pl.pallas_call(
        kern,
        grid=(batch, nc, groups // gp),
        in_specs=[pl.BlockSpec((L, LANES), lambda b, c, g: (row(b, c, g), 0)),
                  const2((1, LANES)),
                  const2((1, LANES)),
                  pl.BlockSpec((L, bw), lambda b, c, g: (row(b, c, g), g)),
                  pl.BlockSpec((L, bw), lambda b, c, g: (row(b, c, g), g)),
                  pl.BlockSpec((L, bn), lambda b, c, g: (row(b, c, g), b_blk0 + g)),
                  pl.BlockSpec((L, bn), lambda b, c, g: (row(b, c, g), c_blk0 + g)),
                  pl.BlockSpec((1, bw), lambda b, c, g: (0, g)),
                  pl.BlockSpec((1, bw), lambda b, c, g: (0, g)),
                  const2((LANES, rheads * LANES)),
                  const2((LANES, 2 * xw)),
                  const2((L, L))],
        out_specs=pl.BlockSpec((L, bw), lambda b, c, g: (row(b, c, g), g)),
        out_shape=jax.ShapeDtypeStruct((t, d_inner), bf16),
        scratch_shapes=[pltpu.VMEM((gp, groups // gp, n, xw), f32),
                        pltpu.VMEM((groups, L, LANES), f32),
                        pltpu.VMEM((groups, L, LANES), f32),
                        pltpu.VMEM((LANES, L), f32),
                        pltpu.VMEM((LANES, L), f32)],
        compiler_params=_params("arbitrary", "arbitrary", "arbitrary"),
        name="ssd_chunk",
    )(dt_raw, pad(dt_bias), pad(a_log), zproj, xbc, xbc, xbc,
      jnp.repeat(d_skip, SSM_HEAD_DIM).reshape(1, -1), norm_w.reshape(1, -1),
      seg_oh, exp_oh, upper)


def kernel(x, c, positions, norm_w, ada_w, ada_b, attn_w_in, attn_q_norm, attn_k_norm, attn_lambda_q1, attn_lambda_k1, attn_lambda_q2, attn_lambda_k2, attn_subln_w, attn_w_out, ssm_w_in, ssm_conv_w, ssm_conv_b, ssm_dt_bias, ssm_A_log, ssm_D, ssm_norm_w, ssm_w_out):
    batch, seq, d = x.shape
    depth = norm_w.shape[0]
    t = batch * seq
    heads = d // (2 * HEAD_DIM)
    qk_width = heads * 2 * HEAD_DIM
    d_inner = ssm_w_out.shape[1]
    conv_dim = ssm_conv_w.shape[2]
    groups = (conv_dim - d_inner) // (2 * SSM_STATE)
    ssm_heads = ssm_dt_bias.shape[1]
    tm = min(1024, seq)

    c_pad = jnp.pad(c, ((0, 8 - batch), (0, 0)))
    mod = _ada_mod(c_pad, ada_w, ada_b)[:, :batch]
    shift, scale, gate = (mod[:, :, i * d:(i + 1) * d].reshape(depth, batch, 1, d) for i in range(3))

    inv_freq = ROPE_THETA ** (-jnp.arange(0, HEAD_DIM, 2, dtype=f32) / HEAD_DIM)
    freq_row = jnp.tile(inv_freq, LANES // (HEAD_DIM // 2)).reshape(1, LANES)
    cos_t, sin_t = _rope_tables(positions.reshape(t, 1), freq_row)

    x2 = x.reshape(t, d)
    for layer in range(depth):
        j = layer // 2
        nw = norm_w[layer].reshape(1, d)
        if layer % 2 == 0:
            lambda_init = 0.8 - 0.6 * math.exp(-0.3 * layer)
            n_in = attn_w_in.shape[2]
            proj = _norm_proj(x2, nw, scale[layer], shift[layer], attn_w_in[j].astype(bf16)[None], 0, n_in,
                              seq, bf16, tm, _tile(n_in, 1024))
            rep = qk_width // HEAD_DIM
            qk_w = jnp.stack([jnp.tile(attn_q_norm[j], rep) * (HEAD_DIM ** -0.5 * math.log2(math.e)),
                              jnp.tile(attn_k_norm[j], rep)]).reshape(2, 1, qk_width)
            qk = _qk_prep(proj, qk_w, cos_t, sin_t, qk_width)
            o = _diff_attn(qk, proj, attn_lambda_q1[j], attn_lambda_k1[j], attn_lambda_q2[j],
                           attn_lambda_k2[j], attn_subln_w[j], batch=batch, seq=seq, heads=heads,
                           lambda_init=lambda_init)
            x2 = _out_proj(o, attn_w_out[j].astype(bf16)[None], 0, x2, gate[layer], seq, tm, _tile(d, 1024))
        else:
            main = d_inner + conv_dim
            w_dt = jnp.pad(ssm_w_in[j][:, main:], ((0, 0), (0, LANES - ssm_heads)))[None]
            w_z = ssm_w_in[j][:, :d_inner].astype(bf16)[None]
            w_xbc = ssm_w_in[j][:, d_inner:main].astype(bf16)[None]
            zproj = _norm_proj(x2, nw, scale[layer], shift[layer], w_z, 0, d_inner, seq, bf16, tm,
                               _tile(d_inner, 1024))
            xbc = _norm_proj_conv(x2, nw, scale[layer], shift[layer], w_xbc, ssm_conv_w[j],
                                  ssm_conv_b[j].reshape(1, -1), seq, tm, _tile(conv_dim, 1024))
            dt_raw = _norm_proj(x2, nw, scale[layer], shift[layer], w_dt, 0, LANES, seq, f32, tm, LANES)
            y = _ssd(zproj, xbc, dt_raw, ssm_dt_bias[j], ssm_A_log[j], ssm_D[j], ssm_norm_w[j],
                     batch=batch, seq=seq, d_inner=d_inner, groups=groups)
            x2 = _out_proj(y, ssm_w_out[j].astype(bf16)[None], 0, x2, gate[layer], seq, tm, _tile(d, 512))
    return x2.reshape(batch, seq, d)
```

```python
import functools
import math

import numpy as np
import jax
import jax.numpy as jnp
from jax import lax
from jax.experimental import pallas as pl
from jax.experimental.pallas import tpu as pltpu

EPS = 1e-6
ROPE_THETA = 10000.0
LANES = 128
BF16_ROWS = 16
CONV_HALO = 8
CONV_ROW_CHUNKS = 4
HEAD_DIM = 64
SSM_HEAD_DIM = 64
SSM_STATE = 128
SSM_CHUNK = 128
SSM_CONV = 4
VMEM_LIMIT_BYTES = 56 * 1024 * 1024
NEG_BIG = -1e30

f32 = jnp.float32
bf16 = jnp.bfloat16


def _params(*sem):
    return pltpu.CompilerParams(dimension_semantics=sem, vmem_limit_bytes=VMEM_LIMIT_BYTES)


def _silu(x):
    return x * jax.nn.sigmoid(x)


def _tile(n, target):
    if n <= target:
        return n
    best = None
    for cand in range(LANES, target + 1, LANES):
        if n % cand == 0:
            best = cand
    assert best is not None, (n, target)
    return best


def _ada_kernel(c_ref, w_ref, b_ref, o_ref):
    cond = _silu(c_ref[...])
    o_ref[0] = jnp.dot(cond, w_ref[0], preferred_element_type=f32) + b_ref[0]


def _ada_mod(c_pad, ada_w, ada_b, tn=1024):
    depth, d, n = ada_w.shape
    rows = c_pad.shape[0]
    tn = _tile(n, tn)
    return pl.pallas_call(
        _ada_kernel,
        grid=(depth, n // tn),
        in_specs=[pl.BlockSpec((rows, d), lambda l, j: (0, 0)),
                  pl.BlockSpec((1, d, tn), lambda l, j: (l, 0, j)),
                  pl.BlockSpec((1, 1, tn), lambda l, j: (l, 0, j))],
        out_specs=pl.BlockSpec((1, rows, tn), lambda l, j: (l, 0, j)),
        out_shape=jax.ShapeDtypeStruct((depth, rows, n), f32),
        compiler_params=_params("arbitrary", "arbitrary"),
        name="ada_mod",
    )(c_pad, ada_w, ada_b.reshape(depth, 1, n))


def _norm_kernel(x_ref, nw_ref, sc_ref, sh_ref, o_ref):
    x = x_ref[...]
    ms = jnp.mean(x * x, axis=-1, keepdims=True)
    y = x * lax.rsqrt(ms + EPS) * nw_ref[...]
    o_ref[...] = (y * (1.0 + sc_ref[0]) + sh_ref[0]).astype(o_ref.dtype)


def _norm_mod(x2, nw, scale, shift, seq, tm=512):
    t, d = x2.shape
    tm = min(tm, seq)
    tiles_per_seq = seq // tm
    return pl.pallas_call(
        _norm_kernel,
        grid=(t // tm,),
        in_specs=[pl.BlockSpec((tm, d), lambda i: (i, 0)),
                  pl.BlockSpec((1, d), lambda i: (0, 0)),
                  pl.BlockSpec((1, 1, d), lambda i: (i // tiles_per_seq, 0, 0)),
                  pl.BlockSpec((1, 1, d), lambda i: (i // tiles_per_seq, 0, 0))],
        out_specs=pl.BlockSpec((tm, d), lambda i: (i, 0)),
        out_shape=jax.ShapeDtypeStruct((t, d), bf16),
        compiler_params=_params("arbitrary"),
        name="norm_mod",
    )(x2, nw, scale, shift)


def _stage_weight(w_ref, wb_ref):
    @pl.when(pl.program_id(1) == 0)
    def _():
        wb_ref[...] = w_ref[0].astype(bf16)


def _proj_kernel(a_ref, w_ref, o_ref, wb_ref, *, silu):
    _stage_weight(w_ref, wb_ref)
    acc = jnp.dot(a_ref[...], wb_ref[...], preferred_element_type=f32)
    if silu:
        acc = acc / (1.0 + jnp.exp(-acc))
    o_ref[...] = acc.astype(o_ref.dtype)


def _proj(a, w, w_layer, n_out, out_dtype, tm, tn, silu=False):
    t, k = a.shape
    kern = functools.partial(_proj_kernel, silu=silu)
    return pl.pallas_call(
        kern,
        grid=(n_out // tn, t // tm),
        in_specs=[pl.BlockSpec((tm, k), lambda j, i: (i, 0)),
                  pl.BlockSpec((1, k, tn), lambda j, i: (w_layer, 0, j))],
        out_specs=pl.BlockSpec((tm, tn), lambda j, i: (i, j)),
        out_shape=jax.ShapeDtypeStruct((t, n_out), out_dtype),
        scratch_shapes=[pltpu.VMEM((k, tn), bf16)],
        compiler_params=_params("arbitrary", "arbitrary"),
        name="proj",
    )(a, w)


def _proj_conv_kernel(a_ref, w_ref, cw_ref, cb_ref, o_ref, wb_ref, halo_ref, *, tiles_per_seq):
    i = pl.program_id(1)
    _stage_weight(w_ref, wb_ref)

    tm, tn = o_ref.shape
    rc = tm // CONV_ROW_CHUNKS
    w = wb_ref[...]
    first = (i % tiles_per_seq) == 0
    prev = jnp.where(first, 0.0, halo_ref[...])
    sub = lax.broadcasted_iota(jnp.int32, (rc // CONV_HALO, CONV_HALO, tn), 1)
    for r in range(CONV_ROW_CHUNKS):
        acc = jnp.dot(a_ref[r * rc:(r + 1) * rc, :], w, preferred_element_type=f32)
        slabs = jnp.concatenate([prev, acc], axis=0).reshape(rc // CONV_HALO + 1, CONV_HALO, tn)
        y = cb_ref[...] + cw_ref[SSM_CONV - 1:SSM_CONV, :] * acc
        for tap in range(SSM_CONV - 1):
            back = SSM_CONV - 1 - tap
            rot = pltpu.roll(slabs, back, 1)
            shifted = jnp.where(sub < back, rot[:-1], rot[1:]).reshape(rc, tn)
            y = y + cw_ref[tap:tap + 1, :] * shifted
        o_ref[r * rc:(r + 1) * rc, :] = (y / (1.0 + jnp.exp(-y))).astype(o_ref.dtype)
        prev = acc[rc - CONV_HALO:, :]
    halo_ref[...] = prev


def _proj_conv(a, w, conv_w, conv_b, seq, tm, tn):
    t, k = a.shape
    n = w.shape[2]
    tiles_per_seq = seq // tm
    kern = functools.partial(_proj_conv_kernel, tiles_per_seq=tiles_per_seq)
    return pl.pallas_call(
        kern,
        grid=(n // tn, t // tm),
        in_specs=[pl.BlockSpec((tm, k), lambda j, i: (i, 0)),
                  pl.BlockSpec((1, k, tn), lambda j, i: (0, 0, j)),
                  pl.BlockSpec((SSM_CONV, tn), lambda j, i: (0, j)),
                  pl.BlockSpec((1, tn), lambda j, i: (0, j))],
        out_specs=pl.BlockSpec((tm, tn), lambda j, i: (i, j)),
        out_shape=jax.ShapeDtypeStruct((t, n), bf16),
        scratch_shapes=[pltpu.VMEM((k, tn), bf16),
                        pltpu.VMEM((CONV_HALO, tn), f32)],
        compiler_params=_params("arbitrary", "arbitrary"),
        name="proj_conv",
    )(a, w, conv_w, conv_b)


def _outproj_kernel(a_ref, w_ref, x_ref, g_ref, o_ref, wb_ref):
    _stage_weight(w_ref, wb_ref)
    acc = jnp.dot(a_ref[...], wb_ref[...], preferred_element_type=f32)
    o_ref[...] = x_ref[...] + g_ref[0] * acc


def _out_proj(a, w, w_layer, x2, gate, seq, tm, tn):
    t, k = a.shape
    n = w.shape[2]
    tiles_per_seq = seq // tm
    return pl.pallas_call(
        _outproj_kernel,
        grid=(n // tn, t // tm),
        in_specs=[pl.BlockSpec((tm, k), lambda j, i: (i, 0)),
                  pl.BlockSpec((1, k, tn), lambda j, i: (w_layer, 0, j)),
                  pl.BlockSpec((tm, tn), lambda j, i: (i, j)),
                  pl.BlockSpec((1, 1, tn), lambda j, i: (i // tiles_per_seq, 0, j))],
        out_specs=pl.BlockSpec((tm, tn), lambda j, i: (i, j)),
        out_shape=jax.ShapeDtypeStruct((t, n), f32),
        scratch_shapes=[pltpu.VMEM((k, tn), bf16)],
        compiler_params=_params("arbitrary", "arbitrary"),
        name="out_proj",
    )(a, w, x2, gate)


def _rope_table_kernel(pos_ref, freq_ref, cos_ref, sin_ref):
    ang = pos_ref[...].astype(f32) * freq_ref[...]
    lane = lax.broadcasted_iota(jnp.int32, ang.shape, 1)
    first_half = (lane & (HEAD_DIM // 2)) == 0
    s = jnp.sin(ang)
    cos_ref[...] = jnp.cos(ang)
    sin_ref[...] = jnp.where(first_half, -s, s)


def _rope_tables(pos_col, freq_row, tm=1024):
    t = pos_col.shape[0]
    tm = min(tm, t)
    return pl.pallas_call(
        _rope_table_kernel,
        grid=(t // tm,),
        in_specs=[pl.BlockSpec((tm, 1), lambda i: (i, 0)),
                  pl.BlockSpec((1, LANES), lambda i: (0, 0))],
        out_specs=[pl.BlockSpec((tm, LANES), lambda i: (i, 0)),
                   pl.BlockSpec((tm, LANES), lambda i: (i, 0))],
        out_shape=[jax.ShapeDtypeStruct((t, LANES), f32)] * 2,
        compiler_params=_params("arbitrary"),
        name="rope_table",
    )(pos_col, freq_row)


def _qkprep_kernel(p_ref, w_ref, cos_ref, sin_ref, o_ref):
    width = p_ref.shape[1]
    r = lax.broadcasted_iota(jnp.int32, (LANES, LANES), 0) // HEAD_DIM
    c = lax.broadcasted_iota(jnp.int32, (LANES, LANES), 1) // HEAD_DIM
    group_mean = jnp.where(r == c, 1.0 / HEAD_DIM, 0.0).astype(bf16)
    cos = cos_ref[...]
    sin = sin_ref[...]
    lane = lax.broadcasted_iota(jnp.int32, cos.shape, 1)
    first_half = (lane & (HEAD_DIM // 2)) == 0
    for j in range(width // LANES):
        sl = slice(j * LANES, (j + 1) * LANES)
        x = p_ref[:, sl].astype(f32)
        ms = jnp.dot((x * x).astype(bf16), group_mean, preferred_element_type=f32)
        y = x * lax.rsqrt(ms + EPS) * w_ref[0, :, sl]
        partner = jnp.where(first_half,
                            pltpu.roll(y, LANES - HEAD_DIM // 2, 1),
                            pltpu.roll(y, HEAD_DIM // 2, 1))
        o_ref[:, sl] = (y * cos + partner * sin).astype(o_ref.dtype)


def _qk_prep(proj, qk_w, cos_t, sin_t, width, tm=512):
    t = proj.shape[0]
    tm = min(tm, t)
    return pl.pallas_call(
        _qkprep_kernel,
        grid=(t // tm, 2),
        in_specs=[pl.BlockSpec((tm, width), lambda i, j: (i, j)),
                  pl.BlockSpec((1, 1, width), lambda i, j: (j, 0, 0)),
                  pl.BlockSpec((tm, LANES), lambda i, j: (i, 0)),
                  pl.BlockSpec((tm, LANES), lambda i, j: (i, 0))],
        out_specs=pl.BlockSpec((tm, width), lambda i, j: (i, j)),
        out_shape=jax.ShapeDtypeStruct((t, 2 * width), bf16),
        compiler_params=_params("arbitrary", "arbitrary"),
        name="qk_prep",
    )(proj, qk_w, cos_t, sin_t)


def _attn_kernel(q_ref, k_ref, v_ref, g_ref, lq1_ref, lk1_ref, lq2_ref, lk2_ref, sw_ref, bias_ref, o_ref,
                 vt_ref, qq_ref, s_ref, acc_ref, *, bq, bk, hp, lambda_init):
    seq = q_ref.shape[0]
    nslot = bq // bk
    lam = (jnp.exp(jnp.sum(lq1_ref[...] * lk1_ref[...], axis=-1, keepdims=True))
           - jnp.exp(jnp.sum(lq2_ref[...] * lk2_ref[...], axis=-1, keepdims=True))
           + lambda_init)
    cols = [slice(hh * LANES, (hh + 1) * LANES) for hh in range(hp)]

    ones_rows = jnp.where(lax.broadcasted_iota(jnp.int32, (BF16_ROWS, bk), 0) == 0, 1.0, 0.0).astype(bf16)

    def transpose_v(c, _):
        start = pl.multiple_of(c * bk, bk)
        for hh in range(hp):
            vt_ref[hh, c, 0:LANES, :] = v_ref[pl.ds(start, bk), cols[hh]].astype(f32).T.astype(bf16)
            vt_ref[hh, c, LANES:, :] = ones_rows
        return 0
    lax.fori_loop(0, seq // bk, transpose_v, 0)

    d_row = lax.broadcasted_iota(jnp.int32, (LANES, bq), 0)

    def scores(hh, j):
        start = pl.multiple_of(j * bk, bk)
        return jnp.dot(k_ref[pl.ds(start, bk), cols[hh]], qq_ref[hh], preferred_element_type=f32)

    def consume(hh, slot, j, m, masked):
        st = s_ref[hh, slot]
        if masked:
            st = st + bias_ref[slot]
        m_new = jnp.maximum(m, jnp.max(st, axis=0, keepdims=True))
        alpha = jnp.exp2(m - m_new)
        p = jnp.exp2(st - m_new)
        acc_ref[hh] = alpha * acc_ref[hh] + jnp.dot(vt_ref[hh, j], p.astype(bf16),
                                                    preferred_element_type=f32)
        return m_new

    def q_block(i, _):
        q_start = pl.multiple_of(i * bq, bq)
        for hh in range(hp):
            qt = q_ref[pl.ds(q_start, bq), cols[hh]].astype(f32).T
            qq_ref[hh] = jnp.concatenate([jnp.where(d_row < HEAD_DIM, qt, 0.0),
                                          jnp.where(d_row >= HEAD_DIM, qt, 0.0)], axis=1).astype(bf16)
            acc_ref[hh] = jnp.zeros(acc_ref.shape[1:], f32)
            for slot in range(nslot):
                s_ref[hh, slot] = scores(hh, slot)

        def body(jj, carry):
            out = []
            for hh in range(hp):
                m = carry[hh]
                for slot in range(nslot):
                    m = consume(hh, slot, jj * nslot + slot, m, False)
                    s_ref[hh, slot] = scores(hh, (jj + 1) * nslot + slot)
                out.append(m)
            return tuple(out)

        init = tuple(jnp.full((1, 2 * bq), NEG_BIG, f32) for _ in range(hp))
        carry = lax.fori_loop(0, i, body, init)

        for hh in range(hp):
            m = carry[hh]
            for slot in range(nslot):
                m = consume(hh, slot, i * nslot + slot, m, True)
            acc = acc_ref[hh]
            ot = acc[:LANES] / acc[LANES:LANES + 1]
            odt = ot[:, :bq] - lam * ot[:, bq:]
            ms = jnp.mean(odt * odt, axis=0, keepdims=True)
            y = (odt * lax.rsqrt(ms + EPS)).T * (sw_ref[...] * (1.0 - lambda_init))
            g = g_ref[pl.ds(q_start, bq), cols[hh]].astype(f32)
            o_ref[pl.ds(q_start, bq), cols[hh]] = (y * _silu(g)).astype(o_ref.dtype)
        return 0

    lax.fori_loop(0, seq // bq, q_block, 0)


def _diff_attn(qk, proj, lq1, lk1, lq2, lk2, subln_w, *, batch, seq, heads, lambda_init,
               bq=512, bk=256, hp=4):
    t = qk.shape[0]
    bq = min(bq, seq)
    bk = min(bk, bq)
    hp = math.gcd(hp, heads)
    nslot = bq // bk
    w = hp * LANES
    ng = heads // hp
    key = np.arange(bq)[:, None]
    qry = np.arange(bq)[None, :]
    tri = np.where(key <= qry, 0.0, NEG_BIG).astype(np.float32)
    bias = jnp.asarray(np.concatenate([tri, tri], axis=1).reshape(nslot, bk, 2 * bq))
    vec = lambda a: a.reshape(1, -1).astype(f32)
    small = pl.BlockSpec((1, HEAD_DIM), lambda b, h: (0, 0))
    single = pl.Buffered(1)
    kern = functools.partial(_attn_kernel, bq=bq, bk=bk, hp=hp, lambda_init=lambda_init)
    return pl.pallas_call(
        kern,
        grid=(batch, ng),
        in_specs=[pl.BlockSpec((seq, w), lambda b, h: (b, h), pipeline_mode=single),
                  pl.BlockSpec((seq, w), lambda b, h: (b, ng + h)),
                  pl.BlockSpec((seq, w), lambda b, h: (b, 2 * ng + h)),
                  pl.BlockSpec((seq, w), lambda b, h: (b, 3 * ng + h), pipeline_mode=single),
                  small, small, small, small,
                  pl.BlockSpec((1, LANES), lambda b, h: (0, 0)),
                  pl.BlockSpec((nslot, bk, 2 * bq), lambda b, h: (0, 0, 0), pipeline_mode=single)],
        out_specs=pl.BlockSpec((seq, w), lambda b, h: (b, h)),
        out_shape=jax.ShapeDtypeStruct((t, heads * LANES), bf16),
        scratch_shapes=[pltpu.VMEM((hp, seq // bk, LANES + BF16_ROWS, bk), bf16),
                        pltpu.VMEM((hp, LANES, 2 * bq), bf16),
                        pltpu.VMEM((hp, nslot, bk, 2 * bq), f32),
                        pltpu.VMEM((hp, LANES + BF16_ROWS, 2 * bq), f32)],
        compiler_params=_params("arbitrary", "arbitrary"),
        name="diff_attn",
    )(qk, qk, proj, proj, vec(lq1), vec(lk1), vec(lq2), vec(lk2), vec(subln_w), bias)


def _ssd_constants(rheads):
    L, P = SSM_CHUNK, SSM_HEAD_DIM
    k = np.arange(LANES)[:, None]
    col = np.arange(rheads * LANES)[None, :]
    seg = (k < 3 * rheads) & (k % rheads == col // LANES)
    col2 = np.arange(2 * rheads * P)[None, :]
    half = rheads * P
    exp = np.where(col2 < half,
                   (k < 2 * rheads) & (k % rheads == col2 // P),
                   (k >= 2 * rheads) & (k < 4 * rheads) & (k % rheads == (col2 - half) // P))
    upper = (np.arange(L)[:, None] <= np.arange(L)[None, :])
    return jnp.asarray(seg, dtype=bf16), jnp.asarray(exp, dtype=bf16), jnp.asarray(upper, dtype=f32)


def _split_bf16(a, pieces):
    out, rem = [], a
    for _ in range(pieces):
        p = rem.astype(bf16).astype(f32)
        out.append(p)
        rem = rem - p
    return out


def _ssd_kernel(dt_ref, dtb_ref, alog_ref, z_ref, xs_ref, b_ref, c_ref, dskip_ref, nw_ref,
                segoh_ref, expoh_ref, upper_ref,
                o_ref,
                state_ref, cs_ref, dtc_ref, cst_ref, dtt_ref,
                *, groups, rheads, gp):
    c = pl.program_id(1)
    pg = pl.program_id(2)

    @pl.when(pg == 0)
    def _():
        raw = dt_ref[...] + dtb_ref[...]
        dt = jnp.maximum(raw, 0.0) + jnp.log(1.0 + jnp.exp(-jnp.abs(raw)))
        da_t = (dt * (-jnp.exp(alog_ref[...]))).T
        cs_t = jnp.dot(da_t, upper_ref[...], precision=lax.Precision.HIGHEST,
                       preferred_element_type=f32)
        cst_ref[...] = cs_t
        dtt_ref[...] = dt.T
        cs = cs_t.T
        for gg in range(groups):
            shift = (LANES - gg * rheads) % LANES
            cs_ref[gg] = pltpu.roll(cs, shift, 1) if shift else cs
            dtc_ref[gg] = pltpu.roll(dt, shift, 1) if shift else dt

    @pl.when(c == 0)
    def _():
        state_ref[:, pg] = jnp.zeros((gp,) + state_ref.shape[2:], f32)

    for gi in range(gp):
        _ssd_group(gi, pg, z_ref, xs_ref, b_ref, c_ref, dskip_ref, nw_ref, segoh_ref, expoh_ref,
                   o_ref, state_ref, cs_ref, dtc_ref, cst_ref, dtt_ref, rheads, gp)


def _ssd_group(gi, pg, z_ref, xs_ref, b_ref, c_ref, dskip_ref, nw_ref, segoh_ref, expoh_ref, o_ref,
               state_ref, cs_ref, dtc_ref, cst_ref, dtt_ref, rheads, gp):
    L = SSM_CHUNK
    R = rheads
    xw = rheads * SSM_HEAD_DIM
    n = SSM_STATE
    g = pg * gp + gi
    xsl = slice(gi * xw, (gi + 1) * xw)
    nsl = slice(gi * n, (gi + 1) * n)

    xs = xs_ref[:, xsl].astype(f32)
    bm16 = b_ref[:, nsl]
    cm16 = c_ref[:, nsl]

    row0 = pl.multiple_of(g * rheads, rheads)
    cs_t = cst_ref[pl.ds(row0, rheads), :]
    dt_t = dtt_ref[pl.ds(row0, rheads), :]
    cs_c = cs_ref[g]
    dt_c = dtc_ref[g]
    lane = lax.broadcasted_iota(jnp.int32, (L, LANES), 1)

    hi, mid, lo = _split_bf16(cs_c, 3)
    packed = jnp.where(lane < R, hi, jnp.where(lane < 2 * R, pltpu.roll(mid, R, 1), pltpu.roll(lo, 2 * R, 1)))
    seg_col = jnp.dot(packed.astype(bf16), segoh_ref[...], preferred_element_type=f32)

    cb = lax.dot_general(cm16, bm16, (((1,), (1,)), ((), ())),
                         preferred_element_type=f32)
    l_idx = lax.broadcasted_iota(jnp.int32, (L, L), 0)
    s_idx = lax.broadcasted_iota(jnp.int32, (L, L), 1)
    causal = l_idx >= s_idx
    lo_half = lane < SSM_HEAD_DIM

    y_parts = []
    for j in range(rheads // 2):
        xj = xs[:, j * LANES:(j + 1) * LANES]
        x_lo = jnp.where(lo_half, xj, 0.0).astype(bf16)
        x_hi = jnp.where(lo_half, 0.0, xj).astype(bf16)
        yj = None
        for r, xr in ((2 * j, x_lo), (2 * j + 1, x_hi)):
            seg = seg_col[:, r * LANES:(r + 1) * LANES] - cs_t[r:r + 1, :]
            decay = jnp.exp(jnp.where(causal, seg, -jnp.inf))
            mr = (cb * decay * dt_t[r:r + 1, :]).astype(bf16)
            part = jnp.dot(mr, xr, preferred_element_type=f32)
            yj = part if yj is None else yj + part
        y_parts.append(yj)
    y_diag = jnp.concatenate(y_parts, axis=1)

    cs_last = cs_c[L - 1:L, :]
    e_hi, e_lo = _split_bf16(jnp.exp(cs_c), 2)
    w_hi, w_lo = _split_bf16(jnp.exp(cs_last - cs_c) * dt_c, 2)
    packed2 = jnp.where(lane < R, e_hi,
                        jnp.where(lane < 2 * R, pltpu.roll(e_lo, R, 1),
                                  jnp.where(lane < 3 * R, pltpu.roll(w_hi, 2 * R, 1),
                                            pltpu.roll(w_lo, 3 * R, 1))))
    spread = jnp.dot(packed2.astype(bf16), expoh_ref[...], preferred_element_type=f32)
    e_cs = spread[:, :xw]
    wgt = spread[:, xw:]

    prev = state_ref[gi, pg]
    y_off = jnp.dot(cm16, prev.astype(bf16), preferred_element_type=f32) * e_cs
    new = jnp.dot(bm16.astype(f32).T.astype(bf16), (xs * wgt).astype(bf16), preferred_element_type=f32)
    state_ref[gi, pg] = prev * e_cs[L - 1:L, :] + new

    y = y_diag + y_off + xs * dskip_ref[:, xsl]
    y = y * z_ref[:, xsl].astype(f32)
    ms = jnp.mean(y * y, axis=-1, keepdims=True)
    o_ref[:, xsl] = (y * lax.rsqrt(ms + EPS) * nw_ref[:, xsl]).astype(o_ref.dtype)


def _ssd(zproj, xbc, dt_raw, dt_bias, a_log, d_skip, norm_w, *, batch, seq, d_inner, groups, gp=2):
    t = zproj.shape[0]
    L = SSM_CHUNK
    nc = seq // L
    heads = d_inner // SSM_HEAD_DIM
    rheads = heads // groups
    assert rheads % 2 == 0 and 4 * rheads <= LANES
    gp = math.gcd(gp, groups)
    xw = rheads * SSM_HEAD_DIM
    n = SSM_STATE
    bw, bn = gp * xw, gp * n
    assert d_inner % bw == 0 and d_inner % bn == 0
    b_blk0 = d_inner // bn
    c_blk0 = b_blk0 + groups // gp
    row = lambda b, c, g: b * nc + c
    pad = lambda a: jnp.pad(a.reshape(1, -1).astype(f32), ((0, 0), (0, LANES - heads)))
    const2 = lambda shape: pl.BlockSpec(shape, lambda b, c, g: (0, 0))
    seg_oh, exp_oh, upper = _ssd_constants(rheads)
    kern = functools.partial(_ssd_kernel, groups=groups, rheads=rheads, gp=gp)
    return pl.pallas_call(
        kern,
        grid=(batch, nc, groups // gp),
        in_specs=[pl.BlockSpec((L, LANES), lambda b, c, g: (row(b, c, g), 0)),
                  const2((1, LANES)),
                  const2((1, LANES)),
                  pl.BlockSpec((L, bw), lambda b, c, g: (row(b, c, g), g)),
                  pl.BlockSpec((L, bw), lambda b, c, g: (row(b, c, g), g)),
                  pl.BlockSpec((L, bn), lambda b, c, g: (row(b, c, g), b_blk0 + g)),
                  pl.BlockSpec((L, bn), lambda b, c, g: (row(b, c, g), c_blk0 + g)),
                  pl.BlockSpec((1, bw), lambda b, c, g: (0, g)),
                  pl.BlockSpec((1, bw), lambda b, c, g: (0, g)),
                  const2((LANES, rheads * LANES)),
                  const2((LANES, 2 * xw)),
                  const2((L, L))],
        out_specs=pl.BlockSpec((L, bw), lambda b, c, g: (row(b, c, g), g)),
        out_shape=jax.ShapeDtypeStruct((t, d_inner), bf16),
        scratch_shapes=[pltpu.VMEM((gp, groups // gp, n, xw), f32),
                        pltpu.VMEM((groups, L, LANES), f32),
                        pltpu.VMEM((groups, L, LANES), f32),
                        pltpu.VMEM((LANES, L), f32),
                        pltpu.VMEM((LANES, L), f32)],
        compiler_params=_params("arbitrary", "arbitrary", "arbitrary"),
        name="ssd_chunk",
    )(dt_raw, pad(dt_bias), pad(a_log), zproj, xbc, xbc, xbc,
      jnp.repeat(d_skip, SSM_HEAD_DIM).reshape(1, -1), norm_w.reshape(1, -1),
      seg_oh, exp_oh, upper)


def kernel(x, c, positions, norm_w, ada_w, ada_b, attn_w_in, attn_q_norm, attn_k_norm, attn_lambda_q1, attn_lambda_k1, attn_lambda_q2, attn_lambda_k2, attn_subln_w, attn_w_out, ssm_w_in, ssm_conv_w, ssm_conv_b, ssm_dt_bias, ssm_A_log, ssm_D, ssm_norm_w, ssm_w_out):
    batch, seq, d = x.shape
    depth = norm_w.shape[0]
    t = batch * seq
    heads = d // (2 * HEAD_DIM)
    qk_width = heads * 2 * HEAD_DIM
    d_inner = ssm_w_out.shape[1]
    conv_dim = ssm_conv_w.shape[2]
    groups = (conv_dim - d_inner) // (2 * SSM_STATE)
    ssm_heads = ssm_dt_bias.shape[1]
    tm = min(1024, seq)

    c_pad = jnp.pad(c, ((0, 8 - batch), (0, 0)))
    mod = _ada_mod(c_pad, ada_w, ada_b)[:, :batch]
    shift, scale, gate = (mod[:, :, i * d:(i + 1) * d].reshape(depth, batch, 1, d) for i in range(3))

    inv_freq = ROPE_THETA ** (-jnp.arange(0, HEAD_DIM, 2, dtype=f32) / HEAD_DIM)
    freq_row = jnp.tile(inv_freq, LANES // (HEAD_DIM // 2)).reshape(1, LANES)
    cos_t, sin_t = _rope_tables(positions.reshape(t, 1), freq_row)

    x2 = x.reshape(t, d)
    for layer in range(depth):
        j = layer // 2
        h = _norm_mod(x2, norm_w[layer].reshape(1, d), scale[layer], shift[layer], seq)
        if layer % 2 == 0:
            lambda_init = 0.8 - 0.6 * math.exp(-0.3 * layer)
            n_in = attn_w_in.shape[2]
            proj = _proj(h, attn_w_in, j, n_in, bf16, tm, _tile(n_in, 1024))
            rep = qk_width // HEAD_DIM
            qk_w = jnp.stack([jnp.tile(attn_q_norm[j], rep) * (HEAD_DIM ** -0.5 * math.log2(math.e)),
                              jnp.tile(attn_k_norm[j], rep)]).reshape(2, 1, qk_width)
            qk = _qk_prep(proj, qk_w, cos_t, sin_t, qk_width)
            o = _diff_attn(qk, proj, attn_lambda_q1[j], attn_lambda_k1[j], attn_lambda_q2[j],
                           attn_lambda_k2[j], attn_subln_w[j], batch=batch, seq=seq, heads=heads,
                           lambda_init=lambda_init)
            x2 = _out_proj(o, attn_w_out, j, x2, gate[layer], seq, tm, _tile(d, 1024))
        else:
            main = d_inner + conv_dim
            w_dt = jnp.pad(ssm_w_in[j][:, main:], ((0, 0), (0, LANES - ssm_heads)))[None]
            w_z = ssm_w_in[j][:, :d_inner].astype(bf16)[None]
            w_xbc = ssm_w_in[j][:, d_inner:main].astype(bf16)[None]
            zact = _proj(h, w_z, 0, d_inner, bf16, tm, _tile(d_inner, 1024), silu=True)
            xbc = _proj_conv(h, w_xbc, ssm_conv_w[j], ssm_conv_b[j].reshape(1, -1), seq, tm,
                             _tile(conv_dim, 1024))
            dt_raw = _proj(h, w_dt, 0, LANES, f32, tm, LANES)
            y = _ssd(zact, xbc, dt_raw, ssm_dt_bias[j], ssm_A_log[j], ssm_D[j], ssm_norm_w[j],
                     batch=batch, seq=seq, d_inner=d_inner, groups=groups)
            x2 = _out_proj(y, ssm_w_out, j, x2, gate[layer], seq, tm, _tile(d, 512))
    return x2.reshape(batch, seq, d)
```

```python
import functools
import math

import numpy as np
import jax
import jax.numpy as jnp
from jax import lax
from jax.experimental import pallas as pl
from jax.experimental.pallas import tpu as pltpu

EPS = 1e-6
ROPE_THETA = 10000.0
LANES = 128
BF16_ROWS = 16
CONV_HALO = 8
EPILOGUE_ROW_CHUNKS = 4
HEAD_DIM = 64
SSM_HEAD_DIM = 64
SSM_STATE = 128
SSM_CHUNK = 128
SSM_CONV = 4
VMEM_LIMIT_BYTES = 56 * 1024 * 1024
NEG_BIG = -1e30

f32 = jnp.float32
bf16 = jnp.bfloat16


def _params(*sem):
    return pltpu.CompilerParams(dimension_semantics=sem, vmem_limit_bytes=VMEM_LIMIT_BYTES)


def _silu(x):
    return x * jax.nn.sigmoid(x)


def _tile(n, target):
    if n <= target:
        return n
    best = None
    for cand in range(LANES, target + 1, LANES):
        if n % cand == 0:
            best = cand
    assert best is not None, (n, target)
    return best


def _ada_kernel(c_ref, w_ref, b_ref, o_ref):
    cond = _silu(c_ref[...])
    o_ref[0] = jnp.dot(cond, w_ref[0], preferred_element_type=f32) + b_ref[0]


def _ada_mod(c_pad, ada_w, ada_b, tn=1024):
    depth, d, n = ada_w.shape
    rows = c_pad.shape[0]
    tn = _tile(n, tn)
    return pl.pallas_call(
        _ada_kernel,
        grid=(depth, n // tn),
        in_specs=[pl.BlockSpec((rows, d), lambda l, j: (0, 0)),
                  pl.BlockSpec((1, d, tn), lambda l, j: (l, 0, j)),
                  pl.BlockSpec((1, 1, tn), lambda l, j: (l, 0, j))],
        out_specs=pl.BlockSpec((1, rows, tn), lambda l, j: (l, 0, j)),
        out_shape=jax.ShapeDtypeStruct((depth, rows, n), f32),
        compiler_params=_params("arbitrary", "arbitrary"),
        name="ada_mod",
    )(c_pad, ada_w, ada_b.reshape(depth, 1, n))


def _norm_kernel(x_ref, nw_ref, sc_ref, sh_ref, o_ref):
    x = x_ref[...]
    ms = jnp.mean(x * x, axis=-1, keepdims=True)
    y = x * lax.rsqrt(ms + EPS) * nw_ref[...]
    o_ref[...] = (y * (1.0 + sc_ref[0]) + sh_ref[0]).astype(o_ref.dtype)


def _norm_mod(x2, nw, scale, shift, seq, tm=512):
    t, d = x2.shape
    tm = min(tm, seq)
    tiles_per_seq = seq // tm
    return pl.pallas_call(
        _norm_kernel,
        grid=(t // tm,),
        in_specs=[pl.BlockSpec((tm, d), lambda i: (i, 0)),
                  pl.BlockSpec((1, d), lambda i: (0, 0)),
                  pl.BlockSpec((1, 1, d), lambda i: (i // tiles_per_seq, 0, 0)),
                  pl.BlockSpec((1, 1, d), lambda i: (i // tiles_per_seq, 0, 0))],
        out_specs=pl.BlockSpec((tm, d), lambda i: (i, 0)),
        out_shape=jax.ShapeDtypeStruct((t, d), bf16),
        compiler_params=_params("arbitrary"),
        name="norm_mod",
    )(x2, nw, scale, shift)


def _stage_weight(w_ref, wb_ref):
    @pl.when(pl.program_id(1) == 0)
    def _():
        wb_ref[...] = w_ref[0].astype(bf16)


def _proj_kernel(a_ref, w_ref, o_ref, wb_ref, *, silu):
    _stage_weight(w_ref, wb_ref)
    acc = jnp.dot(a_ref[...], wb_ref[...], preferred_element_type=f32)
    if silu:
        acc = acc / (1.0 + jnp.exp(-acc))
    o_ref[...] = acc.astype(o_ref.dtype)


def _proj(a, w, w_layer, n_out, out_dtype, tm, tn, silu=False, col0=0):
    t, k = a.shape
    assert col0 % tn == 0
    cb0 = col0 // tn
    kern = functools.partial(_proj_kernel, silu=silu)
    return pl.pallas_call(
        kern,
        grid=(n_out // tn, t // tm),
        in_specs=[pl.BlockSpec((tm, k), lambda j, i: (i, 0)),
                  pl.BlockSpec((1, k, tn), lambda j, i: (w_layer, 0, cb0 + j))],
        out_specs=pl.BlockSpec((tm, tn), lambda j, i: (i, j)),
        out_shape=jax.ShapeDtypeStruct((t, n_out), out_dtype),
        scratch_shapes=[pltpu.VMEM((k, tn), bf16)],
        compiler_params=_params("arbitrary", "arbitrary"),
        name="proj",
    )(a, w)


def _proj_qk_kernel(a_ref, w_ref, qw_ref, cos_ref, sin_ref, o_ref, wb_ref):
    _stage_weight(w_ref, wb_ref)
    tm, tn = o_ref.shape
    rc = tm // EPILOGUE_ROW_CHUNKS
    r_idx = lax.broadcasted_iota(jnp.int32, (LANES, LANES), 0) // HEAD_DIM
    c_idx = lax.broadcasted_iota(jnp.int32, (LANES, LANES), 1) // HEAD_DIM
    group_mean = jnp.where(r_idx == c_idx, 1.0 / HEAD_DIM, 0.0).astype(bf16)
    lane = lax.broadcasted_iota(jnp.int32, (rc, LANES), 1)
    first_half = (lane & (HEAD_DIM // 2)) == 0
    w = wb_ref[...]
    for r in range(EPILOGUE_ROW_CHUNKS):
        rows = slice(r * rc, (r + 1) * rc)
        acc = jnp.dot(a_ref[rows, :], w, preferred_element_type=f32)
        cos = cos_ref[rows, :]
        sin = sin_ref[rows, :]
        for c in range(tn // LANES):
            sl = slice(c * LANES, (c + 1) * LANES)
            x = acc[:, sl]
            ms = jnp.dot((x * x).astype(bf16), group_mean, preferred_element_type=f32)
            y = x * lax.rsqrt(ms + EPS) * qw_ref[:, sl]
            partner = jnp.where(first_half,
                                pltpu.roll(y, LANES - HEAD_DIM // 2, 1),
                                pltpu.roll(y, HEAD_DIM // 2, 1))
            o_ref[rows, sl] = (y * cos + partner * sin).astype(o_ref.dtype)


def _proj_qk(a, w, w_layer, qk_w, cos_t, sin_t, tm, tn):
    t, k = a.shape
    n = qk_w.shape[1]
    return pl.pallas_call(
        _proj_qk_kernel,
        grid=(n // tn, t // tm),
        in_specs=[pl.BlockSpec((tm, k), lambda j, i: (i, 0)),
                  pl.BlockSpec((1, k, tn), lambda j, i: (w_layer, 0, j)),
                  pl.BlockSpec((1, tn), lambda j, i: (0, j)),
                  pl.BlockSpec((tm, LANES), lambda j, i: (i, 0)),
                  pl.BlockSpec((tm, LANES), lambda j, i: (i, 0))],
        out_specs=pl.BlockSpec((tm, tn), lambda j, i: (i, j)),
        out_shape=jax.ShapeDtypeStruct((t, n), bf16),
        scratch_shapes=[pltpu.VMEM((k, tn), bf16)],
        compiler_params=_params("arbitrary", "arbitrary"),
        name="proj_qk",
    )(a, w, qk_w, cos_t, sin_t)


def _proj_conv_kernel(a_ref, w_ref, cw_ref, cb_ref, o_ref, wb_ref, halo_ref, *, tiles_per_seq):
    i = pl.program_id(1)
    _stage_weight(w_ref, wb_ref)

    tm, tn = o_ref.shape
    rc = tm // EPILOGUE_ROW_CHUNKS
    w = wb_ref[...]
    first = (i % tiles_per_seq) == 0
    prev = jnp.where(first, 0.0, halo_ref[...])
    sub = lax.broadcasted_iota(jnp.int32, (rc // CONV_HALO, CONV_HALO, tn), 1)
    for r in range(EPILOGUE_ROW_CHUNKS):
        acc = jnp.dot(a_ref[r * rc:(r + 1) * rc, :], w, preferred_element_type=f32)
        slabs = jnp.concatenate([prev, acc], axis=0).reshape(rc // CONV_HALO + 1, CONV_HALO, tn)
        y = cb_ref[...] + cw_ref[SSM_CONV - 1:SSM_CONV, :] * acc
        for tap in range(SSM_CONV - 1):
            back = SSM_CONV - 1 - tap
            rot = pltpu.roll(slabs, back, 1)
            shifted = jnp.where(sub < back, rot[:-1], rot[1:]).reshape(rc, tn)
            y = y + cw_ref[tap:tap + 1, :] * shifted
        o_ref[r * rc:(r + 1) * rc, :] = (y / (1.0 + jnp.exp(-y))).astype(o_ref.dtype)
        prev = acc[rc - CONV_HALO:, :]
    halo_ref[...] = prev


def _proj_conv(a, w, w_layer, col0, conv_w, conv_b, seq, tm, tn):
    t, k = a.shape
    n = conv_w.shape[1]
    assert col0 % tn == 0
    cb0 = col0 // tn
    tiles_per_seq = seq // tm
    kern = functools.partial(_proj_conv_kernel, tiles_per_seq=tiles_per_seq)
    return pl.pallas_call(
        kern,
        grid=(n // tn, t // tm),
        in_specs=[pl.BlockSpec((tm, k), lambda j, i: (i, 0)),
                  pl.BlockSpec((1, k, tn), lambda j, i: (w_layer, 0, cb0 + j)),
                  pl.BlockSpec((SSM_CONV, tn), lambda j, i: (0, j)),
                  pl.BlockSpec((1, tn), lambda j, i: (0, j))],
        out_specs=pl.BlockSpec((tm, tn), lambda j, i: (i, j)),
        out_shape=jax.ShapeDtypeStruct((t, n), bf16),
        scratch_shapes=[pltpu.VMEM((k, tn), bf16),
                        pltpu.VMEM((CONV_HALO, tn), f32)],
        compiler_params=_params("arbitrary", "arbitrary"),
        name="proj_conv",
    )(a, w, conv_w, conv_b)


def _outproj_kernel(a_ref, w_ref, x_ref, g_ref, o_ref, wb_ref):
    _stage_weight(w_ref, wb_ref)
    acc = jnp.dot(a_ref[...], wb_ref[...], preferred_element_type=f32)
    o_ref[...] = x_ref[...] + g_ref[0] * acc


def _out_proj(a, w, w_layer, x2, gate, seq, tm, tn):
    t, k = a.shape
    n = w.shape[2]
    tiles_per_seq = seq // tm
    return pl.pallas_call(
        _outproj_kernel,
        grid=(n // tn, t // tm),
        in_specs=[pl.BlockSpec((tm, k), lambda j, i: (i, 0)),
                  pl.BlockSpec((1, k, tn), lambda j, i: (w_layer, 0, j)),
                  pl.BlockSpec((tm, tn), lambda j, i: (i, j)),
                  pl.BlockSpec((1, 1, tn), lambda j, i: (i // tiles_per_seq, 0, j))],
        out_specs=pl.BlockSpec((tm, tn), lambda j, i: (i, j)),
        out_shape=jax.ShapeDtypeStruct((t, n), f32),
        scratch_shapes=[pltpu.VMEM((k, tn), bf16)],
        compiler_params=_params("arbitrary", "arbitrary"),
        name="out_proj",
    )(a, w, x2, gate)


def _rope_table_kernel(pos_ref, freq_ref, cos_ref, sin_ref):
    ang = pos_ref[...].astype(f32) * freq_ref[...]
    lane = lax.broadcasted_iota(jnp.int32, ang.shape, 1)
    first_half = (lane & (HEAD_DIM // 2)) == 0
    s = jnp.sin(ang)
    cos_ref[...] = jnp.cos(ang)
    sin_ref[...] = jnp.where(first_half, -s, s)


def _rope_tables(pos_col, freq_row, tm=1024):
    t = pos_col.shape[0]
    tm = min(tm, t)
    return pl.pallas_call(
        _rope_table_kernel,
        grid=(t // tm,),
        in_specs=[pl.BlockSpec((tm, 1), lambda i: (i, 0)),
                  pl.BlockSpec((1, LANES), lambda i: (0, 0))],
        out_specs=[pl.BlockSpec((tm, LANES), lambda i: (i, 0)),
                   pl.BlockSpec((tm, LANES), lambda i: (i, 0))],
        out_shape=[jax.ShapeDtypeStruct((t, LANES), f32)] * 2,
        compiler_params=_params("arbitrary"),
        name="rope_table",
    )(pos_col, freq_row)


def _attn_kernel(q_ref, k_ref, v_ref, g_ref, lq1_ref, lk1_ref, lq2_ref, lk2_ref, sw_ref, bias_ref, o_ref,
                 vt_ref, qq_ref, s_ref, acc_ref, *, bq, bk, hp, lambda_init):
    seq = q_ref.shape[0]
    nslot = bq // bk
    lam = (jnp.exp(jnp.sum(lq1_ref[...] * lk1_ref[...], axis=-1, keepdims=True))
           - jnp.exp(jnp.sum(lq2_ref[...] * lk2_ref[...], axis=-1, keepdims=True))
           + lambda_init)
    cols = [slice(hh * LANES, (hh + 1) * LANES) for hh in range(hp)]

    ones_rows = jnp.where(lax.broadcasted_iota(jnp.int32, (BF16_ROWS, bk), 0) == 0, 1.0, 0.0).astype(bf16)

    def transpose_v(c, _):
        start = pl.multiple_of(c * bk, bk)
        for hh in range(hp):
            vt_ref[hh, c, 0:LANES, :] = v_ref[pl.ds(start, bk), cols[hh]].astype(f32).T.astype(bf16)
            vt_ref[hh, c, LANES:, :] = ones_rows
        return 0
    lax.fori_loop(0, seq // bk, transpose_v, 0)

    d_row = lax.broadcasted_iota(jnp.int32, (LANES, bq), 0)

    def scores(hh, j):
        start = pl.multiple_of(j * bk, bk)
        return jnp.dot(k_ref[pl.ds(start, bk), cols[hh]], qq_ref[hh], preferred_element_type=f32)

    def consume(hh, slot, j, m, masked):
        st = s_ref[hh, slot]
        if masked:
            st = st + bias_ref[slot]
        m_new = jnp.maximum(m, jnp.max(st, axis=0, keepdims=True))
        alpha = jnp.exp2(m - m_new)
        p = jnp.exp2(st - m_new)
        acc_ref[hh] = alpha * acc_ref[hh] + jnp.dot(vt_ref[hh, j], p.astype(bf16),
                                                    preferred_element_type=f32)
        return m_new

    def q_block(i, _):
        q_start = pl.multiple_of(i * bq, bq)
        for hh in range(hp):
            qt = q_ref[pl.ds(q_start, bq), cols[hh]].astype(f32).T
            qq_ref[hh] = jnp.concatenate([jnp.where(d_row < HEAD_DIM, qt, 0.0),
                                          jnp.where(d_row >= HEAD_DIM, qt, 0.0)], axis=1).astype(bf16)
            acc_ref[hh] = jnp.zeros(acc_ref.shape[1:], f32)
            for slot in range(nslot):
                s_ref[hh, slot] = scores(hh, slot)

        def body(jj, carry):
            out = []
            for hh in range(hp):
                m = carry[hh]
                for slot in range(nslot):
                    m = consume(hh, slot, jj * nslot + slot, m, False)
                    s_ref[hh, slot] = scores(hh, (jj + 1) * nslot + slot)
                out.append(m)
            return tuple(out)

        init = tuple(jnp.full((1, 2 * bq), NEG_BIG, f32) for _ in range(hp))
        carry = lax.fori_loop(0, i, body, init)

        for hh in range(hp):
            m = carry[hh]
            for slot in range(nslot):
                m = consume(hh, slot, i * nslot + slot, m, True)
            acc = acc_ref[hh]
            ot = acc[:LANES] / acc[LANES:LANES + 1]
            odt = ot[:, :bq] - lam * ot[:, bq:]
            ms = jnp.mean(odt * odt, axis=0, keepdims=True)
            y = (odt * lax.rsqrt(ms + EPS)).T * (sw_ref[...] * (1.0 - lambda_init))
            g = g_ref[pl.ds(q_start, bq), cols[hh]].astype(f32)
            o_ref[pl.ds(q_start, bq), cols[hh]] = (y * _silu(g)).astype(o_ref.dtype)
        return 0

    lax.fori_loop(0, seq // bq, q_block, 0)


def _diff_attn(qk, vg, lq1, lk1, lq2, lk2, subln_w, *, batch, seq, heads, lambda_init,
               bq=512, bk=256, hp=4):
    t = qk.shape[0]
    bq = min(bq, seq)
    bk = min(bk, bq)
    hp = math.gcd(hp, heads)
    nslot = bq // bk
    w = hp * LANES
    ng = heads // hp
    key = np.arange(bq)[:, None]
    qry = np.arange(bq)[None, :]
    tri = np.where(key <= qry, 0.0, NEG_BIG).astype(np.float32)
    bias = jnp.asarray(np.concatenate([tri, tri], axis=1).reshape(nslot, bk, 2 * bq))
    vec = lambda a: a.reshape(1, -1).astype(f32)
    small = pl.BlockSpec((1, HEAD_DIM), lambda b, h: (0, 0))
    single = pl.Buffered(1)
    kern = functools.partial(_attn_kernel, bq=bq, bk=bk, hp=hp, lambda_init=lambda_init)
    return pl.pallas_call(
        kern,
        grid=(batch, ng),
        in_specs=[pl.BlockSpec((seq, w), lambda b, h: (b, h), pipeline_mode=single),
                  pl.BlockSpec((seq, w), lambda b, h: (b, ng + h)),
                  pl.BlockSpec((seq, w), lambda b, h: (b, h)),
                  pl.BlockSpec((seq, w), lambda b, h: (b, ng + h), pipeline_mode=single),
                  small, small, small, small,
                  pl.BlockSpec((1, LANES), lambda b, h: (0, 0)),
                  pl.BlockSpec((nslot, bk, 2 * bq), lambda b, h: (0, 0, 0), pipeline_mode=single)],
        out_specs=pl.BlockSpec((seq, w), lambda b, h: (b, h)),
        out_shape=jax.ShapeDtypeStruct((t, heads * LANES), bf16),
        scratch_shapes=[pltpu.VMEM((hp, seq // bk, LANES + BF16_ROWS, bk), bf16),
                        pltpu.VMEM((hp, LANES, 2 * bq), bf16),
                        pltpu.VMEM((hp, nslot, bk, 2 * bq), f32),
                        pltpu.VMEM((hp, LANES + BF16_ROWS, 2 * bq), f32)],
        compiler_params=_params("arbitrary", "arbitrary"),
        name="diff_attn",
    )(qk, qk, vg, vg, vec(lq1), vec(lk1), vec(lq2), vec(lk2), vec(subln_w), bias)


def _ssd_constants(rheads):
    L, P = SSM_CHUNK, SSM_HEAD_DIM
    k = np.arange(LANES)[:, None]
    col = np.arange(rheads * LANES)[None, :]
    seg = (k < 3 * rheads) & (k % rheads == col // LANES)
    col2 = np.arange(2 * rheads * P)[None, :]
    half = rheads * P
    exp = np.where(col2 < half,
                   (k < 2 * rheads) & (k % rheads == col2 // P),
                   (k >= 2 * rheads) & (k < 4 * rheads) & (k % rheads == (col2 - half) // P))
    upper = (np.arange(L)[:, None] <= np.arange(L)[None, :])
    return jnp.asarray(seg, dtype=bf16), jnp.asarray(exp, dtype=bf16), jnp.asarray(upper, dtype=f32)


def _split_bf16(a, pieces):
    out, rem = [], a
    for _ in range(pieces):
        p = rem.astype(bf16).astype(f32)
        out.append(p)
        rem = rem - p
    return out


def _ssd_kernel(dt_ref, dtb_ref, alog_ref, z_ref, xs_ref, b_ref, c_ref, dskip_ref, nw_ref,
                segoh_ref, expoh_ref, upper_ref,
                o_ref,
                state_ref, cs_ref, dtc_ref, cst_ref, dtt_ref,
                *, groups, rheads, gp):
    c = pl.program_id(1)
    pg = pl.program_id(2)

    @pl.when(pg == 0)
    def _():
        raw = dt_ref[...] + dtb_ref[...]
        dt = jnp.maximum(raw, 0.0) + jnp.log(1.0 + jnp.exp(-jnp.abs(raw)))
        da_t = (dt * (-jnp.exp(alog_ref[...]))).T
        cs_t = jnp.dot(da_t, upper_ref[...], precision=lax.Precision.HIGHEST,
                       preferred_element_type=f32)
        cst_ref[...] = cs_t
        dtt_ref[...] = dt.T
        cs = cs_t.T
        for gg in range(groups):
            shift = (LANES - gg * rheads) % LANES
            cs_ref[gg] = pltpu.roll(cs, shift, 1) if shift else cs
            dtc_ref[gg] = pltpu.roll(dt, shift, 1) if shift else dt

    @pl.when(c == 0)
    def _():
        state_ref[:, pg] = jnp.zeros((gp,) + state_ref.shape[2:], f32)

    for gi in range(gp):
        _ssd_group(gi, pg, z_ref, xs_ref, b_ref, c_ref, dskip_ref, nw_ref, segoh_ref, expoh_ref,
                   o_ref, state_ref, cs_ref, dtc_ref, cst_ref, dtt_ref, rheads, gp)


def _ssd_group(gi, pg, z_ref, xs_ref, b_ref, c_ref, dskip_ref, nw_ref, segoh_ref, expoh_ref, o_ref,
               state_ref, cs_ref, dtc_ref, cst_ref, dtt_ref, rheads, gp):
    L = SSM_CHUNK
    R = rheads
    xw = rheads * SSM_HEAD_DIM
    n = SSM_STATE
    g = pg * gp + gi
    xsl = slice(gi * xw, (gi + 1) * xw)
    nsl = slice(gi * n, (gi + 1) * n)

    xs = xs_ref[:, xsl].astype(f32)
    bm16 = b_ref[:, nsl]
    cm16 = c_ref[:, nsl]

    row0 = pl.multiple_of(g * rheads, rheads)
    cs_t = cst_ref[pl.ds(row0, rheads), :]
    dt_t = dtt_ref[pl.ds(row0, rheads), :]
    cs_c = cs_ref[g]
    dt_c = dtc_ref[g]
    lane = lax.broadcasted_iota(jnp.int32, (L, LANES), 1)

    hi, mid, lo = _split_bf16(cs_c, 3)
    packed = jnp.where(lane < R, hi, jnp.where(lane < 2 * R, pltpu.roll(mid, R, 1), pltpu.roll(lo, 2 * R, 1)))
    seg_col = jnp.dot(packed.astype(bf16), segoh_ref[...], preferred_element_type=f32)

    cb = lax.dot_general(cm16, bm16, (((1,), (1,)), ((), ())),
                         preferred_element_type=f32)
    l_idx = lax.broadcasted_iota(jnp.int32, (L, L), 0)
    s_idx = lax.broadcasted_iota(jnp.int32, (L, L), 1)
    causal = l_idx >= s_idx
    lo_half = lane < SSM_HEAD_DIM

    y_parts = []
    for j in range(rheads // 2):
        xj = xs[:, j * LANES:(j + 1) * LANES]
        x_lo = jnp.where(lo_half, xj, 0.0).astype(bf16)
        x_hi = jnp.where(lo_half, 0.0, xj).astype(bf16)
        yj = None
        for r, xr in ((2 * j, x_lo), (2 * j + 1, x_hi)):
            seg = seg_col[:, r * LANES:(r + 1) * LANES] - cs_t[r:r + 1, :]
            decay = jnp.exp(jnp.where(causal, seg, -jnp.inf))
            mr = (cb * decay * dt_t[r:r + 1, :]).astype(bf16)
            part = jnp.dot(mr, xr, preferred_element_type=f32)
            yj = part if yj is None else yj + part
        y_parts.append(yj)
    y_diag = jnp.concatenate(y_parts, axis=1)

    cs_last = cs_c[L - 1:L, :]
    e_hi, e_lo = _split_bf16(jnp.exp(cs_c), 2)
    w_hi, w_lo = _split_bf16(jnp.exp(cs_last - cs_c) * dt_c, 2)
    packed2 = jnp.where(lane < R, e_hi,
                        jnp.where(lane < 2 * R, pltpu.roll(e_lo, R, 1),
                                  jnp.where(lane < 3 * R, pltpu.roll(w_hi, 2 * R, 1),
                                            pltpu.roll(w_lo, 3 * R, 1))))
    spread = jnp.dot(packed2.astype(bf16), expoh_ref[...], preferred_element_type=f32)
    e_cs = spread[:, :xw]
    wgt = spread[:, xw:]

    prev = state_ref[gi, pg]
    y_off = jnp.dot(cm16, prev.astype(bf16), preferred_element_type=f32) * e_cs
    new = jnp.dot(bm16.astype(f32).T.astype(bf16), (xs * wgt).astype(bf16), preferred_element_type=f32)
    state_ref[gi, pg] = prev * e_cs[L - 1:L, :] + new

    y = y_diag + y_off + xs * dskip_ref[:, xsl]
    y = y * z_ref[:, xsl].astype(f32)
    ms = jnp.mean(y * y, axis=-1, keepdims=True)
    o_ref[:, xsl] = (y * lax.rsqrt(ms + EPS) * nw_ref[:, xsl]).astype(o_ref.dtype)


def _ssd(zproj, xbc, dt_raw, dt_bias, a_log, d_skip, norm_w, *, batch, seq, d_inner, groups, gp=4):
    t = zproj.shape[0]
    L = SSM_CHUNK
    nc = seq // L
    heads = d_inner // SSM_HEAD_DIM
    rheads = heads // groups
    assert rheads % 2 == 0 and 4 * rheads <= LANES
    gp = math.gcd(gp, groups)
    xw = rheads * SSM_HEAD_DIM
    n = SSM_STATE
    bw, bn = gp * xw, gp * n
    assert d_inner % bw == 0 and d_inner % bn == 0
    b_blk0 = d_inner // bn
    c_blk0 = b_blk0 + groups // gp
    row = lambda b, c, g: b * nc + c
    pad = lambda a: jnp.pad(a.reshape(1, -1).astype(f32), ((0, 0), (0, LANES - heads)))
    const2 = lambda shape: pl.BlockSpec(shape, lambda b, c, g: (0, 0))
    seg_oh, exp_oh, upper = _ssd_constants(rheads)
    kern = functools.partial(_ssd_kernel, groups=groups, rheads=rheads, gp=gp)
    return pl.pallas_call(
        kern,
        grid=(batch, nc, groups // gp),
        in_specs=[pl.BlockSpec((L, LANES), lambda b, c, g: (row(b, c, g), 0)),
                  const2((1, LANES)),
                  const2((1, LANES)),
                  pl.BlockSpec((L, bw), lambda b, c, g: (row(b, c, g), g)),
                  pl.BlockSpec((L, bw), lambda b, c, g: (row(b, c, g), g)),
                  pl.BlockSpec((L, bn), lambda b, c, g: (row(b, c, g), b_blk0 + g)),
                  pl.BlockSpec((L, bn), lambda b, c, g: (row(b, c, g), c_blk0 + g)),
                  pl.BlockSpec((1, bw), lambda b, c, g: (0, g)),
                  pl.BlockSpec((1, bw), lambda b, c, g: (0, g)),
                  const2((LANES, rheads * LANES)),
                  const2((LANES, 2 * xw)),
                  const2((L, L))],
        out_specs=pl.BlockSpec((L, bw), lambda b, c, g: (row(b, c, g), g)),
        out_shape=jax.ShapeDtypeStruct((t, d_inner), bf16),
        scratch_shapes=[pltpu.VMEM((gp, groups // gp, n, xw), f32),
                        pltpu.VMEM((groups, L, LANES), f32),
                        pltpu.VMEM((groups, L, LANES), f32),
                        pltpu.VMEM((LANES, L), f32),
                        pltpu.VMEM((LANES, L), f32)],
        compiler_params=_params("arbitrary", "arbitrary", "arbitrary"),
        name="ssd_chunk",
    )(dt_raw, pad(dt_bias), pad(a_log), zproj, xbc, xbc, xbc,
      jnp.repeat(d_skip, SSM_HEAD_DIM).reshape(1, -1), norm_w.reshape(1, -1),
      seg_oh, exp_oh, upper)


def kernel(x, c, positions, norm_w, ada_w, ada_b, attn_w_in, attn_q_norm, attn_k_norm, attn_lambda_q1, attn_lambda_k1, attn_lambda_q2, attn_lambda_k2, attn_subln_w, attn_w_out, ssm_w_in, ssm_conv_w, ssm_conv_b, ssm_dt_bias, ssm_A_log, ssm_D, ssm_norm_w, ssm_w_out):
    batch, seq, d = x.shape
    depth = norm_w.shape[0]
    t = batch * seq
    heads = d // (2 * HEAD_DIM)
    qk_width = heads * 2 * HEAD_DIM
    d_inner = ssm_w_out.shape[1]
    conv_dim = ssm_conv_w.shape[2]
    groups = (conv_dim - d_inner) // (2 * SSM_STATE)
    ssm_heads = ssm_dt_bias.shape[1]
    tm = min(1024, seq)

    c_pad = jnp.pad(c, ((0, 8 - batch), (0, 0)))
    mod = _ada_mod(c_pad, ada_w, ada_b)[:, :batch]
    shift, scale, gate = (mod[:, :, i * d:(i + 1) * d].reshape(depth, batch, 1, d) for i in range(3))

    inv_freq = ROPE_THETA ** (-jnp.arange(0, HEAD_DIM, 2, dtype=f32) / HEAD_DIM)
    freq_row = jnp.tile(inv_freq, LANES // (HEAD_DIM // 2)).reshape(1, LANES)
    cos_t, sin_t = _rope_tables(positions.reshape(t, 1), freq_row)

    main = d_inner + conv_dim
    w_ssm_main = ssm_w_in[:, :, :main].astype(bf16)
    w_ssm_dt = jnp.pad(ssm_w_in[:, :, main:], ((0, 0), (0, 0), (0, LANES - ssm_heads)))

    x2 = x.reshape(t, d)
    for layer in range(depth):
        j = layer // 2
        h = _norm_mod(x2, norm_w[layer].reshape(1, d), scale[layer], shift[layer], seq)
        if layer % 2 == 0:
            lambda_init = 0.8 - 0.6 * math.exp(-0.3 * layer)
            rep = qk_width // HEAD_DIM
            qk_w = jnp.concatenate([jnp.tile(attn_q_norm[j], rep) * (HEAD_DIM ** -0.5 * math.log2(math.e)),
                                    jnp.tile(attn_k_norm[j], rep)]).reshape(1, 2 * qk_width)
            tn_a = _tile(qk_width, 1024)
            qk = _proj_qk(h, attn_w_in, j, qk_w, cos_t, sin_t, tm, tn_a)
            vg = _proj(h, attn_w_in, j, attn_w_in.shape[2] - 2 * qk_width, bf16, tm, tn_a,
                       col0=2 * qk_width)
            o = _diff_attn(qk, vg, attn_lambda_q1[j], attn_lambda_k1[j], attn_lambda_q2[j],
                           attn_lambda_k2[j], attn_subln_w[j], batch=batch, seq=seq, heads=heads,
                           lambda_init=lambda_init)
            x2 = _out_proj(o, attn_w_out, j, x2, gate[layer], seq, tm, _tile(d, 1024))
        else:
            tn_c = _tile(math.gcd(d_inner, conv_dim), 1024)
            zact = _proj(h, w_ssm_main, j, d_inner, bf16, tm, tn_c, silu=True)
            xbc = _proj_conv(h, w_ssm_main, j, d_inner, ssm_conv_w[j], ssm_conv_b[j].reshape(1, -1), seq, tm, tn_c)
            dt_raw = _proj(h, w_ssm_dt, j, LANES, f32, tm, LANES)
            y = _ssd(zact, xbc, dt_raw, ssm_dt_bias[j], ssm_A_log[j], ssm_D[j], ssm_norm_w[j],
                     batch=batch, seq=seq, d_inner=d_inner, groups=groups)
            x2 = _out_proj(y, ssm_w_out, j, x2, gate[layer], seq, tm, _tile(d, 512))
    return x2.reshape(batch, seq, d)
```

```python
import functools
import math

import numpy as np
import jax
import jax.numpy as jnp
from jax import lax
from jax.experimental import pallas as pl
from jax.experimental.pallas import tpu as pltpu

EPS = 1e-6
ROPE_THETA = 10000.0
LANES = 128
BF16_ROWS = 16
CONV_HALO = 8
EPILOGUE_ROW_CHUNKS = 4
HEAD_DIM = 64
SSM_HEAD_DIM = 64
SSM_STATE = 128
SSM_CHUNK = 128
SSM_CONV = 4
VMEM_LIMIT_BYTES = 56 * 1024 * 1024
NEG_BIG = -1e30

f32 = jnp.float32
bf16 = jnp.bfloat16


def _params(*sem):
    return pltpu.CompilerParams(dimension_semantics=sem, vmem_limit_bytes=VMEM_LIMIT_BYTES)


def _silu(x):
    return x * jax.nn.sigmoid(x)


def _tile(n, target):
    if n <= target:
        return n
    best = None
    for cand in range(LANES, target + 1, LANES):
        if n % cand == 0:
            best = cand
    assert best is not None, (n, target)
    return best


def _ada_kernel(c_ref, w_ref, b_ref, o_ref):
    cond = _silu(c_ref[...])
    o_ref[0] = jnp.dot(cond, w_ref[0], preferred_element_type=f32) + b_ref[0]


def _ada_mod(c_pad, ada_w, ada_b, tn=1024):
    depth, d, n = ada_w.shape
    rows = c_pad.shape[0]
    tn = _tile(n, tn)
    return pl.pallas_call(
        _ada_kernel,
        grid=(depth, n // tn),
        in_specs=[pl.BlockSpec((rows, d), lambda l, j: (0, 0)),
                  pl.BlockSpec((1, d, tn), lambda l, j: (l, 0, j)),
                  pl.BlockSpec((1, 1, tn), lambda l, j: (l, 0, j))],
        out_specs=pl.BlockSpec((1, rows, tn), lambda l, j: (l, 0, j)),
        out_shape=jax.ShapeDtypeStruct((depth, rows, n), f32),
        compiler_params=_params("arbitrary", "arbitrary"),
        name="ada_mod",
    )(c_pad, ada_w, ada_b.reshape(depth, 1, n))


def _norm_kernel(x_ref, nw_ref, sc_ref, sh_ref, o_ref):
    x = x_ref[...]
    ms = jnp.mean(x * x, axis=-1, keepdims=True)
    y = x * lax.rsqrt(ms + EPS) * nw_ref[...]
    o_ref[...] = (y * (1.0 + sc_ref[0]) + sh_ref[0]).astype(o_ref.dtype)


def _norm_mod(x2, nw, scale, shift, seq, tm=512):
    t, d = x2.shape
    tm = min(tm, seq)
    tiles_per_seq = seq // tm
    return pl.pallas_call(
        _norm_kernel,
        grid=(t // tm,),
        in_specs=[pl.BlockSpec((tm, d), lambda i: (i, 0)),
                  pl.BlockSpec((1, d), lambda i: (0, 0)),
                  pl.BlockSpec((1, 1, d), lambda i: (i // tiles_per_seq, 0, 0)),
                  pl.BlockSpec((1, 1, d), lambda i: (i // tiles_per_seq, 0, 0))],
        out_specs=pl.BlockSpec((tm, d), lambda i: (i, 0)),
        out_shape=jax.ShapeDtypeStruct((t, d), bf16),
        compiler_params=_params("arbitrary"),
        name="norm_mod",
    )(x2, nw, scale, shift)


def _stage_weight(w_ref, wb_ref):
    @pl.when(pl.program_id(1) == 0)
    def _():
        wb_ref[...] = w_ref[0].astype(bf16)


def _proj_kernel(a_ref, w_ref, o_ref, wb_ref, *, silu):
    _stage_weight(w_ref, wb_ref)
    acc = jnp.dot(a_ref[...], wb_ref[...], preferred_element_type=f32)
    if silu:
        acc = acc / (1.0 + jnp.exp(-acc))
    o_ref[...] = acc.astype(o_ref.dtype)


def _proj(a, w, w_layer, n_out, out_dtype, tm, tn, silu=False, col0=0):
    t, k = a.shape
    assert col0 % tn == 0
    cb0 = col0 // tn
    kern = functools.partial(_proj_kernel, silu=silu)
    return pl.pallas_call(
        kern,
        grid=(n_out // tn, t // tm),
        in_specs=[pl.BlockSpec((tm, k), lambda j, i: (i, 0)),
                  pl.BlockSpec((1, k, tn), lambda j, i: (w_layer, 0, cb0 + j))],
        out_specs=pl.BlockSpec((tm, tn), lambda j, i: (i, j)),
        out_shape=jax.ShapeDtypeStruct((t, n_out), out_dtype),
        scratch_shapes=[pltpu.VMEM((k, tn), bf16)],
        compiler_params=_params("arbitrary", "arbitrary"),
        name="proj",
    )(a, w)


def _proj_qk_kernel(a_ref, w_ref, qw_ref, cos_ref, sin_ref, o_ref, wb_ref):
    _stage_weight(w_ref, wb_ref)
    tm, tn = o_ref.shape
    rc = tm // EPILOGUE_ROW_CHUNKS
    r_idx = lax.broadcasted_iota(jnp.int32, (LANES, LANES), 0) // HEAD_DIM
    c_idx = lax.broadcasted_iota(jnp.int32, (LANES, LANES), 1) // HEAD_DIM
    group_mean = jnp.where(r_idx == c_idx, 1.0 / HEAD_DIM, 0.0).astype(bf16)
    lane = lax.broadcasted_iota(jnp.int32, (rc, LANES), 1)
    first_half = (lane & (HEAD_DIM // 2)) == 0
    w = wb_ref[...]
    for r in range(EPILOGUE_ROW_CHUNKS):
        rows = slice(r * rc, (r + 1) * rc)
        acc = jnp.dot(a_ref[rows, :], w, preferred_element_type=f32)
        cos = cos_ref[rows, :]
        sin = sin_ref[rows, :]
        for c in range(tn // LANES):
            sl = slice(c * LANES, (c + 1) * LANES)
            x = acc[:, sl]
            ms = jnp.dot((x * x).astype(bf16), group_mean, preferred_element_type=f32)
            y = x * lax.rsqrt(ms + EPS) * qw_ref[:, sl]
            partner = jnp.where(first_half,
                                pltpu.roll(y, LANES - HEAD_DIM // 2, 1),
                                pltpu.roll(y, HEAD_DIM // 2, 1))
            o_ref[rows, sl] = (y * cos + partner * sin).astype(o_ref.dtype)


def _proj_qk(a, w, w_layer, qk_w, cos_t, sin_t, tm, tn):
    t, k = a.shape
    n = qk_w.shape[1]
    return pl.pallas_call(
        _proj_qk_kernel,
        grid=(n // tn, t // tm),
        in_specs=[pl.BlockSpec((tm, k), lambda j, i: (i, 0)),
                  pl.BlockSpec((1, k, tn), lambda j, i: (w_layer, 0, j)),
                  pl.BlockSpec((1, tn), lambda j, i: (0, j)),
                  pl.BlockSpec((tm, LANES), lambda j, i: (i, 0)),
                  pl.BlockSpec((tm, LANES), lambda j, i: (i, 0))],
        out_specs=pl.BlockSpec((tm, tn), lambda j, i: (i, j)),
        out_shape=jax.ShapeDtypeStruct((t, n), bf16),
        scratch_shapes=[pltpu.VMEM((k, tn), bf16)],
        compiler_params=_params("arbitrary", "arbitrary"),
        name="proj_qk",
    )(a, w, qk_w, cos_t, sin_t)


def _proj_conv_kernel(a_ref, w_ref, cw_ref, cb_ref, o_ref, wb_ref, halo_ref, *, tiles_per_seq):
    i = pl.program_id(1)
    _stage_weight(w_ref, wb_ref)

    tm, tn = o_ref.shape
    rc = tm // EPILOGUE_ROW_CHUNKS
    w = wb_ref[...]
    first = (i % tiles_per_seq) == 0
    prev = jnp.where(first, 0.0, halo_ref[...])
    sub = lax.broadcasted_iota(jnp.int32, (rc // CONV_HALO, CONV_HALO, tn), 1)
    for r in range(EPILOGUE_ROW_CHUNKS):
        acc = jnp.dot(a_ref[r * rc:(r + 1) * rc, :], w, preferred_element_type=f32)
        slabs = jnp.concatenate([prev, acc], axis=0).reshape(rc // CONV_HALO + 1, CONV_HALO, tn)
        y = cb_ref[...] + cw_ref[SSM_CONV - 1:SSM_CONV, :] * acc
        for tap in range(SSM_CONV - 1):
            back = SSM_CONV - 1 - tap
            rot = pltpu.roll(slabs, back, 1)
            shifted = jnp.where(sub < back, rot[:-1], rot[1:]).reshape(rc, tn)
            y = y + cw_ref[tap:tap + 1, :] * shifted
        o_ref[r * rc:(r + 1) * rc, :] = (y / (1.0 + jnp.exp(-y))).astype(o_ref.dtype)
        prev = acc[rc - CONV_HALO:, :]
    halo_ref[...] = prev


def _proj_conv(a, w, w_layer, col0, conv_w, conv_b, seq, tm, tn):
    t, k = a.shape
    n = conv_w.shape[1]
    assert col0 % tn == 0
    cb0 = col0 // tn
    tiles_per_seq = seq // tm
    kern = functools.partial(_proj_conv_kernel, tiles_per_seq=tiles_per_seq)
    return pl.pallas_call(
        kern,
        grid=(n // tn, t // tm),
        in_specs=[pl.BlockSpec((tm, k), lambda j, i: (i, 0)),
                  pl.BlockSpec((1, k, tn), lambda j, i: (w_layer, 0, cb0 + j)),
                  pl.BlockSpec((SSM_CONV, tn), lambda j, i: (0, j)),
                  pl.BlockSpec((1, tn), lambda j, i: (0, j))],
        out_specs=pl.BlockSpec((tm, tn), lambda j, i: (i, j)),
        out_shape=jax.ShapeDtypeStruct((t, n), bf16),
        scratch_shapes=[pltpu.VMEM((k, tn), bf16),
                        pltpu.VMEM((CONV_HALO, tn), f32)],
        compiler_params=_params("arbitrary", "arbitrary"),
        name="proj_conv",
    )(a, w, conv_w, conv_b)


def _outproj_kernel(a_ref, w_ref, x_ref, g_ref, o_ref, wb_ref):
    _stage_weight(w_ref, wb_ref)
    acc = jnp.dot(a_ref[...], wb_ref[...], preferred_element_type=f32)
    o_ref[...] = x_ref[...] + g_ref[0] * acc


def _out_proj(a, w, w_layer, x2, gate, seq, tm, tn):
    t, k = a.shape
    n = w.shape[2]
    tiles_per_seq = seq // tm
    return pl.pallas_call(
        _outproj_kernel,
        grid=(n // tn, t // tm),
        in_specs=[pl.BlockSpec((tm, k), lambda j, i: (i, 0)),
                  pl.BlockSpec((1, k, tn), lambda j, i: (w_layer, 0, j)),
                  pl.BlockSpec((tm, tn), lambda j, i: (i, j)),
                  pl.BlockSpec((1, 1, tn), lambda j, i: (i // tiles_per_seq, 0, j))],
        out_specs=pl.BlockSpec((tm, tn), lambda j, i: (i, j)),
        out_shape=jax.ShapeDtypeStruct((t, n), f32),
        scratch_shapes=[pltpu.VMEM((k, tn), bf16)],
        compiler_params=_params("arbitrary", "arbitrary"),
        name="out_proj",
    )(a, w, x2, gate)


def _rope_table_kernel(pos_ref, freq_ref, cos_ref, sin_ref):
    ang = pos_ref[...].astype(f32) * freq_ref[...]
    lane = lax.broadcasted_iota(jnp.int32, ang.shape, 1)
    first_half = (lane & (HEAD_DIM // 2)) == 0
    s = jnp.sin(ang)
    cos_ref[...] = jnp.cos(ang)
    sin_ref[...] = jnp.where(first_half, -s, s)


def _rope_tables(pos_col, freq_row, tm=1024):
    t = pos_col.shape[0]
    tm = min(tm, t)
    return pl.pallas_call(
        _rope_table_kernel,
        grid=(t // tm,),
        in_specs=[pl.BlockSpec((tm, 1), lambda i: (i, 0)),
                  pl.BlockSpec((1, LANES), lambda i: (0, 0))],
        out_specs=[pl.BlockSpec((tm, LANES), lambda i: (i, 0)),
                   pl.BlockSpec((tm, LANES), lambda i: (i, 0))],
        out_shape=[jax.ShapeDtypeStruct((t, LANES), f32)] * 2,
        compiler_params=_params("arbitrary"),
        name="rope_table",
    )(pos_col, freq_row)


def _attn_kernel(q_ref, k_ref, v_ref, g_ref, lq1_ref, lk1_ref, lq2_ref, lk2_ref, sw_ref, bias_ref, o_ref,
                 vt_ref, qq_ref, s_ref, mx_ref, acc_ref, *, bq, bk, hp, lambda_init):
    seq = q_ref.shape[0]
    nslot = bq // bk
    lam = (jnp.exp(jnp.sum(lq1_ref[...] * lk1_ref[...], axis=-1, keepdims=True))
           - jnp.exp(jnp.sum(lq2_ref[...] * lk2_ref[...], axis=-1, keepdims=True))
           + lambda_init)
    cols = [slice(hh * LANES, (hh + 1) * LANES) for hh in range(hp)]

    ones_rows = jnp.where(lax.broadcasted_iota(jnp.int32, (BF16_ROWS, bk), 0) == 0, 1.0, 0.0).astype(bf16)

    def transpose_v(c, _):
        start = pl.multiple_of(c * bk, bk)
        for hh in range(hp):
            vt_ref[hh, c, 0:LANES, :] = v_ref[pl.ds(start, bk), cols[hh]].astype(f32).T.astype(bf16)
            vt_ref[hh, c, LANES:, :] = ones_rows
        return 0
    lax.fori_loop(0, seq // bk, transpose_v, 0)

    d_row = lax.broadcasted_iota(jnp.int32, (LANES, bq), 0)

    def produce(hh, slot, j):
        start = pl.multiple_of(j * bk, bk)
        st = jnp.dot(k_ref[pl.ds(start, bk), cols[hh]], qq_ref[hh], preferred_element_type=f32)
        s_ref[hh, slot] = st
        mx_ref[hh, slot] = jnp.max(st, axis=0, keepdims=True)

    def consume(hh, slot, j, m, masked):
        st = s_ref[hh, slot]
        if masked:
            st = st + bias_ref[slot]
            blk_max = jnp.max(st, axis=0, keepdims=True)
        else:
            blk_max = mx_ref[hh, slot]
        m_new = jnp.maximum(m, blk_max)
        alpha = jnp.exp2(m - m_new)
        p = jnp.exp2(st - m_new)
        acc_ref[hh] = alpha * acc_ref[hh] + jnp.dot(vt_ref[hh, j], p.astype(bf16),
                                                    preferred_element_type=f32)
        return m_new

    def q_block(i, _):
        q_start = pl.multiple_of(i * bq, bq)
        for hh in range(hp):
            qt = q_ref[pl.ds(q_start, bq), cols[hh]].astype(f32).T
            qq_ref[hh] = jnp.concatenate([jnp.where(d_row < HEAD_DIM, qt, 0.0),
                                          jnp.where(d_row >= HEAD_DIM, qt, 0.0)], axis=1).astype(bf16)
            acc_ref[hh] = jnp.zeros(acc_ref.shape[1:], f32)
            for slot in range(nslot):
                produce(hh, slot, slot)

        def body(jj, carry):
            out = []
            for hh in range(hp):
                m = carry[hh]
                for slot in range(nslot):
                    m = consume(hh, slot, jj * nslot + slot, m, False)
                    produce(hh, slot, (jj + 1) * nslot + slot)
                out.append(m)
            return tuple(out)

        init = tuple(jnp.full((1, 2 * bq), NEG_BIG, f32) for _ in range(hp))
        carry = lax.fori_loop(0, i, body, init)

        for hh in range(hp):
            m = carry[hh]
            for slot in range(nslot):
                m = consume(hh, slot, i * nslot + slot, m, True)
            acc = acc_ref[hh]
            ot = acc[:LANES] / acc[LANES:LANES + 1]
            odt = ot[:, :bq] - lam * ot[:, bq:]
            ms = jnp.mean(odt * odt, axis=0, keepdims=True)
            y = (odt * lax.rsqrt(ms + EPS)).T * (sw_ref[...] * (1.0 - lambda_init))
            g = g_ref[pl.ds(q_start, bq), cols[hh]].astype(f32)
            o_ref[pl.ds(q_start, bq), cols[hh]] = (y * _silu(g)).astype(o_ref.dtype)
        return 0

    lax.fori_loop(0, seq // bq, q_block, 0)


def _diff_attn(qk, vg, lq1, lk1, lq2, lk2, subln_w, *, batch, seq, heads, lambda_init,
               bq=512, bk=512, hp=4):
    t = qk.shape[0]
    bq = min(bq, seq)
    bk = min(bk, bq)
    hp = math.gcd(hp, heads)
    nslot = bq // bk
    w = hp * LANES
    ng = heads // hp
    key = np.arange(bq)[:, None]
    qry = np.arange(bq)[None, :]
    tri = np.where(key <= qry, 0.0, NEG_BIG).astype(np.float32)
    bias = jnp.asarray(np.concatenate([tri, tri], axis=1).reshape(nslot, bk, 2 * bq))
    vec = lambda a: a.reshape(1, -1).astype(f32)
    small = pl.BlockSpec((1, HEAD_DIM), lambda b, h: (0, 0))
    single = pl.Buffered(1)
    kern = functools.partial(_attn_kernel, bq=bq, bk=bk, hp=hp, lambda_init=lambda_init)
    return pl.pallas_call(
        kern,
        grid=(batch, ng),
        in_specs=[pl.BlockSpec((seq, w), lambda b, h: (b, h), pipeline_mode=single),
                  pl.BlockSpec((seq, w), lambda b, h: (b, ng + h)),
                  pl.BlockSpec((seq, w), lambda b, h: (b, h)),
                  pl.BlockSpec((seq, w), lambda b, h: (b, ng + h), pipeline_mode=single),
                  small, small, small, small,
                  pl.BlockSpec((1, LANES), lambda b, h: (0, 0)),
                  pl.BlockSpec((nslot, bk, 2 * bq), lambda b, h: (0, 0, 0), pipeline_mode=single)],
        out_specs=pl.BlockSpec((seq, w), lambda b, h: (b, h)),
        out_shape=jax.ShapeDtypeStruct((t, heads * LANES), bf16),
        scratch_shapes=[pltpu.VMEM((hp, seq // bk, LANES + BF16_ROWS, bk), bf16),
                        pltpu.VMEM((hp, LANES, 2 * bq), bf16),
                        pltpu.VMEM((hp, nslot, bk, 2 * bq), f32),
                        pltpu.VMEM((hp, nslot, 1, 2 * bq), f32),
                        pltpu.VMEM((hp, LANES + BF16_ROWS, 2 * bq), f32)],
        compiler_params=_params("arbitrary", "arbitrary"),
        name="diff_attn",
    )(qk, qk, vg, vg, vec(lq1), vec(lk1), vec(lq2), vec(lk2), vec(subln_w), bias)


def _ssd_constants(rheads):
    L, P = SSM_CHUNK, SSM_HEAD_DIM
    k = np.arange(LANES)[:, None]
    col = np.arange(rheads * LANES)[None, :]
    seg = (k < 3 * rheads) & (k % rheads == col // LANES)
    col2 = np.arange(2 * rheads * P)[None, :]
    half = rheads * P
    exp = np.where(col2 < half,
                   (k < 2 * rheads) & (k % rheads == col2 // P),
                   (k >= 2 * rheads) & (k < 4 * rheads) & (k % rheads == (col2 - half) // P))
    upper = (np.arange(L)[:, None] <= np.arange(L)[None, :])
    return jnp.asarray(seg, dtype=bf16), jnp.asarray(exp, dtype=bf16), jnp.asarray(upper, dtype=f32)


def _split_bf16(a, pieces):
    out, rem = [], a
    for _ in range(pieces):
        p = rem.astype(bf16).astype(f32)
        out.append(p)
        rem = rem - p
    return out


def _ssd_kernel(dt_ref, dtb_ref, alog_ref, z_ref, xs_ref, b_ref, c_ref, dskip_ref, nw_ref,
                segoh_ref, expoh_ref, upper_ref,
                o_ref,
                state_ref, cs_ref, dtc_ref, cst_ref, dtt_ref,
                *, groups, rheads, gp):
    c = pl.program_id(1)
    pg = pl.program_id(2)

    @pl.when(pg == 0)
    def _():
        raw = dt_ref[...] + dtb_ref[...]
        dt = jnp.maximum(raw, 0.0) + jnp.log(1.0 + jnp.exp(-jnp.abs(raw)))
        da_t = (dt * (-jnp.exp(alog_ref[...]))).T
        cs_t = jnp.dot(da_t, upper_ref[...], precision=lax.Precision.HIGHEST,
                       preferred_element_type=f32)
        cst_ref[...] = cs_t
        dtt_ref[...] = dt.T
        cs = cs_t.T
        for gg in range(groups):
            shift = (LANES - gg * rheads) % LANES
            cs_ref[gg] = pltpu.roll(cs, shift, 1) if shift else cs
            dtc_ref[gg] = pltpu.roll(dt, shift, 1) if shift else dt

    @pl.when(c == 0)
    def _():
        state_ref[:, pg] = jnp.zeros((gp,) + state_ref.shape[2:], f32)

    for gi in range(gp):
        _ssd_group(gi, pg, z_ref, xs_ref, b_ref, c_ref, dskip_ref, nw_ref, segoh_ref, expoh_ref,
                   o_ref, state_ref, cs_ref, dtc_ref, cst_ref, dtt_ref, rheads, gp)


def _ssd_group(gi, pg, z_ref, xs_ref, b_ref, c_ref, dskip_ref, nw_ref, segoh_ref, expoh_ref, o_ref,
               state_ref, cs_ref, dtc_ref, cst_ref, dtt_ref, rheads, gp):
    L = SSM_CHUNK
    R = rheads
    xw = rheads * SSM_HEAD_DIM
    n = SSM_STATE
    g = pg * gp + gi
    xsl = slice(gi * xw, (gi + 1) * xw)
    nsl = slice(gi * n, (gi + 1) * n)

    xs = xs_ref[:, xsl].astype(f32)
    bm16 = b_ref[:, nsl]
    cm16 = c_ref[:, nsl]

    row0 = pl.multiple_of(g * rheads, rheads)
    cs_t = cst_ref[pl.ds(row0, rheads), :]
    dt_t = dtt_ref[pl.ds(row0, rheads), :]
    cs_c = cs_ref[g]
    dt_c = dtc_ref[g]
    lane = lax.broadcasted_iota(jnp.int32, (L, LANES), 1)

    hi, mid, lo = _split_bf16(cs_c, 3)
    packed = jnp.where(lane < R, hi, jnp.where(lane < 2 * R, pltpu.roll(mid, R, 1), pltpu.roll(lo, 2 * R, 1)))
    seg_col = jnp.dot(packed.astype(bf16), segoh_ref[...], preferred_element_type=f32)

    cb = lax.dot_general(cm16, bm16, (((1,), (1,)), ((), ())),
                         preferred_element_type=f32)
    l_idx = lax.broadcasted_iota(jnp.int32, (L, L), 0)
    s_idx = lax.broadcasted_iota(jnp.int32, (L, L), 1)
    causal = l_idx >= s_idx
    lo_half = lane < SSM_HEAD_DIM

    y_parts = []
    for j in range(rheads // 2):
        xj = xs[:, j * LANES:(j + 1) * LANES]
        x_lo = jnp.where(lo_half, xj, 0.0).astype(bf16)
        x_hi = jnp.where(lo_half, 0.0, xj).astype(bf16)
        yj = None
        for r, xr in ((2 * j, x_lo), (2 * j + 1, x_hi)):
            seg = seg_col[:, r * LANES:(r + 1) * LANES] - cs_t[r:r + 1, :]
            decay = jnp.exp(jnp.where(causal, seg, -jnp.inf))
            mr = (cb * decay * dt_t[r:r + 1, :]).astype(bf16)
            part = jnp.dot(mr, xr, preferred_element_type=f32)
            yj = part if yj is None else yj + part
        y_parts.append(yj)
    y_diag = jnp.concatenate(y_parts, axis=1)

    cs_last = cs_c[L - 1:L, :]
    e_hi, e_lo = _split_bf16(jnp.exp(cs_c), 2)
    w_hi, w_lo = _split_bf16(jnp.exp(cs_last - cs_c) * dt_c, 2)
    packed2 = jnp.where(lane < R, e_hi,
                        jnp.where(lane < 2 * R, pltpu.roll(e_lo, R, 1),
                                  jnp.where(lane < 3 * R, pltpu.roll(w_hi, 2 * R, 1),
                                            pltpu.roll(w_lo, 3 * R, 1))))
    spread = jnp.dot(packed2.astype(bf16), expoh_ref[...], preferred_element_type=f32)
    e_cs = spread[:, :xw]
    wgt = spread[:, xw:]

    prev = state_ref[gi, pg]
    y_off = jnp.dot(cm16, prev.astype(bf16), preferred_element_type=f32) * e_cs
    new = jnp.dot(bm16.astype(f32).T.astype(bf16), (xs * wgt).astype(bf16), preferred_element_type=f32)
    state_ref[gi, pg] = prev * e_cs[L - 1:L, :] + new

    y = y_diag + y_off + xs * dskip_ref[:, xsl]
    y = y * z_ref[:, xsl].astype(f32)
    ms = jnp.mean(y * y, axis=-1, keepdims=True)
    o_ref[:, xsl] = (y * lax.rsqrt(ms + EPS) * nw_ref[:, xsl]).astype(o_ref.dtype)


def _ssd(zproj, xbc, dt_raw, dt_bias, a_log, d_skip, norm_w, *, batch, seq, d_inner, groups, gp=4):
    t = zproj.shape[0]
    L = SSM_CHUNK
    nc = seq // L
    heads = d_inner // SSM_HEAD_DIM
    rheads = heads // groups
    assert rheads % 2 == 0 and 4 * rheads <= LANES
    gp = math.gcd(gp, groups)
    xw = rheads * SSM_HEAD_DIM
    n = SSM_STATE
    bw, bn = gp * xw, gp * n
    assert d_inner % bw == 0 and d_inner % bn == 0
    b_blk0 = d_inner // bn
    c_blk0 = b_blk0 + groups // gp
    row = lambda b, c, g: b * nc + c
    pad = lambda a: jnp.pad(a.reshape(1, -1).astype(f32), ((0, 0), (0, LANES - heads)))
    const2 = lambda shape: pl.BlockSpec(shape, lambda b, c, g: (0, 0))
    seg_oh, exp_oh, upper = _ssd_constants(rheads)
    kern = functools.partial(_ssd_kernel, groups=groups, rheads=rheads, gp=gp)
    return pl.pallas_call(
        kern,
        grid=(batch, nc, groups // gp),
        in_specs=[pl.BlockSpec((L, LANES), lambda b, c, g: (row(b, c, g), 0)),
                  const2((1, LANES)),
                  const2((1, LANES)),
                  pl.BlockSpec((L, bw), lambda b, c, g: (row(b, c, g), g)),
                  pl.BlockSpec((L, bw), lambda b, c, g: (row(b, c, g), g)),
                  pl.BlockSpec((L, bn), lambda b, c, g: (row(b, c, g), b_blk0 + g)),
                  pl.BlockSpec((L, bn), lambda b, c, g: (row(b, c, g), c_blk0 + g)),
                  pl.BlockSpec((1, bw), lambda b, c, g: (0, g)),
                  pl.BlockSpec((1, bw), lambda b, c, g: (0, g)),
                  const2((LANES, rheads * LANES)),
                  const2((LANES, 2 * xw)),
                  const2((L, L))],
        out_specs=pl.BlockSpec((L, bw), lambda b, c, g: (row(b, c, g), g)),
        out_shape=jax.ShapeDtypeStruct((t, d_inner), bf16),
        scratch_shapes=[pltpu.VMEM((gp, groups // gp, n, xw), f32),
                        pltpu.VMEM((groups, L, LANES), f32),
                        pltpu.VMEM((groups, L, LANES), f32),
                        pltpu.VMEM((LANES, L), f32),
                        pltpu.VMEM((LANES, L), f32)],
        compiler_params=_params("arbitrary", "arbitrary", "arbitrary"),
        name="ssd_chunk",
    )(dt_raw, pad(dt_bias), pad(a_log), zproj, xbc, xbc, xbc,
      jnp.repeat(d_skip, SSM_HEAD_DIM).reshape(1, -1), norm_w.reshape(1, -1),
      seg_oh, exp_oh, upper)


def kernel(x, c, positions, norm_w, ada_w, ada_b, attn_w_in, attn_q_norm, attn_k_norm, attn_lambda_q1, attn_lambda_k1, attn_lambda_q2, attn_lambda_k2, attn_subln_w, attn_w_out, ssm_w_in, ssm_conv_w, ssm_conv_b, ssm_dt_bias, ssm_A_log, ssm_D, ssm_norm_w, ssm_w_out):
    batch, seq, d = x.shape
    depth = norm_w.shape[0]
    t = batch * seq
    heads = d // (2 * HEAD_DIM)
    qk_width = heads * 2 * HEAD_DIM
    d_inner = ssm_w_out.shape[1]
    conv_dim = ssm_conv_w.shape[2]
    groups = (conv_dim - d_inner) // (2 * SSM_STATE)
    ssm_heads = ssm_dt_bias.shape[1]
    tm = min(1024, seq)

    c_pad = jnp.pad(c, ((0, 8 - batch), (0, 0)))
    mod = _ada_mod(c_pad, ada_w, ada_b)[:, :batch]
    shift, scale, gate = (mod[:, :, i * d:(i + 1) * d].reshape(depth, batch, 1, d) for i in range(3))

    inv_freq = ROPE_THETA ** (-jnp.arange(0, HEAD_DIM, 2, dtype=f32) / HEAD_DIM)
    freq_row = jnp.tile(inv_freq, LANES // (HEAD_DIM // 2)).reshape(1, LANES)
    cos_t, sin_t = _rope_tables(positions.reshape(t, 1), freq_row)

    main = d_inner + conv_dim
    assert main % LANES == 0 and ssm_heads <= LANES
    w_ssm = jnp.pad(ssm_w_in, ((0, 0), (0, 0), (0, LANES - ssm_heads))).astype(bf16)

    x2 = x.reshape(t, d)
    for layer in range(depth):
        j = layer // 2
        h = _norm_mod(x2, norm_w[layer].reshape(1, d), scale[layer], shift[layer], seq)
        if layer % 2 == 0:
            lambda_init = 0.8 - 0.6 * math.exp(-0.3 * layer)
            rep = qk_width // HEAD_DIM
            qk_w = jnp.concatenate([jnp.tile(attn_q_norm[j], rep) * (HEAD_DIM ** -0.5 * math.log2(math.e)),
                                    jnp.tile(attn_k_norm[j], rep)]).reshape(1, 2 * qk_width)
            tn_a = _tile(qk_width, 1024)
            qk = _proj_qk(h, attn_w_in, j, qk_w, cos_t, sin_t, tm, tn_a)
            vg = _proj(h, attn_w_in, j, attn_w_in.shape[2] - 2 * qk_width, bf16, tm, tn_a,
                       col0=2 * qk_width)
            o = _diff_attn(qk, vg, attn_lambda_q1[j], attn_lambda_k1[j], attn_lambda_q2[j],
                           attn_lambda_k2[j], attn_subln_w[j], batch=batch, seq=seq, heads=heads,
                           lambda_init=lambda_init)
            x2 = _out_proj(o, attn_w_out, j, x2, gate[layer], seq, tm, _tile(d, 1024))
        else:
            tn_c = _tile(math.gcd(d_inner, conv_dim), 1024)
            zact = _proj(h, w_ssm, j, d_inner, bf16, tm, tn_c, silu=True)
            xbc = _proj_conv(h, w_ssm, j, d_inner, ssm_conv_w[j], ssm_conv_b[j].reshape(1, -1), seq, tm, tn_c)
            dt_raw = _proj(h, w_ssm, j, LANES, f32, tm, LANES, col0=main)
            y = _ssd(zact, xbc, dt_raw, ssm_dt_bias[j], ssm_A_log[j], ssm_D[j], ssm_norm_w[j],
                     batch=batch, seq=seq, d_inner=d_inner, groups=groups)
            x2 = _out_proj(y, ssm_w_out, j, x2, gate[layer], seq, tm, _tile(d, 512))
    return x2.reshape(batch, seq, d)
```

```python
import functools
import math

import numpy as np
import jax
import jax.numpy as jnp
from jax import lax
from jax.experimental import pallas as pl
from jax.experimental.pallas import tpu as pltpu

EPS = 1e-6
ROPE_THETA = 10000.0
LANES = 128
BF16_ROWS = 16
CONV_HALO = 8
QK_ROW_CHUNKS = 4
CONV_ROW_CHUNKS = 8
HEAD_DIM = 64
SSM_HEAD_DIM = 64
SSM_STATE = 128
SSM_CHUNK = 128
SSM_CONV = 4
VMEM_LIMIT_BYTES = 56 * 1024 * 1024
NEG_BIG = -1e30

f32 = jnp.float32
bf16 = jnp.bfloat16


def _params(*sem):
    return pltpu.CompilerParams(dimension_semantics=sem, vmem_limit_bytes=VMEM_LIMIT_BYTES)


def _silu(x):
    return x * jax.nn.sigmoid(x)


def _tile(n, target):
    if n <= target:
        return n
    best = None
    for cand in range(LANES, target + 1, LANES):
        if n % cand == 0:
            best = cand
    assert best is not None, (n, target)
    return best


def _ada_kernel(c_ref, w_ref, b_ref, o_ref):
    cond = _silu(c_ref[...])
    o_ref[0] = jnp.dot(cond, w_ref[0], preferred_element_type=f32) + b_ref[0]


def _ada_mod(c_pad, ada_w, ada_b, tn=1024):
    depth, d, n = ada_w.shape
    rows = c_pad.shape[0]
    tn = _tile(n, tn)
    return pl.pallas_call(
        _ada_kernel,
        grid=(depth, n // tn),
        in_specs=[pl.BlockSpec((rows, d), lambda l, j: (0, 0)),
                  pl.BlockSpec((1, d, tn), lambda l, j: (l, 0, j)),
                  pl.BlockSpec((1, 1, tn), lambda l, j: (l, 0, j))],
        out_specs=pl.BlockSpec((1, rows, tn), lambda l, j: (l, 0, j)),
        out_shape=jax.ShapeDtypeStruct((depth, rows, n), f32),
        compiler_params=_params("arbitrary", "arbitrary"),
        name="ada_mod",
    )(c_pad, ada_w, ada_b.reshape(depth, 1, n))


def _norm_kernel(x_ref, nw_ref, sc_ref, sh_ref, o_ref):
    x = x_ref[...]
    ms = jnp.mean(x * x, axis=-1, keepdims=True)
    y = x * lax.rsqrt(ms + EPS) * nw_ref[...]
    o_ref[...] = (y * (1.0 + sc_ref[0]) + sh_ref[0]).astype(o_ref.dtype)


def _norm_mod(x2, nw, scale, shift, seq, tm=512):
    t, d = x2.shape
    tm = min(tm, seq)
    tiles_per_seq = seq // tm
    return pl.pallas_call(
        _norm_kernel,
        grid=(t // tm,),
        in_specs=[pl.BlockSpec((tm, d), lambda i: (i, 0)),
                  pl.BlockSpec((1, d), lambda i: (0, 0)),
                  pl.BlockSpec((1, 1, d), lambda i: (i // tiles_per_seq, 0, 0)),
                  pl.BlockSpec((1, 1, d), lambda i: (i // tiles_per_seq, 0, 0))],
        out_specs=pl.BlockSpec((tm, d), lambda i: (i, 0)),
        out_shape=jax.ShapeDtypeStruct((t, d), bf16),
        compiler_params=_params("arbitrary"),
        name="norm_mod",
    )(x2, nw, scale, shift)


def _stage_weight(w_ref, wb_ref):
    @pl.when(pl.program_id(1) == 0)
    def _():
        wb_ref[...] = w_ref[0].astype(bf16)


def _proj_kernel(a_ref, w_ref, o_ref, wb_ref, *, silu):
    _stage_weight(w_ref, wb_ref)
    acc = jnp.dot(a_ref[...], wb_ref[...], preferred_element_type=f32)
    if silu:
        acc = acc / (1.0 + jnp.exp(-acc))
    o_ref[...] = acc.astype(o_ref.dtype)


def _proj(a, w, w_layer, n_out, out_dtype, tm, tn, silu=False, col0=0):
    t, k = a.shape
    assert col0 % tn == 0
    cb0 = col0 // tn
    kern = functools.partial(_proj_kernel, silu=silu)
    return pl.pallas_call(
        kern,
        grid=(n_out // tn, t // tm),
        in_specs=[pl.BlockSpec((tm, k), lambda j, i: (i, 0)),
                  pl.BlockSpec((1, k, tn), lambda j, i: (w_layer, 0, cb0 + j))],
        out_specs=pl.BlockSpec((tm, tn), lambda j, i: (i, j)),
        out_shape=jax.ShapeDtypeStruct((t, n_out), out_dtype),
        scratch_shapes=[pltpu.VMEM((k, tn), bf16)],
        compiler_params=_params("arbitrary", "arbitrary"),
        name="proj",
    )(a, w)


def _proj_qk_kernel(a_ref, w_ref, qw_ref, cos_ref, sin_ref, o_ref, wb_ref):
    _stage_weight(w_ref, wb_ref)
    tm, tn = o_ref.shape
    r_idx = lax.broadcasted_iota(jnp.int32, (LANES, LANES), 0) // HEAD_DIM
    c_idx = lax.broadcasted_iota(jnp.int32, (LANES, LANES), 1) // HEAD_DIM
    group_mean = jnp.where(r_idx == c_idx, 1.0 / HEAD_DIM, 0.0).astype(bf16)
    rc = tm // QK_ROW_CHUNKS
    lane = lax.broadcasted_iota(jnp.int32, (rc, LANES), 1)
    first_half = (lane & (HEAD_DIM // 2)) == 0
    w = wb_ref[...]
    for r in range(QK_ROW_CHUNKS):
        rows = slice(r * rc, (r + 1) * rc)
        acc = jnp.dot(a_ref[rows, :], w, preferred_element_type=f32)
        cos = cos_ref[rows, :]
        sin = sin_ref[rows, :]
        for c in range(tn // LANES):
            sl = slice(c * LANES, (c + 1) * LANES)
            x = acc[:, sl]
            ms = jnp.dot((x * x).astype(bf16), group_mean, preferred_element_type=f32)
            y = x * lax.rsqrt(ms + EPS) * qw_ref[:, sl]
            partner = jnp.where(first_half,
                                pltpu.roll(y, LANES - HEAD_DIM // 2, 1),
                                pltpu.roll(y, HEAD_DIM // 2, 1))
            o_ref[rows, sl] = (y * cos + partner * sin).astype(o_ref.dtype)


def _proj_qk(a, w, w_layer, qk_w, cos_t, sin_t, tm, tn):
    t, k = a.shape
    n = qk_w.shape[1]
    return pl.pallas_call(
        _proj_qk_kernel,
        grid=(n // tn, t // tm),
        in_specs=[pl.BlockSpec((tm, k), lambda j, i: (i, 0)),
                  pl.BlockSpec((1, k, tn), lambda j, i: (w_layer, 0, j)),
                  pl.BlockSpec((1, tn), lambda j, i: (0, j)),
                  pl.BlockSpec((tm, LANES), lambda j, i: (i, 0)),
                  pl.BlockSpec((tm, LANES), lambda j, i: (i, 0))],
        out_specs=pl.BlockSpec((tm, tn), lambda j, i: (i, j)),
        out_shape=jax.ShapeDtypeStruct((t, n), bf16),
        scratch_shapes=[pltpu.VMEM((k, tn), bf16)],
        compiler_params=_params("arbitrary", "arbitrary"),
        name="proj_qk",
    )(a, w, qk_w, cos_t, sin_t)


def _proj_conv_kernel(a_ref, w_ref, cw_ref, cb_ref, o_ref, wb_ref, halo_ref, *, tiles_per_seq):
    i = pl.program_id(1)
    _stage_weight(w_ref, wb_ref)

    tm, tn = o_ref.shape
    rc = tm // CONV_ROW_CHUNKS
    w = wb_ref[...]
    first = (i % tiles_per_seq) == 0
    prev = jnp.where(first, 0.0, halo_ref[...])
    sub = lax.broadcasted_iota(jnp.int32, (rc // CONV_HALO, CONV_HALO, tn), 1)
    for r in range(CONV_ROW_CHUNKS):
        acc = jnp.dot(a_ref[r * rc:(r + 1) * rc, :], w, preferred_element_type=f32)
        slabs = jnp.concatenate([prev, acc], axis=0).reshape(rc // CONV_HALO + 1, CONV_HALO, tn)
        y = cb_ref[...] + cw_ref[SSM_CONV - 1:SSM_CONV, :] * acc
        for tap in range(SSM_CONV - 1):
            back = SSM_CONV - 1 - tap
            rot = pltpu.roll(slabs, back, 1)
            shifted = jnp.where(sub < back, rot[:-1], rot[1:]).reshape(rc, tn)
            y = y + cw_ref[tap:tap + 1, :] * shifted
        o_ref[r * rc:(r + 1) * rc, :] = (y / (1.0 + jnp.exp(-y))).astype(o_ref.dtype)
        prev = acc[rc - CONV_HALO:, :]
    halo_ref[...] = prev


def _proj_conv(a, w, w_layer, col0, conv_w, conv_b, seq, tm, tn):
    t, k = a.shape
    n = conv_w.shape[1]
    assert col0 % tn == 0
    cb0 = col0 // tn
    tiles_per_seq = seq // tm
    kern = functools.partial(_proj_conv_kernel, tiles_per_seq=tiles_per_seq)
    return pl.pallas_call(
        kern,
        grid=(n // tn, t // tm),
        in_specs=[pl.BlockSpec((tm, k), lambda j, i: (i, 0)),
                  pl.BlockSpec((1, k, tn), lambda j, i: (w_layer, 0, cb0 + j)),
                  pl.BlockSpec((SSM_CONV, tn), lambda j, i: (0, j)),
                  pl.BlockSpec((1, tn), lambda j, i: (0, j))],
        out_specs=pl.BlockSpec((tm, tn), lambda j, i: (i, j)),
        out_shape=jax.ShapeDtypeStruct((t, n), bf16),
        scratch_shapes=[pltpu.VMEM((k, tn), bf16),
                        pltpu.VMEM((CONV_HALO, tn), f32)],
        compiler_params=_params("arbitrary", "arbitrary"),
        name="proj_conv",
    )(a, w, conv_w, conv_b)


def _outproj_kernel(a_ref, w_ref, x_ref, g_ref, o_ref, wb_ref):
    _stage_weight(w_ref, wb_ref)
    acc = jnp.dot(a_ref[...], wb_ref[...], preferred_element_type=f32)
    o_ref[...] = x_ref[...] + g_ref[0] * acc


def _out_proj(a, w, w_layer, x2, gate, seq, tm, tn):
    t, k = a.shape
    n = w.shape[2]
    tiles_per_seq = seq // tm
    return pl.pallas_call(
        _outproj_kernel,
        grid=(n // tn, t // tm),
        in_specs=[pl.BlockSpec((tm, k), lambda j, i: (i, 0)),
                  pl.BlockSpec((1, k, tn), lambda j, i: (w_layer, 0, j)),
                  pl.BlockSpec((tm, tn), lambda j, i: (i, j)),
                  pl.BlockSpec((1, 1, tn), lambda j, i: (i // tiles_per_seq, 0, j))],
        out_specs=pl.BlockSpec((tm, tn), lambda j, i: (i, j)),
        out_shape=jax.ShapeDtypeStruct((t, n), f32),
        scratch_shapes=[pltpu.VMEM((k, tn), bf16)],
        compiler_params=_params("arbitrary", "arbitrary"),
        name="out_proj",
    )(a, w, x2, gate)


def _rope_table_kernel(pos_ref, freq_ref, cos_ref, sin_ref):
    ang = pos_ref[...].astype(f32) * freq_ref[...]
    lane = lax.broadcasted_iota(jnp.int32, ang.shape, 1)
    first_half = (lane & (HEAD_DIM // 2)) == 0
    s = jnp.sin(ang)
    cos_ref[...] = jnp.cos(ang)
    sin_ref[...] = jnp.where(first_half, -s, s)


def _rope_tables(pos_col, freq_row, tm=1024):
    t = pos_col.shape[0]
    tm = min(tm, t)
    return pl.pallas_call(
        _rope_table_kernel,
        grid=(t // tm,),
        in_specs=[pl.BlockSpec((tm, 1), lambda i: (i, 0)),
                  pl.BlockSpec((1, LANES), lambda i: (0, 0))],
        out_specs=[pl.BlockSpec((tm, LANES), lambda i: (i, 0)),
                   pl.BlockSpec((tm, LANES), lambda i: (i, 0))],
        out_shape=[jax.ShapeDtypeStruct((t, LANES), f32)] * 2,
        compiler_params=_params("arbitrary"),
        name="rope_table",
    )(pos_col, freq_row)


def _attn_kernel(q_ref, k_ref, v_ref, g_ref, lq1_ref, lk1_ref, lq2_ref, lk2_ref, sw_ref, bias_ref, o_ref,
                 vt_ref, qq_ref, s_ref, mx_ref, acc_ref, *, bq, bk, hp, lambda_init):
    seq = q_ref.shape[0]
    nslot = bq // bk
    lam = (jnp.exp(jnp.sum(lq1_ref[...] * lk1_ref[...], axis=-1, keepdims=True))
           - jnp.exp(jnp.sum(lq2_ref[...] * lk2_ref[...], axis=-1, keepdims=True))
           + lambda_init)
    cols = [slice(hh * LANES, (hh + 1) * LANES) for hh in range(hp)]

    ones_rows = jnp.where(lax.broadcasted_iota(jnp.int32, (BF16_ROWS, bk), 0) == 0, 1.0, 0.0).astype(bf16)

    def transpose_v(c, _):
        start = pl.multiple_of(c * bk, bk)
        for hh in range(hp):
            vt_ref[hh, c, 0:LANES, :] = v_ref[pl.ds(start, bk), cols[hh]].astype(f32).T.astype(bf16)
            vt_ref[hh, c, LANES:, :] = ones_rows
        return 0
    lax.fori_loop(0, seq // bk, transpose_v, 0)

    d_row = lax.broadcasted_iota(jnp.int32, (LANES, bq), 0)

    def produce(hh, slot, j):
        start = pl.multiple_of(j * bk, bk)
        st = jnp.dot(k_ref[pl.ds(start, bk), cols[hh]], qq_ref[hh], preferred_element_type=f32)
        s_ref[hh, slot] = st
        mx_ref[hh, slot] = jnp.max(st, axis=0, keepdims=True)

    def consume(hh, slot, j, m, masked):
        st = s_ref[hh, slot]
        if masked:
            st = st + bias_ref[slot]
            blk_max = jnp.max(st, axis=0, keepdims=True)
        else:
            blk_max = mx_ref[hh, slot]
        m_new = jnp.maximum(m, blk_max)
        alpha = jnp.exp2(m - m_new)
        p = jnp.exp2(st - m_new)
        acc_ref[hh] = alpha * acc_ref[hh] + jnp.dot(vt_ref[hh, j], p.astype(bf16),
                                                    preferred_element_type=f32)
        return m_new

    def q_block(i, _):
        q_start = pl.multiple_of(i * bq, bq)
        for hh in range(hp):
            qt = q_ref[pl.ds(q_start, bq), cols[hh]].astype(f32).T
            qq_ref[hh] = jnp.concatenate([jnp.where(d_row < HEAD_DIM, qt, 0.0),
                                          jnp.where(d_row >= HEAD_DIM, qt, 0.0)], axis=1).astype(bf16)
            acc_ref[hh] = jnp.zeros(acc_ref.shape[1:], f32)
            for slot in range(nslot):
                produce(hh, slot, slot)

        def body(jj, carry):
            out = []
            for hh in range(hp):
                m = carry[hh]
                for slot in range(nslot):
                    m = consume(hh, slot, jj * nslot + slot, m, False)
                    produce(hh, slot, (jj + 1) * nslot + slot)
                out.append(m)
            return tuple(out)

        init = tuple(jnp.full((1, 2 * bq), NEG_BIG, f32) for _ in range(hp))
        carry = lax.fori_loop(0, i, body, init)

        for hh in range(hp):
            m = carry[hh]
            for slot in range(nslot):
                m = consume(hh, slot, i * nslot + slot, m, True)
            acc = acc_ref[hh]
            ot = acc[:LANES] / acc[LANES:LANES + 1]
            odt = ot[:, :bq] - lam * ot[:, bq:]
            ms = jnp.mean(odt * odt, axis=0, keepdims=True)
            y = (odt * lax.rsqrt(ms + EPS)).T * (sw_ref[...] * (1.0 - lambda_init))
            g = g_ref[pl.ds(q_start, bq), cols[hh]].astype(f32)
            o_ref[pl.ds(q_start, bq), cols[hh]] = (y * _silu(g)).astype(o_ref.dtype)
        return 0

    lax.fori_loop(0, seq // bq, q_block, 0)


def _diff_attn(qk, vg, lq1, lk1, lq2, lk2, subln_w, *, batch, seq, heads, lambda_init,
               bq=512, bk=512, hp=4):
    t = qk.shape[0]
    bq = min(bq, seq)
    bk = min(bk, bq)
    hp = math.gcd(hp, heads)
    nslot = bq // bk
    w = hp * LANES
    ng = heads // hp
    key = np.arange(bq)[:, None]
    qry = np.arange(bq)[None, :]
    tri = np.where(key <= qry, 0.0, NEG_BIG).astype(np.float32)
    bias = jnp.asarray(np.concatenate([tri, tri], axis=1).reshape(nslot, bk, 2 * bq))
    vec = lambda a: a.reshape(1, -1).astype(f32)
    small = pl.BlockSpec((1, HEAD_DIM), lambda b, h: (0, 0))
    single = pl.Buffered(1)
    kern = functools.partial(_attn_kernel, bq=bq, bk=bk, hp=hp, lambda_init=lambda_init)
    return pl.pallas_call(
        kern,
        grid=(batch, ng),
        in_specs=[pl.BlockSpec((seq, w), lambda b, h: (b, h), pipeline_mode=single),
                  pl.BlockSpec((seq, w), lambda b, h: (b, ng + h)),
                  pl.BlockSpec((seq, w), lambda b, h: (b, h)),
                  pl.BlockSpec((seq, w), lambda b, h: (b, ng + h), pipeline_mode=single),
                  small, small, small, small,
                  pl.BlockSpec((1, LANES), lambda b, h: (0, 0)),
                  pl.BlockSpec((nslot, bk, 2 * bq), lambda b, h: (0, 0, 0), pipeline_mode=single)],
        out_specs=pl.BlockSpec((seq, w), lambda b, h: (b, h)),
        out_shape=jax.ShapeDtypeStruct((t, heads * LANES), bf16),
        scratch_shapes=[pltpu.VMEM((hp, seq // bk, LANES + BF16_ROWS, bk), bf16),
                        pltpu.VMEM((hp, LANES, 2 * bq), bf16),
                        pltpu.VMEM((hp, nslot, bk, 2 * bq), f32),
                        pltpu.VMEM((hp, nslot, 1, 2 * bq), f32),
                        pltpu.VMEM((hp, LANES + BF16_ROWS, 2 * bq), f32)],
        compiler_params=_params("arbitrary", "arbitrary"),
        name="diff_attn",
    )(qk, qk, vg, vg, vec(lq1), vec(lk1), vec(lq2), vec(lk2), vec(subln_w), bias)


def _ssd_constants(rheads):
    L, P = SSM_CHUNK, SSM_HEAD_DIM
    k = np.arange(LANES)[:, None]
    col = np.arange(rheads * LANES)[None, :]
    seg = (k < 3 * rheads) & (k % rheads == col // LANES)
    col2 = np.arange(2 * rheads * P)[None, :]
    half = rheads * P
    exp = np.where(col2 < half,
                   (k < 2 * rheads) & (k % rheads == col2 // P),
                   (k >= 2 * rheads) & (k < 4 * rheads) & (k % rheads == (col2 - half) // P))
    upper = (np.arange(L)[:, None] <= np.arange(L)[None, :])
    return jnp.asarray(seg, dtype=bf16), jnp.asarray(exp, dtype=bf16), jnp.asarray(upper, dtype=bf16)


def _split_bf16(a, pieces):
    out, rem = [], a
    for _ in range(pieces):
        p = rem.astype(bf16).astype(f32)
        out.append(p)
        rem = rem - p
    return out


def _ssd_kernel(dt_ref, dtb_ref, alog_ref, z_ref, xs_ref, b_ref, c_ref, dskip_ref, nw_ref,
                segoh_ref, expoh_ref, upper_ref,
                o_ref,
                state_ref, cs_ref, dtc_ref, cst_ref, dtt_ref,
                *, groups, rheads, gp):
    c = pl.program_id(1)
    pg = pl.program_id(2)

    @pl.when(c == 0)
    def _():
        state_ref[:, pg] = jnp.zeros((gp,) + state_ref.shape[2:], f32)

    def bookkeeping():
        raw = dt_ref[...] + dtb_ref[...]
        dt = jnp.maximum(raw, 0.0) + jnp.log(1.0 + jnp.exp(-jnp.abs(raw)))
        da_t = (dt * (-jnp.exp(alog_ref[...]))).T
        cs_t = sum(jnp.dot(piece.astype(bf16), upper_ref[...], preferred_element_type=f32)
                   for piece in _split_bf16(da_t, 3))
        cst_ref[...] = cs_t
        dtt_ref[...] = dt.T
        cs = cs_t.T
        for gg in range(groups):
            shift = (LANES - gg * rheads) % LANES
            cs_ref[gg] = pltpu.roll(cs, shift, 1) if shift else cs
            dtc_ref[gg] = pltpu.roll(dt, shift, 1) if shift else dt

    if gp == groups:
        bookkeeping()
    else:
        pl.when(pg == 0)(bookkeeping)

    for gi in range(gp):
        _ssd_group(gi, pg, z_ref, xs_ref, b_ref, c_ref, dskip_ref, nw_ref, segoh_ref, expoh_ref,
                   o_ref, state_ref, cs_ref, dtc_ref, cst_ref, dtt_ref, rheads, gp)


def _ssd_group(gi, pg, z_ref, xs_ref, b_ref, c_ref, dskip_ref, nw_ref, segoh_ref, expoh_ref, o_ref,
               state_ref, cs_ref, dtc_ref, cst_ref, dtt_ref, rheads, gp):
    L = SSM_CHUNK
    R = rheads
    xw = rheads * SSM_HEAD_DIM
    n = SSM_STATE
    g = pg * gp + gi
    xsl = slice(gi * xw, (gi + 1) * xw)
    nsl = slice(gi * n, (gi + 1) * n)

    xs = xs_ref[:, xsl].astype(f32)
    bm16 = b_ref[:, nsl]
    cm16 = c_ref[:, nsl]

    row0 = pl.multiple_of(g * rheads, rheads)
    cs_t = cst_ref[pl.ds(row0, rheads), :]
    dt_t = dtt_ref[pl.ds(row0, rheads), :]
    cs_c = cs_ref[g]
    dt_c = dtc_ref[g]
    lane = lax.broadcasted_iota(jnp.int32, (L, LANES), 1)

    hi, mid, lo = _split_bf16(cs_c, 3)
    packed = jnp.where(lane < R, hi, jnp.where(lane < 2 * R, pltpu.roll(mid, R, 1), pltpu.roll(lo, 2 * R, 1)))
    seg_col = jnp.dot(packed.astype(bf16), segoh_ref[...], preferred_element_type=f32)

    cb = lax.dot_general(cm16, bm16, (((1,), (1,)), ((), ())),
                         preferred_element_type=f32)
    l_idx = lax.broadcasted_iota(jnp.int32, (L, L), 0)
    s_idx = lax.broadcasted_iota(jnp.int32, (L, L), 1)
    causal = l_idx >= s_idx
    lo_half = lane < SSM_HEAD_DIM

    y_parts = []
    for j in range(rheads // 2):
        xj = xs[:, j * LANES:(j + 1) * LANES]
        x_lo = jnp.where(lo_half, xj, 0.0).astype(bf16)
        x_hi = jnp.where(lo_half, 0.0, xj).astype(bf16)
        yj = None
        for r, xr in ((2 * j, x_lo), (2 * j + 1, x_hi)):
            seg = seg_col[:, r * LANES:(r + 1) * LANES] - cs_t[r:r + 1, :]
            decay = jnp.exp(jnp.where(causal, seg, -jnp.inf))
            mr = (cb * decay * dt_t[r:r + 1, :]).astype(bf16)
            part = jnp.dot(mr, xr, preferred_element_type=f32)
            yj = part if yj is None else yj + part
        y_parts.append(yj)
    y_diag = jnp.concatenate(y_parts, axis=1)

    cs_last = cs_c[L - 1:L, :]
    e_hi, e_lo = _split_bf16(jnp.exp(cs_c), 2)
    w_hi, w_lo = _split_bf16(jnp.exp(cs_last - cs_c) * dt_c, 2)
    packed2 = jnp.where(lane < R, e_hi,
                        jnp.where(lane < 2 * R, pltpu.roll(e_lo, R, 1),
                                  jnp.where(lane < 3 * R, pltpu.roll(w_hi, 2 * R, 1),
                                            pltpu.roll(w_lo, 3 * R, 1))))
    spread = jnp.dot(packed2.astype(bf16), expoh_ref[...], preferred_element_type=f32)
    e_cs = spread[:, :xw]
    wgt = spread[:, xw:]

    prev = state_ref[gi, pg]
    y_off = jnp.dot(cm16, prev.astype(bf16), preferred_element_type=f32) * e_cs
    new = jnp.dot(bm16.astype(f32).T.astype(bf16), (xs * wgt).astype(bf16), preferred_element_type=f32)
    state_ref[gi, pg] = prev * e_cs[L - 1:L, :] + new

    y = y_diag + y_off + xs * dskip_ref[:, xsl]
    y = y * z_ref[:, xsl].astype(f32)
    ms = jnp.mean(y * y, axis=-1, keepdims=True)
    o_ref[:, xsl] = (y * lax.rsqrt(ms + EPS) * nw_ref[:, xsl]).astype(o_ref.dtype)


def _ssd(zproj, xbc, dt_raw, dt_bias, a_log, d_skip, norm_w, *, batch, seq, d_inner, groups, gp=8):
    t = zproj.shape[0]
    L = SSM_CHUNK
    nc = seq // L
    heads = d_inner // SSM_HEAD_DIM
    rheads = heads // groups
    assert rheads % 2 == 0 and 4 * rheads <= LANES
    gp = math.gcd(gp, groups)
    xw = rheads * SSM_HEAD_DIM
    n = SSM_STATE
    bw, bn = gp * xw, gp * n
    assert d_inner % bw == 0 and d_inner % bn == 0
    b_blk0 = d_inner // bn
    c_blk0 = b_blk0 + groups // gp
    row = lambda b, c, g: b * nc + c
    pad = lambda a: jnp.pad(a.reshape(1, -1).astype(f32), ((0, 0), (0, LANES - heads)))
    const2 = lambda shape: pl.BlockSpec(shape, lambda b, c, g: (0, 0))
    seg_oh, exp_oh, upper = _ssd_constants(rheads)
    kern = functools.partial(_ssd_kernel, groups=groups, rheads=rheads, gp=gp)
    return pl.pallas_call(
        kern,
        grid=(batch, nc, groups // gp),
        in_specs=[pl.BlockSpec((L, LANES), lambda b, c, g: (row(b, c, g), 0)),
                  const2((1, LANES)),
                  const2((1, LANES)),
                  pl.BlockSpec((L, bw), lambda b, c, g: (row(b, c, g), g)),
                  pl.BlockSpec((L, bw), lambda b, c, g: (row(b, c, g), g)),
                  pl.BlockSpec((L, bn), lambda b, c, g: (row(b, c, g), b_blk0 + g)),
                  pl.BlockSpec((L, bn), lambda b, c, g: (row(b, c, g), c_blk0 + g)),
                  pl.BlockSpec((1, bw), lambda b, c, g: (0, g)),
                  pl.BlockSpec((1, bw), lambda b, c, g: (0, g)),
                  const2((LANES, rheads * LANES)),
                  const2((LANES, 2 * xw)),
                  const2((L, L))],
        out_specs=pl.BlockSpec((L, bw), lambda b, c, g: (row(b, c, g), g)),
        out_shape=jax.ShapeDtypeStruct((t, d_inner), bf16),
        scratch_shapes=[pltpu.VMEM((gp, groups // gp, n, xw), f32),
                        pltpu.VMEM((groups, L, LANES), f32),
                        pltpu.VMEM((groups, L, LANES), f32),
                        pltpu.VMEM((LANES, L), f32),
                        pltpu.VMEM((LANES, L), f32)],
        compiler_params=_params("arbitrary", "arbitrary", "arbitrary"),
        name="ssd_chunk",
    )(dt_raw, pad(dt_bias), pad(a_log), zproj, xbc, xbc, xbc,
      jnp.repeat(d_skip, SSM_HEAD_DIM).reshape(1, -1), norm_w.reshape(1, -1),
      seg_oh, exp_oh, upper)


def kernel(x, c, positions, norm_w, ada_w, ada_b, attn_w_in, attn_q_norm, attn_k_norm, attn_lambda_q1, attn_lambda_k1, attn_lambda_q2, attn_lambda_k2, attn_subln_w, attn_w_out, ssm_w_in, ssm_conv_w, ssm_conv_b, ssm_dt_bias, ssm_A_log, ssm_D, ssm_norm_w, ssm_w_out):
    batch, seq, d = x.shape
    depth = norm_w.shape[0]
    t = batch * seq
    heads = d // (2 * HEAD_DIM)
    qk_width = heads * 2 * HEAD_DIM
    d_inner = ssm_w_out.shape[1]
    conv_dim = ssm_conv_w.shape[2]
    groups = (conv_dim - d_inner) // (2 * SSM_STATE)
    ssm_heads = ssm_dt_bias.shape[1]
    tm = min(1024, seq)

    c_pad = jnp.pad(c, ((0, 8 - batch), (0, 0)))
    mod = _ada_mod(c_pad, ada_w, ada_b)[:, :batch]
    shift, scale, gate = (mod[:, :, i * d:(i + 1) * d].reshape(depth, batch, 1, d) for i in range(3))

    inv_freq = ROPE_THETA ** (-jnp.arange(0, HEAD_DIM, 2, dtype=f32) / HEAD_DIM)
    freq_row = jnp.tile(inv_freq, LANES // (HEAD_DIM // 2)).reshape(1, LANES)
    cos_t, sin_t = _rope_tables(positions.reshape(t, 1), freq_row)

    main = d_inner + conv_dim
    assert main % LANES == 0 and ssm_heads <= LANES
    w_ssm = jnp.pad(ssm_w_in, ((0, 0), (0, 0), (0, LANES - ssm_heads))).astype(bf16)

    x2 = x.reshape(t, d)
    for layer in range(depth):
        j = layer // 2
        h = _norm_mod(x2, norm_w[layer].reshape(1, d), scale[layer], shift[layer], seq)
        if layer % 2 == 0:
            lambda_init = 0.8 - 0.6 * math.exp(-0.3 * layer)
            rep = qk_width // HEAD_DIM
            qk_w = jnp.concatenate([jnp.tile(attn_q_norm[j], rep) * (HEAD_DIM ** -0.5 * math.log2(math.e)),
                                    jnp.tile(attn_k_norm[j], rep)]).reshape(1, 2 * qk_width)
            tn_a = _tile(qk_width, 1024)
            qk = _proj_qk(h, attn_w_in, j, qk_w, cos_t, sin_t, tm, tn_a)
            vg = _proj(h, attn_w_in, j, attn_w_in.shape[2] - 2 * qk_width, bf16, tm, tn_a,
                       col0=2 * qk_width)
            o = _diff_attn(qk, vg, attn_lambda_q1[j], attn_lambda_k1[j], attn_lambda_q2[j],
                           attn_lambda_k2[j], attn_subln_w[j], batch=batch, seq=seq, heads=heads,
                           lambda_init=lambda_init)
            x2 = _out_proj(o, attn_w_out, j, x2, gate[layer], seq, tm, _tile(d, 1024))
        else:
            tn_c = _tile(math.gcd(d_inner, conv_dim), 1024)
            zact = _proj(h, w_ssm, j, d_inner, bf16, tm, tn_c, silu=True)
            xbc = _proj_conv(h, w_ssm, j, d_inner, ssm_conv_w[j], ssm_conv_b[j].reshape(1, -1), seq, tm, tn_c)
            dt_raw = _proj(h, w_ssm, j, LANES, f32, tm, LANES, col0=main)
            y = _ssd(zact, xbc, dt_raw, ssm_dt_bias[j], ssm_A_log[j], ssm_D[j], ssm_norm_w[j],
                     batch=batch, seq=seq, d_inner=d_inner, groups=groups)
            x2 = _out_proj(y, ssm_w_out, j, x2, gate[layer], seq, tm, _tile(d, 512))
    return x2.reshape(batch, seq, d)
```

```python
import functools
import math

import numpy as np
import jax
import jax.numpy as jnp
from jax import lax
from jax.experimental import pallas as pl
from jax.experimental.pallas import tpu as pltpu

EPS = 1e-6
ROPE_THETA = 10000.0
LANES = 128
BF16_ROWS = 16
CONV_HALO = 8
QK_ROW_CHUNKS = 4
CONV_ROW_CHUNKS = 4
HEAD_DIM = 64
SSM_HEAD_DIM = 64
SSM_STATE = 128
SSM_CHUNK = 128
SSM_CONV = 4
VMEM_LIMIT_BYTES = 56 * 1024 * 1024
NEG_BIG = -1e30

f32 = jnp.float32
bf16 = jnp.bfloat16


def _params(*sem):
    return pltpu.CompilerParams(dimension_semantics=sem, vmem_limit_bytes=VMEM_LIMIT_BYTES)


def _silu(x):
    return x * jax.nn.sigmoid(x)


def _tile(n, target):
    if n <= target:
        return n
    best = None
    for cand in range(LANES, target + 1, LANES):
        if n % cand == 0:
            best = cand
    assert best is not None, (n, target)
    return best


def _ada_kernel(c_ref, w_ref, b_ref, o_ref):
    cond = _silu(c_ref[...])
    o_ref[0] = jnp.dot(cond, w_ref[0], preferred_element_type=f32) + b_ref[0]


def _ada_mod(c_pad, ada_w, ada_b, tn=1024):
    depth, d, n = ada_w.shape
    rows = c_pad.shape[0]
    tn = _tile(n, tn)
    return pl.pallas_call(
        _ada_kernel,
        grid=(depth, n // tn),
        in_specs=[pl.BlockSpec((rows, d), lambda l, j: (0, 0)),
                  pl.BlockSpec((1, d, tn), lambda l, j: (l, 0, j)),
                  pl.BlockSpec((1, 1, tn), lambda l, j: (l, 0, j))],
        out_specs=pl.BlockSpec((1, rows, tn), lambda l, j: (l, 0, j)),
        out_shape=jax.ShapeDtypeStruct((depth, rows, n), f32),
        compiler_params=_params("arbitrary", "arbitrary"),
        name="ada_mod",
    )(c_pad, ada_w, ada_b.reshape(depth, 1, n))


def _norm_kernel(x_ref, nw_ref, sc_ref, sh_ref, o_ref):
    x = x_ref[...]
    ms = jnp.mean(x * x, axis=-1, keepdims=True)
    y = x * lax.rsqrt(ms + EPS) * nw_ref[...]
    o_ref[...] = (y * (1.0 + sc_ref[0]) + sh_ref[0]).astype(o_ref.dtype)


def _norm_mod(x2, nw, scale, shift, seq, tm=512):
    t, d = x2.shape
    tm = min(tm, seq)
    tiles_per_seq = seq // tm
    return pl.pallas_call(
        _norm_kernel,
        grid=(t // tm,),
        in_specs=[pl.BlockSpec((tm, d), lambda i: (i, 0)),
                  pl.BlockSpec((1, d), lambda i: (0, 0)),
                  pl.BlockSpec((1, 1, d), lambda i: (i // tiles_per_seq, 0, 0)),
                  pl.BlockSpec((1, 1, d), lambda i: (i // tiles_per_seq, 0, 0))],
        out_specs=pl.BlockSpec((tm, d), lambda i: (i, 0)),
        out_shape=jax.ShapeDtypeStruct((t, d), bf16),
        compiler_params=_params("arbitrary"),
        name="norm_mod",
    )(x2, nw, scale, shift)


def _stage_weight(w_ref, wb_ref):
    @pl.when(pl.program_id(1) == 0)
    def _():
        wb_ref[...] = w_ref[0].astype(bf16)


def _proj_kernel(a_ref, w_ref, o_ref, wb_ref, *, silu):
    _stage_weight(w_ref, wb_ref)
    acc = jnp.dot(a_ref[...], wb_ref[...], preferred_element_type=f32)
    if silu:
        acc = acc / (1.0 + jnp.exp(-acc))
    o_ref[...] = acc.astype(o_ref.dtype)


def _proj(a, w, w_layer, n_out, out_dtype, tm, tn, silu=False, col0=0):
    t, k = a.shape
    assert col0 % tn == 0
    cb0 = col0 // tn
    kern = functools.partial(_proj_kernel, silu=silu)
    return pl.pallas_call(
        kern,
        grid=(n_out // tn, t // tm),
        in_specs=[pl.BlockSpec((tm, k), lambda j, i: (i, 0)),
                  pl.BlockSpec((1, k, tn), lambda j, i: (w_layer, 0, cb0 + j))],
        out_specs=pl.BlockSpec((tm, tn), lambda j, i: (i, j)),
        out_shape=jax.ShapeDtypeStruct((t, n_out), out_dtype),
        scratch_shapes=[pltpu.VMEM((k, tn), bf16)],
        compiler_params=_params("arbitrary", "arbitrary"),
        name="proj",
    )(a, w)


def _proj_qk_kernel(a_ref, w_ref, qw_ref, cos_ref, sin_ref, o_ref, wb_ref):
    _stage_weight(w_ref, wb_ref)
    tm, tn = o_ref.shape
    r_idx = lax.broadcasted_iota(jnp.int32, (LANES, LANES), 0) // HEAD_DIM
    c_idx = lax.broadcasted_iota(jnp.int32, (LANES, LANES), 1) // HEAD_DIM
    group_mean = jnp.where(r_idx == c_idx, 1.0 / HEAD_DIM, 0.0).astype(bf16)
    rc = tm // QK_ROW_CHUNKS
    lane = lax.broadcasted_iota(jnp.int32, (rc, LANES), 1)
    first_half = (lane & (HEAD_DIM // 2)) == 0
    w = wb_ref[...]
    for r in range(QK_ROW_CHUNKS):
        rows = slice(r * rc, (r + 1) * rc)
        acc = jnp.dot(a_ref[rows, :], w, preferred_element_type=f32)
        cos = cos_ref[rows, :]
        sin = sin_ref[rows, :]
        for c in range(tn // LANES):
            sl = slice(c * LANES, (c + 1) * LANES)
            x = acc[:, sl]
            ms = jnp.dot((x * x).astype(bf16), group_mean, preferred_element_type=f32)
            y = x * lax.rsqrt(ms + EPS) * qw_ref[:, sl]
            partner = jnp.where(first_half,
                                pltpu.roll(y, LANES - HEAD_DIM // 2, 1),
                                pltpu.roll(y, HEAD_DIM // 2, 1))
            o_ref[rows, sl] = (y * cos + partner * sin).astype(o_ref.dtype)


def _proj_qk(a, w, w_layer, qk_w, cos_t, sin_t, tm, tn):
    t, k = a.shape
    n = qk_w.shape[1]
    return pl.pallas_call(
        _proj_qk_kernel,
        grid=(n // tn, t // tm),
        in_specs=[pl.BlockSpec((tm, k), lambda j, i: (i, 0)),
                  pl.BlockSpec((1, k, tn), lambda j, i: (w_layer, 0, j)),
                  pl.BlockSpec((1, tn), lambda j, i: (0, j)),
                  pl.BlockSpec((tm, LANES), lambda j, i: (i, 0)),
                  pl.BlockSpec((tm, LANES), lambda j, i: (i, 0))],
        out_specs=pl.BlockSpec((tm, tn), lambda j, i: (i, j)),
        out_shape=jax.ShapeDtypeStruct((t, n), bf16),
        scratch_shapes=[pltpu.VMEM((k, tn), bf16)],
        compiler_params=_params("arbitrary", "arbitrary"),
        name="proj_qk",
    )(a, w, qk_w, cos_t, sin_t)


def _proj_conv_kernel(a_ref, w_ref, cw_ref, cb_ref, o_ref, wb_ref, halo_ref, *, tiles_per_seq):
    i = pl.program_id(1)
    _stage_weight(w_ref, wb_ref)

    tm, tn = o_ref.shape
    rc = tm // CONV_ROW_CHUNKS
    w = wb_ref[...]
    first = (i % tiles_per_seq) == 0
    prev = jnp.where(first, 0.0, halo_ref[...])
    nsl = rc // CONV_HALO
    sub = lax.broadcasted_iota(jnp.int32, (nsl + 1, CONV_HALO, tn), 1)
    w0, w1, w2, w3 = (cw_ref[tap:tap + 1, :] for tap in range(SSM_CONV))

    def shift_rows(slabs, back):
        rot = pltpu.roll(slabs, back, 1)
        above = jnp.concatenate([rot[:1], rot[:-1]], axis=0)
        return jnp.where(sub < back, above, rot)

    for r in range(CONV_ROW_CHUNKS):
        acc = jnp.dot(a_ref[r * rc:(r + 1) * rc, :], w, preferred_element_type=f32)
        slabs = jnp.concatenate([prev, acc], axis=0).reshape(nsl + 1, CONV_HALO, tn)
        x1 = shift_rows(slabs, 1)
        v2 = shift_rows(w1 * slabs + w0 * x1, 2)
        y = (cb_ref[...] + w3 * slabs + w2 * x1 + v2)[1:].reshape(rc, tn)
        o_ref[r * rc:(r + 1) * rc, :] = (y / (1.0 + jnp.exp(-y))).astype(o_ref.dtype)
        prev = acc[rc - CONV_HALO:, :]
    halo_ref[...] = prev


def _proj_conv(a, w, w_layer, col0, conv_w, conv_b, seq, tm, tn):
    t, k = a.shape
    n = conv_w.shape[1]
    assert col0 % tn == 0
    cb0 = col0 // tn
    tiles_per_seq = seq // tm
    kern = functools.partial(_proj_conv_kernel, tiles_per_seq=tiles_per_seq)
    return pl.pallas_call(
        kern,
        grid=(n // tn, t // tm),
        in_specs=[pl.BlockSpec((tm, k), lambda j, i: (i, 0)),
                  pl.BlockSpec((1, k, tn), lambda j, i: (w_layer, 0, cb0 + j)),
                  pl.BlockSpec((SSM_CONV, tn), lambda j, i: (0, j)),
                  pl.BlockSpec((1, tn), lambda j, i: (0, j))],
        out_specs=pl.BlockSpec((tm, tn), lambda j, i: (i, j)),
        out_shape=jax.ShapeDtypeStruct((t, n), bf16),
        scratch_shapes=[pltpu.VMEM((k, tn), bf16),
                        pltpu.VMEM((CONV_HALO, tn), f32)],
        compiler_params=_params("arbitrary", "arbitrary"),
        name="proj_conv",
    )(a, w, conv_w, conv_b)


def _outproj_kernel(a_ref, w_ref, x_ref, g_ref, o_ref, wb_ref):
    _stage_weight(w_ref, wb_ref)
    acc = jnp.dot(a_ref[...], wb_ref[...], preferred_element_type=f32)
    o_ref[...] = x_ref[...] + g_ref[0] * acc


def _out_proj(a, w, w_layer, x2, gate, seq, tm, tn):
    t, k = a.shape
    n = w.shape[2]
    tiles_per_seq = seq // tm
    return pl.pallas_call(
        _outproj_kernel,
        grid=(n // tn, t // tm),
        in_specs=[pl.BlockSpec((tm, k), lambda j, i: (i, 0)),
                  pl.BlockSpec((1, k, tn), lambda j, i: (w_layer, 0, j)),
                  pl.BlockSpec((tm, tn), lambda j, i: (i, j)),
                  pl.BlockSpec((1, 1, tn), lambda j, i: (i // tiles_per_seq, 0, j))],
        out_specs=pl.BlockSpec((tm, tn), lambda j, i: (i, j)),
        out_shape=jax.ShapeDtypeStruct((t, n), f32),
        scratch_shapes=[pltpu.VMEM((k, tn), bf16)],
        compiler_params=_params("arbitrary", "arbitrary"),
        name="out_proj",
    )(a, w, x2, gate)


def _rope_table_kernel(pos_ref, freq_ref, cos_ref, sin_ref):
    ang = pos_ref[...].astype(f32) * freq_ref[...]
    lane = lax.broadcasted_iota(jnp.int32, ang.shape, 1)
    first_half = (lane & (HEAD_DIM // 2)) == 0
    s = jnp.sin(ang)
    cos_ref[...] = jnp.cos(ang)
    sin_ref[...] = jnp.where(first_half, -s, s)


def _rope_tables(pos_col, freq_row, tm=1024):
    t = pos_col.shape[0]
    tm = min(tm, t)
    return pl.pallas_call(
        _rope_table_kernel,
        grid=(t // tm,),
        in_specs=[pl.BlockSpec((tm, 1), lambda i: (i, 0)),
                  pl.BlockSpec((1, LANES), lambda i: (0, 0))],
        out_specs=[pl.BlockSpec((tm, LANES), lambda i: (i, 0)),
                   pl.BlockSpec((tm, LANES), lambda i: (i, 0))],
        out_shape=[jax.ShapeDtypeStruct((t, LANES), f32)] * 2,
        compiler_params=_params("arbitrary"),
        name="rope_table",
    )(pos_col, freq_row)


def _attn_kernel(q_ref, k_ref, v_ref, g_ref, lq1_ref, lk1_ref, lq2_ref, lk2_ref, sw_ref, bias_ref, o_ref,
                 vt_ref, qq_ref, s_ref, mx_ref, acc_ref, *, bq, bk, hp, lambda_init):
    seq = q_ref.shape[0]
    nslot = bq // bk
    lam = (jnp.exp(jnp.sum(lq1_ref[...] * lk1_ref[...], axis=-1, keepdims=True))
           - jnp.exp(jnp.sum(lq2_ref[...] * lk2_ref[...], axis=-1, keepdims=True))
           + lambda_init)
    cols = [slice(hh * LANES, (hh + 1) * LANES) for hh in range(hp)]

    ones_rows = jnp.where(lax.broadcasted_iota(jnp.int32, (BF16_ROWS, bk), 0) == 0, 1.0, 0.0).astype(bf16)

    def transpose_v(c, _):
        start = pl.multiple_of(c * bk, bk)
        for hh in range(hp):
            vt_ref[hh, c, 0:LANES, :] = v_ref[pl.ds(start, bk), cols[hh]].astype(f32).T.astype(bf16)
            vt_ref[hh, c, LANES:, :] = ones_rows
        return 0
    lax.fori_loop(0, seq // bk, transpose_v, 0)

    d_row = lax.broadcasted_iota(jnp.int32, (LANES, bq), 0)

    def produce(hh, slot, j):
        start = pl.multiple_of(j * bk, bk)
        st = jnp.dot(k_ref[pl.ds(start, bk), cols[hh]], qq_ref[hh], preferred_element_type=f32)
        s_ref[hh, slot] = st
        mx_ref[hh, slot] = jnp.max(st, axis=0, keepdims=True)

    def consume(hh, slot, j, m, masked):
        st = s_ref[hh, slot]
        if masked:
            st = st + bias_ref[slot]
            blk_max = jnp.max(st, axis=0, keepdims=True)
        else:
            blk_max = mx_ref[hh, slot]
        m_new = jnp.maximum(m, blk_max)
        alpha = jnp.exp2(m - m_new)
        p = jnp.exp2(st - m_new)
        acc_ref[hh] = alpha * acc_ref[hh] + jnp.dot(vt_ref[hh, j], p.astype(bf16),
                                                    preferred_element_type=f32)
        return m_new

    def q_block(i, _):
        q_start = pl.multiple_of(i * bq, bq)
        for hh in range(hp):
            qt = q_ref[pl.ds(q_start, bq), cols[hh]].astype(f32).T
            qq_ref[hh] = jnp.concatenate([jnp.where(d_row < HEAD_DIM, qt, 0.0),
                                          jnp.where(d_row >= HEAD_DIM, qt, 0.0)], axis=1).astype(bf16)
            acc_ref[hh] = jnp.zeros(acc_ref.shape[1:], f32)
            for slot in range(nslot):
                produce(hh, slot, slot)

        def body(jj, carry):
            out = []
            for hh in range(hp):
                m = carry[hh]
                for slot in range(nslot):
                    m = consume(hh, slot, jj * nslot + slot, m, False)
                    produce(hh, slot, (jj + 1) * nslot + slot)
                out.append(m)
            return tuple(out)

        init = tuple(jnp.full((1, 2 * bq), NEG_BIG, f32) for _ in range(hp))
        carry = lax.fori_loop(0, i, body, init)

        for hh in range(hp):
            m = carry[hh]
            for slot in range(nslot):
                m = consume(hh, slot, i * nslot + slot, m, True)
            acc = acc_ref[hh]
            ot = acc[:LANES] / acc[LANES:LANES + 1]
            odt = ot[:, :bq] - lam * ot[:, bq:]
            ms = jnp.mean(odt * odt, axis=0, keepdims=True)
            y = (odt * lax.rsqrt(ms + EPS)).T * (sw_ref[...] * (1.0 - lambda_init))
            g = g_ref[pl.ds(q_start, bq), cols[hh]].astype(f32)
            o_ref[pl.ds(q_start, bq), cols[hh]] = (y * _silu(g)).astype(o_ref.dtype)
        return 0

    lax.fori_loop(0, seq // bq, q_block, 0)


def _diff_attn(qk, vg, lq1, lk1, lq2, lk2, subln_w, *, batch, seq, heads, lambda_init,
               bq=512, bk=512, hp=4):
    t = qk.shape[0]
    bq = min(bq, seq)
    bk = min(bk, bq)
    hp = math.gcd(hp, heads)
    nslot = bq // bk
    w = hp * LANES
    ng = heads // hp
    key = np.arange(bq)[:, None]
    qry = np.arange(bq)[None, :]
    tri = np.where(key <= qry, 0.0, NEG_BIG).astype(np.float32)
    bias = jnp.asarray(np.concatenate([tri, tri], axis=1).reshape(nslot, bk, 2 * bq))
    vec = lambda a: a.reshape(1, -1).astype(f32)
    small = pl.BlockSpec((1, HEAD_DIM), lambda b, h: (0, 0))
    single = pl.Buffered(1)
    kern = functools.partial(_attn_kernel, bq=bq, bk=bk, hp=hp, lambda_init=lambda_init)
    return pl.pallas_call(
        kern,
        grid=(batch, ng),
        in_specs=[pl.BlockSpec((seq, w), lambda b, h: (b, h), pipeline_mode=single),
                  pl.BlockSpec((seq, w), lambda b, h: (b, ng + h)),
                  pl.BlockSpec((seq, w), lambda b, h: (b, h)),
                  pl.BlockSpec((seq, w), lambda b, h: (b, ng + h), pipeline_mode=single),
                  small, small, small, small,
                  pl.BlockSpec((1, LANES), lambda b, h: (0, 0)),
                  pl.BlockSpec((nslot, bk, 2 * bq), lambda b, h: (0, 0, 0), pipeline_mode=single)],
        out_specs=pl.BlockSpec((seq, w), lambda b, h: (b, h)),
        out_shape=jax.ShapeDtypeStruct((t, heads * LANES), bf16),
        scratch_shapes=[pltpu.VMEM((hp, seq // bk, LANES + BF16_ROWS, bk), bf16),
                        pltpu.VMEM((hp, LANES, 2 * bq), bf16),
                        pltpu.VMEM((hp, nslot, bk, 2 * bq), f32),
                        pltpu.VMEM((hp, nslot, 1, 2 * bq), f32),
                        pltpu.VMEM((hp, LANES + BF16_ROWS, 2 * bq), f32)],
        compiler_params=_params("arbitrary", "arbitrary"),
        name="diff_attn",
    )(qk, qk, vg, vg, vec(lq1), vec(lk1), vec(lq2), vec(lk2), vec(subln_w), bias)


def _ssd_constants(rheads):
    L, P = SSM_CHUNK, SSM_HEAD_DIM
    k = np.arange(LANES)[:, None]
    col = np.arange(rheads * LANES)[None, :]
    seg = (k < 3 * rheads) & (k % rheads == col // LANES)
    col2 = np.arange(2 * rheads * P)[None, :]
    half = rheads * P
    exp = np.where(col2 < half,
                   (k < 2 * rheads) & (k % rheads == col2 // P),
                   (k >= 2 * rheads) & (k < 4 * rheads) & (k % rheads == (col2 - half) // P))
    upper = (np.arange(L)[:, None] <= np.arange(L)[None, :])
    return jnp.asarray(seg, dtype=bf16), jnp.asarray(exp, dtype=bf16), jnp.asarray(upper, dtype=bf16)


def _split_bf16(a, pieces):
    out, rem = [], a
    for _ in range(pieces):
        p = rem.astype(bf16).astype(f32)
        out.append(p)
        rem = rem - p
    return out


def _ssd_kernel(dt_ref, dtb_ref, alog_ref, z_ref, xs_ref, b_ref, c_ref, dskip_ref, nw_ref,
                segoh_ref, expoh_ref, upper_ref,
                o_ref,
                state_ref, cs_ref, dtc_ref, cst_ref, dtt_ref,
                *, groups, rheads, gp):
    c = pl.program_id(1)
    pg = pl.program_id(2)

    @pl.when(c == 0)
    def _():
        state_ref[:, pg] = jnp.zeros((gp,) + state_ref.shape[2:], f32)

    def bookkeeping():
        raw = dt_ref[...] + dtb_ref[...]
        dt = jnp.maximum(raw, 0.0) + jnp.log(1.0 + jnp.exp(-jnp.abs(raw)))
        da_t = (dt * (-jnp.exp(alog_ref[...]))).T
        cs_t = sum(jnp.dot(piece.astype(bf16), upper_ref[...], preferred_element_type=f32)
                   for piece in _split_bf16(da_t, 3))
        cst_ref[...] = cs_t
        dtt_ref[...] = dt.T
        cs = cs_t.T
        for gg in range(groups):
            shift = (LANES - gg * rheads) % LANES
            cs_ref[gg] = pltpu.roll(cs, shift, 1) if shift else cs
            dtc_ref[gg] = pltpu.roll(dt, shift, 1) if shift else dt

    if gp == groups:
        bookkeeping()
    else:
        pl.when(pg == 0)(bookkeeping)

    for gi in range(gp):
        _ssd_group(gi, pg, z_ref, xs_ref, b_ref, c_ref, dskip_ref, nw_ref, segoh_ref, expoh_ref,
                   o_ref, state_ref, cs_ref, dtc_ref, cst_ref, dtt_ref, rheads, gp)


def _ssd_group(gi, pg, z_ref, xs_ref, b_ref, c_ref, dskip_ref, nw_ref, segoh_ref, expoh_ref, o_ref,
               state_ref, cs_ref, dtc_ref, cst_ref, dtt_ref, rheads, gp):
    L = SSM_CHUNK
    R = rheads
    xw = rheads * SSM_HEAD_DIM
    n = SSM_STATE
    g = pg * gp + gi
    xsl = slice(gi * xw, (gi + 1) * xw)
    nsl = slice(gi * n, (gi + 1) * n)

    xs = xs_ref[:, xsl].astype(f32)
    bm16 = b_ref[:, nsl]
    cm16 = c_ref[:, nsl]

    row0 = pl.multiple_of(g * rheads, rheads)
    cs_t = cst_ref[pl.ds(row0, rheads), :]
    dt_t = dtt_ref[pl.ds(row0, rheads), :]
    cs_c = cs_ref[g]
    dt_c = dtc_ref[g]
    lane = lax.broadcasted_iota(jnp.int32, (L, LANES), 1)

    hi, mid, lo = _split_bf16(cs_c, 3)
    packed = jnp.where(lane < R, hi, jnp.where(lane < 2 * R, pltpu.roll(mid, R, 1), pltpu.roll(lo, 2 * R, 1)))
    seg_col = jnp.dot(packed.astype(bf16), segoh_ref[...], preferred_element_type=f32)

    cb = lax.dot_general(cm16, bm16, (((1,), (1,)), ((), ())),
                         preferred_element_type=f32)
    l_idx = lax.broadcasted_iota(jnp.int32, (L, L), 0)
    s_idx = lax.broadcasted_iota(jnp.int32, (L, L), 1)
    causal = l_idx >= s_idx
    lo_half = lane < SSM_HEAD_DIM

    y_parts = []
    for j in range(rheads // 2):
        xj = xs[:, j * LANES:(j + 1) * LANES]
        x_lo = jnp.where(lo_half, xj, 0.0).astype(bf16)
        x_hi = jnp.where(lo_half, 0.0, xj).astype(bf16)
        yj = None
        for r, xr in ((2 * j, x_lo), (2 * j + 1, x_hi)):
            seg = seg_col[:, r * LANES:(r + 1) * LANES] - cs_t[r:r + 1, :]
            decay = jnp.exp(jnp.where(causal, seg, -jnp.inf))
            mr = (cb * decay * dt_t[r:r + 1, :]).astype(bf16)
            part = jnp.dot(mr, xr, preferred_element_type=f32)
            yj = part if yj is None else yj + part
        y_parts.append(yj)
    y_diag = jnp.concatenate(y_parts, axis=1)

    cs_last = cs_c[L - 1:L, :]
    e_hi, e_lo = _split_bf16(jnp.exp(cs_c), 2)
    w_hi, w_lo = _split_bf16(jnp.exp(cs_last - cs_c) * dt_c, 2)
    packed2 = jnp.where(lane < R, e_hi,
                        jnp.where(lane < 2 * R, pltpu.roll(e_lo, R, 1),
                                  jnp.where(lane < 3 * R, pltpu.roll(w_hi, 2 * R, 1),
                                            pltpu.roll(w_lo, 3 * R, 1))))
    spread = jnp.dot(packed2.astype(bf16), expoh_ref[...], preferred_element_type=f32)
    e_cs = spread[:, :xw]
    wgt = spread[:, xw:]

    prev = state_ref[gi, pg]
    y_off = jnp.dot(cm16, prev.astype(bf16), preferred_element_type=f32) * e_cs
    new = jnp.dot(bm16.astype(f32).T.astype(bf16), (xs * wgt).astype(bf16), preferred_element_type=f32)
    state_ref[gi, pg] = prev * e_cs[L - 1:L, :] + new

    y = y_diag + y_off + xs * dskip_ref[:, xsl]
    y = y * z_ref[:, xsl].astype(f32)
    ms = jnp.mean(y * y, axis=-1, keepdims=True)
    o_ref[:, xsl] = (y * lax.rsqrt(ms + EPS) * nw_ref[:, xsl]).astype(o_ref.dtype)


def _ssd(zproj, xbc, dt_raw, dt_bias, a_log, d_skip, norm_w, *, batch, seq, d_inner, groups, gp=8):
    t = zproj.shape[0]
    L = SSM_CHUNK
    nc = seq // L
    heads = d_inner // SSM_HEAD_DIM
    rheads = heads // groups
    assert rheads % 2 == 0 and 4 * rheads <= LANES
    gp = math.gcd(gp, groups)
    xw = rheads * SSM_HEAD_DIM
    n = SSM_STATE
    bw, bn = gp * xw, gp * n
    assert d_inner % bw == 0 and d_inner % bn == 0
    b_blk0 = d_inner // bn
    c_blk0 = b_blk0 + groups // gp
    row = lambda b, c, g: b * nc + c
    pad = lambda a: jnp.pad(a.reshape(1, -1).astype(f32), ((0, 0), (0, LANES - heads)))
    const2 = lambda shape: pl.BlockSpec(shape, lambda b, c, g: (0, 0))
    seg_oh, exp_oh, upper = _ssd_constants(rheads)
    kern = functools.partial(_ssd_kernel, groups=groups, rheads=rheads, gp=gp)
    return pl.pallas_call(
        kern,
        grid=(batch, nc, groups // gp),
        in_specs=[pl.BlockSpec((L, LANES), lambda b, c, g: (row(b, c, g), 0)),
                  const2((1, LANES)),
                  const2((1, LANES)),
                  pl.BlockSpec((L, bw), lambda b, c, g: (row(b, c, g), g)),
                  pl.BlockSpec((L, bw), lambda b, c, g: (row(b, c, g), g)),
                  pl.BlockSpec((L, bn), lambda b, c, g: (row(b, c, g), b_blk0 + g)),
                  pl.BlockSpec((L, bn), lambda b, c, g: (row(b, c, g), c_blk0 + g)),
                  pl.BlockSpec((1, bw), lambda b, c, g: (0, g)),
                  pl.BlockSpec((1, bw), lambda b, c, g: (0, g)),
                  const2((LANES, rheads * LANES)),
                  const2((LANES, 2 * xw)),
                  const2((L, L))],
        out_specs=pl.BlockSpec((L, bw), lambda b, c, g: (row(b, c, g), g)),
        out_shape=jax.ShapeDtypeStruct((t, d_inner), bf16),
        scratch_shapes=[pltpu.VMEM((gp, groups // gp, n, xw), f32),
                        pltpu.VMEM((groups, L, LANES), f32),
                        pltpu.VMEM((groups, L, LANES), f32),
                        pltpu.VMEM((LANES, L), f32),
                        pltpu.VMEM((LANES, L), f32)],
        compiler_params=_params("arbitrary", "arbitrary", "arbitrary"),
        name="ssd_chunk",
    )(dt_raw, pad(dt_bias), pad(a_log), zproj, xbc, xbc, xbc,
      jnp.repeat(d_skip, SSM_HEAD_DIM).reshape(1, -1), norm_w.reshape(1, -1),
      seg_oh, exp_oh, upper)


def kernel(x, c, positions, norm_w, ada_w, ada_b, attn_w_in, attn_q_norm, attn_k_norm, attn_lambda_q1, attn_lambda_k1, attn_lambda_q2, attn_lambda_k2, attn_subln_w, attn_w_out, ssm_w_in, ssm_conv_w, ssm_conv_b, ssm_dt_bias, ssm_A_log, ssm_D, ssm_norm_w, ssm_w_out):
    batch, seq, d = x.shape
    depth = norm_w.shape[0]
    t = batch * seq
    heads = d // (2 * HEAD_DIM)
    qk_width = heads * 2 * HEAD_DIM
    d_inner = ssm_w_out.shape[1]
    conv_dim = ssm_conv_w.shape[2]
    groups = (conv_dim - d_inner) // (2 * SSM_STATE)
    ssm_heads = ssm_dt_bias.shape[1]
    tm = min(1024, seq)

    c_pad = jnp.pad(c, ((0, 8 - batch), (0, 0)))
    mod = _ada_mod(c_pad, ada_w, ada_b)[:, :batch]
    shift, scale, gate = (mod[:, :, i * d:(i + 1) * d].reshape(depth, batch, 1, d) for i in range(3))

    inv_freq = ROPE_THETA ** (-jnp.arange(0, HEAD_DIM, 2, dtype=f32) / HEAD_DIM)
    freq_row = jnp.tile(inv_freq, LANES // (HEAD_DIM // 2)).reshape(1, LANES)
    cos_t, sin_t = _rope_tables(positions.reshape(t, 1), freq_row)

    main = d_inner + conv_dim
    assert main % LANES == 0 and ssm_heads <= LANES
    w_ssm = jnp.pad(ssm_w_in, ((0, 0), (0, 0), (0, LANES - ssm_heads))).astype(bf16)

    x2 = x.reshape(t, d)
    for layer in range(depth):
        j = layer // 2
        h = _norm_mod(x2, norm_w[layer].reshape(1, d), scale[layer], shift[layer], seq)
        if layer % 2 == 0:
            lambda_init = 0.8 - 0.6 * math.exp(-0.3 * layer)
            rep = qk_width // HEAD_DIM
            qk_w = jnp.concatenate([jnp.tile(attn_q_norm[j], rep) * (HEAD_DIM ** -0.5 * math.log2(math.e)),
                                    jnp.tile(attn_k_norm[j], rep)]).reshape(1, 2 * qk_width)
            tn_a = _tile(qk_width, 1024)
            qk = _proj_qk(h, attn_w_in, j, qk_w, cos_t, sin_t, tm, tn_a)
            vg = _proj(h, attn_w_in, j, attn_w_in.shape[2] - 2 * qk_width, bf16, tm, tn_a,
                       col0=2 * qk_width)
            o = _diff_attn(qk, vg, attn_lambda_q1[j], attn_lambda_k1[j], attn_lambda_q2[j],
                           attn_lambda_k2[j], attn_subln_w[j], batch=batch, seq=seq, heads=heads,
                           lambda_init=lambda_init)
            x2 = _out_proj(o, attn_w_out, j, x2, gate[layer], seq, tm, _tile(d, 1024))
        else:
            tn_c = _tile(math.gcd(d_inner, conv_dim), 1024)
            zact = _proj(h, w_ssm, j, d_inner, bf16, tm, tn_c, silu=True)
            xbc = _proj_conv(h, w_ssm, j, d_inner, ssm_conv_w[j], ssm_conv_b[j].reshape(1, -1), seq, tm, tn_c)
            dt_raw = _proj(h, w_ssm, j, LANES, f32, tm, LANES, col0=main)
            y = _ssd(zact, xbc, dt_raw, ssm_dt_bias[j], ssm_A_log[j], ssm_D[j], ssm_norm_w[j],
                     batch=batch, seq=seq, d_inner=d_inner, groups=groups)
            x2 = _out_proj(y, ssm_w_out, j, x2, gate[layer], seq, tm, _tile(d, 512))
    return x2.reshape(batch, seq, d)
```

```python
import functools
import math

import numpy as np
import jax
import jax.numpy as jnp
from jax import lax
from jax.experimental import pallas as pl
from jax.experimental.pallas import tpu as pltpu

EPS = 1e-6
ROPE_THETA = 10000.0
LANES = 128
BF16_ROWS = 16
CONV_HALO = 8
QK_ROW_CHUNKS = 4
CONV_ROW_CHUNKS = 4
HEAD_DIM = 64
SSM_HEAD_DIM = 64
SSM_STATE = 128
SSM_CHUNK = 128
SSM_CONV = 4
VMEM_LIMIT_BYTES = 56 * 1024 * 1024
NEG_BIG = -1e30

f32 = jnp.float32
bf16 = jnp.bfloat16


def _params(*sem):
    return pltpu.CompilerParams(dimension_semantics=sem, vmem_limit_bytes=VMEM_LIMIT_BYTES)


def _silu(x):
    return x * jax.nn.sigmoid(x)


def _tile(n, target):
    if n <= target:
        return n
    best = None
    for cand in range(LANES, target + 1, LANES):
        if n % cand == 0:
            best = cand
    assert best is not None, (n, target)
    return best


def _ada_kernel(c_ref, w_ref, b_ref, o_ref):
    cond = _silu(c_ref[...])
    o_ref[0] = jnp.dot(cond, w_ref[0], preferred_element_type=f32) + b_ref[0]


def _ada_mod(c_pad, ada_w, ada_b, tn=1024):
    depth, d, n = ada_w.shape
    rows = c_pad.shape[0]
    tn = _tile(n, tn)
    return pl.pallas_call(
        _ada_kernel,
        grid=(depth, n // tn),
        in_specs=[pl.BlockSpec((rows, d), lambda l, j: (0, 0)),
                  pl.BlockSpec((1, d, tn), lambda l, j: (l, 0, j)),
                  pl.BlockSpec((1, 1, tn), lambda l, j: (l, 0, j))],
        out_specs=pl.BlockSpec((1, rows, tn), lambda l, j: (l, 0, j)),
        out_shape=jax.ShapeDtypeStruct((depth, rows, n), f32),
        compiler_params=_params("arbitrary", "arbitrary"),
        name="ada_mod",
    )(c_pad, ada_w, ada_b.reshape(depth, 1, n))


def _norm_kernel(x_ref, nw_ref, sc_ref, sh_ref, o_ref):
    x = x_ref[...]
    ms = jnp.mean(x * x, axis=-1, keepdims=True)
    y = x * lax.rsqrt(ms + EPS) * nw_ref[...]
    o_ref[...] = (y * (1.0 + sc_ref[0]) + sh_ref[0]).astype(o_ref.dtype)


def _norm_mod(x2, nw, scale, shift, seq, tm=512):
    t, d = x2.shape
    tm = min(tm, seq)
    tiles_per_seq = seq // tm
    return pl.pallas_call(
        _norm_kernel,
        grid=(t // tm,),
        in_specs=[pl.BlockSpec((tm, d), lambda i: (i, 0)),
                  pl.BlockSpec((1, d), lambda i: (0, 0)),
                  pl.BlockSpec((1, 1, d), lambda i: (i // tiles_per_seq, 0, 0)),
                  pl.BlockSpec((1, 1, d), lambda i: (i // tiles_per_seq, 0, 0))],
        out_specs=pl.BlockSpec((tm, d), lambda i: (i, 0)),
        out_shape=jax.ShapeDtypeStruct((t, d), bf16),
        compiler_params=_params("arbitrary"),
        name="norm_mod",
    )(x2, nw, scale, shift)


def _stage_weight(w_ref, wb_ref):
    @pl.when(pl.program_id(1) == 0)
    def _():
        wb_ref[...] = w_ref[0].astype(bf16)


def _proj_kernel(a_ref, w_ref, o_ref, wb_ref, *, silu):
    _stage_weight(w_ref, wb_ref)
    acc = jnp.dot(a_ref[...], wb_ref[...], preferred_element_type=f32)
    if silu:
        acc = acc / (1.0 + jnp.exp(-acc))
    o_ref[...] = acc.astype(o_ref.dtype)


def _proj(a, w, w_layer, n_out, out_dtype, tm, tn, silu=False, col0=0):
    t, k = a.shape
    assert col0 % tn == 0
    cb0 = col0 // tn
    kern = functools.partial(_proj_kernel, silu=silu)
    return pl.pallas_call(
        kern,
        grid=(n_out // tn, t // tm),
        in_specs=[pl.BlockSpec((tm, k), lambda j, i: (i, 0)),
                  pl.BlockSpec((1, k, tn), lambda j, i: (w_layer, 0, cb0 + j))],
        out_specs=pl.BlockSpec((tm, tn), lambda j, i: (i, j)),
        out_shape=jax.ShapeDtypeStruct((t, n_out), out_dtype),
        scratch_shapes=[pltpu.VMEM((k, tn), bf16)],
        compiler_params=_params("arbitrary", "arbitrary"),
        name="proj",
    )(a, w)


def _proj_qk_kernel(a_ref, w_ref, qw_ref, cos_ref, sin_ref, o_ref, wb_ref):
    _stage_weight(w_ref, wb_ref)
    tm, tn = o_ref.shape
    r_idx = lax.broadcasted_iota(jnp.int32, (LANES, LANES), 0) // HEAD_DIM
    c_idx = lax.broadcasted_iota(jnp.int32, (LANES, LANES), 1) // HEAD_DIM
    group_mean = jnp.where(r_idx == c_idx, 1.0 / HEAD_DIM, 0.0).astype(bf16)
    rc = tm // QK_ROW_CHUNKS
    lane = lax.broadcasted_iota(jnp.int32, (rc, LANES), 1)
    first_half = (lane & (HEAD_DIM // 2)) == 0
    w = wb_ref[...]
    for r in range(QK_ROW_CHUNKS):
        rows = slice(r * rc, (r + 1) * rc)
        acc = jnp.dot(a_ref[rows, :], w, preferred_element_type=f32)
        cos = cos_ref[rows, :]
        sin = sin_ref[rows, :]
        for c in range(tn // LANES):
            sl = slice(c * LANES, (c + 1) * LANES)
            x = acc[:, sl]
            ms = jnp.dot((x * x).astype(bf16), group_mean, preferred_element_type=f32)
            y = x * lax.rsqrt(ms + EPS) * qw_ref[:, sl]
            partner = jnp.where(first_half,
                                pltpu.roll(y, LANES - HEAD_DIM // 2, 1),
                                pltpu.roll(y, HEAD_DIM // 2, 1))
            o_ref[rows, sl] = (y * cos + partner * sin).astype(o_ref.dtype)


def _proj_qk(a, w, w_layer, qk_w, cos_t, sin_t, tm, tn):
    t, k = a.shape
    n = qk_w.shape[1]
    return pl.pallas_call(
        _proj_qk_kernel,
        grid=(n // tn, t // tm),
        in_specs=[pl.BlockSpec((tm, k), lambda j, i: (i, 0)),
                  pl.BlockSpec((1, k, tn), lambda j, i: (w_layer, 0, j)),
                  pl.BlockSpec((1, tn), lambda j, i: (0, j)),
                  pl.BlockSpec((tm, LANES), lambda j, i: (i, 0)),
                  pl.BlockSpec((tm, LANES), lambda j, i: (i, 0))],
        out_specs=pl.BlockSpec((tm, tn), lambda j, i: (i, j)),
        out_shape=jax.ShapeDtypeStruct((t, n), bf16),
        scratch_shapes=[pltpu.VMEM((k, tn), bf16)],
        compiler_params=_params("arbitrary", "arbitrary"),
        name="proj_qk",
    )(a, w, qk_w, cos_t, sin_t)


def _proj_conv_kernel(*refs, tiles_per_seq, with_gate):
    if with_gate:
        a_ref, w_ref, cw_ref, cb_ref, wz_ref, o_ref, oz_ref, halo_ref = refs
    else:
        a_ref, w_ref, cw_ref, cb_ref, o_ref, halo_ref = refs
    i = pl.program_id(1)

    tm, tn = o_ref.shape
    rc = tm // CONV_ROW_CHUNKS
    w = w_ref[0]
    first = (i % tiles_per_seq) == 0
    prev = jnp.where(first, 0.0, halo_ref[...])
    nsl = rc // CONV_HALO
    sub = lax.broadcasted_iota(jnp.int32, (nsl + 1, CONV_HALO, tn), 1)
    w0, w1, w2, w3 = (cw_ref[tap:tap + 1, :] for tap in range(SSM_CONV))

    def shift_rows(slabs, back):
        rot = pltpu.roll(slabs, back, 1)
        above = jnp.concatenate([rot[:1], rot[:-1]], axis=0)
        return jnp.where(sub < back, above, rot)

    for r in range(CONV_ROW_CHUNKS):
        acc = jnp.dot(a_ref[r * rc:(r + 1) * rc, :], w, preferred_element_type=f32)
        slabs = jnp.concatenate([prev, acc], axis=0).reshape(nsl + 1, CONV_HALO, tn)
        x1 = shift_rows(slabs, 1)
        v2 = shift_rows(w1 * slabs + w0 * x1, 2)
        y = (cb_ref[...] + w3 * slabs + w2 * x1 + v2)[1:].reshape(rc, tn)
        o_ref[r * rc:(r + 1) * rc, :] = (y / (1.0 + jnp.exp(-y))).astype(o_ref.dtype)
        prev = acc[rc - CONV_HALO:, :]
        if with_gate:
            z = jnp.dot(a_ref[r * rc:(r + 1) * rc, :], wz_ref[0], preferred_element_type=f32)
            oz_ref[r * rc:(r + 1) * rc, :] = (z / (1.0 + jnp.exp(-z))).astype(oz_ref.dtype)
    halo_ref[...] = prev


def _proj_conv(a, w, w_layer, col0, n, conv_w, conv_b, conv_col0, seq, tm, tn, gate_col0=None):
    t, k = a.shape
    assert col0 % tn == 0 and conv_col0 % tn == 0 and n % tn == 0
    cb0, ccb0 = col0 // tn, conv_col0 // tn
    tiles_per_seq = seq // tm
    with_gate = gate_col0 is not None
    kern = functools.partial(_proj_conv_kernel, tiles_per_seq=tiles_per_seq, with_gate=with_gate)
    in_specs = [pl.BlockSpec((tm, k), lambda j, i: (i, 0)),
                pl.BlockSpec((1, k, tn), lambda j, i: (w_layer, 0, cb0 + j)),
                pl.BlockSpec((SSM_CONV, tn), lambda j, i: (0, ccb0 + j)),
                pl.BlockSpec((1, tn), lambda j, i: (0, ccb0 + j))]
    out_spec = pl.BlockSpec((tm, tn), lambda j, i: (i, j))
    out_shape = jax.ShapeDtypeStruct((t, n), bf16)
    operands = [a, w, conv_w, conv_b]
    if with_gate:
        assert gate_col0 % tn == 0
        gb0 = gate_col0 // tn
        in_specs.append(pl.BlockSpec((1, k, tn), lambda j, i: (w_layer, 0, gb0 + j)))
        operands.append(w)
    return pl.pallas_call(
        kern,
        grid=(n // tn, t // tm),
        in_specs=in_specs,
        out_specs=[out_spec, out_spec] if with_gate else out_spec,
        out_shape=[out_shape, out_shape] if with_gate else out_shape,
        scratch_shapes=[pltpu.VMEM((CONV_HALO, tn), f32)],
        compiler_params=_params("arbitrary", "arbitrary"),
        name="proj_conv",
    )(*operands)


def _outproj_kernel(a_ref, w_ref, x_ref, g_ref, o_ref, wb_ref):
    _stage_weight(w_ref, wb_ref)
    acc = jnp.dot(a_ref[...], wb_ref[...], preferred_element_type=f32)
    o_ref[...] = x_ref[...] + g_ref[0] * acc


def _out_proj(a, w, w_layer, x2, gate, seq, tm, tn):
    t, k = a.shape
    n = w.shape[2]
    tiles_per_seq = seq // tm
    return pl.pallas_call(
        _outproj_kernel,
        grid=(n // tn, t // tm),
        in_specs=[pl.BlockSpec((tm, k), lambda j, i: (i, 0)),
                  pl.BlockSpec((1, k, tn), lambda j, i: (w_layer, 0, j)),
                  pl.BlockSpec((tm, tn), lambda j, i: (i, j)),
                  pl.BlockSpec((1, 1, tn), lambda j, i: (i // tiles_per_seq, 0, j))],
        out_specs=pl.BlockSpec((tm, tn), lambda j, i: (i, j)),
        out_shape=jax.ShapeDtypeStruct((t, n), f32),
        scratch_shapes=[pltpu.VMEM((k, tn), bf16)],
        compiler_params=_params("arbitrary", "arbitrary"),
        name="out_proj",
    )(a, w, x2, gate)


def _rope_table_kernel(pos_ref, freq_ref, cos_ref, sin_ref):
    ang = pos_ref[...].astype(f32) * freq_ref[...]
    lane = lax.broadcasted_iota(jnp.int32, ang.shape, 1)
    first_half = (lane & (HEAD_DIM // 2)) == 0
    s = jnp.sin(ang)
    cos_ref[...] = jnp.cos(ang)
    sin_ref[...] = jnp.where(first_half, -s, s)


def _rope_tables(pos_col, freq_row, tm=1024):
    t = pos_col.shape[0]
    tm = min(tm, t)
    return pl.pallas_call(
        _rope_table_kernel,
        grid=(t // tm,),
        in_specs=[pl.BlockSpec((tm, 1), lambda i: (i, 0)),
                  pl.BlockSpec((1, LANES), lambda i: (0, 0))],
        out_specs=[pl.BlockSpec((tm, LANES), lambda i: (i, 0)),
                   pl.BlockSpec((tm, LANES), lambda i: (i, 0))],
        out_shape=[jax.ShapeDtypeStruct((t, LANES), f32)] * 2,
        compiler_params=_params("arbitrary"),
        name="rope_table",
    )(pos_col, freq_row)


def _attn_kernel(q_ref, k_ref, v_ref, g_ref, lq1_ref, lk1_ref, lq2_ref, lk2_ref, sw_ref, bias_ref, o_ref,
                 vt_ref, qq_ref, s_ref, mx_ref, acc_ref, *, bq, bk, hp, lambda_init):
    seq = q_ref.shape[0]
    nslot = bq // bk
    lam = (jnp.exp(jnp.sum(lq1_ref[...] * lk1_ref[...], axis=-1, keepdims=True))
           - jnp.exp(jnp.sum(lq2_ref[...] * lk2_ref[...], axis=-1, keepdims=True))
           + lambda_init)
    cols = [slice(hh * LANES, (hh + 1) * LANES) for hh in range(hp)]

    ones_rows = jnp.where(lax.broadcasted_iota(jnp.int32, (BF16_ROWS, bk), 0) == 0, 1.0, 0.0).astype(bf16)

    def transpose_v(c, _):
        start = pl.multiple_of(c * bk, bk)
        for hh in range(hp):
            vt_ref[hh, c, 0:LANES, :] = v_ref[pl.ds(start, bk), cols[hh]].astype(f32).T.astype(bf16)
            vt_ref[hh, c, LANES:, :] = ones_rows
        return 0
    lax.fori_loop(0, seq // bk, transpose_v, 0)

    d_row = lax.broadcasted_iota(jnp.int32, (LANES, bq), 0)

    def produce(hh, slot, j):
        start = pl.multiple_of(j * bk, bk)
        st = jnp.dot(k_ref[pl.ds(start, bk), cols[hh]], qq_ref[hh], preferred_element_type=f32)
        s_ref[hh, slot] = st
        mx_ref[hh, slot] = jnp.max(st, axis=0, keepdims=True)

    def consume(hh, slot, j, m, masked):
        st = s_ref[hh, slot]
        if masked:
            st = st + bias_ref[slot]
            blk_max = jnp.max(st, axis=0, keepdims=True)
        else:
            blk_max = mx_ref[hh, slot]
        m_new = jnp.maximum(m, blk_max)
        alpha = jnp.exp2(m - m_new)
        p = jnp.exp2(st - m_new)
        acc_ref[hh] = alpha * acc_ref[hh] + jnp.dot(vt_ref[hh, j], p.astype(bf16),
                                                    preferred_element_type=f32)
        return m_new

    def q_block(i, _):
        q_start = pl.multiple_of(i * bq, bq)
        for hh in range(hp):
            qt = q_ref[pl.ds(q_start, bq), cols[hh]].astype(f32).T
            qq_ref[hh] = jnp.concatenate([jnp.where(d_row < HEAD_DIM, qt, 0.0),
                                          jnp.where(d_row >= HEAD_DIM, qt, 0.0)], axis=1).astype(bf16)
            acc_ref[hh] = jnp.zeros(acc_ref.shape[1:], f32)
            for slot in range(nslot):
                produce(hh, slot, slot)

        def body(jj, carry):
            out = []
            for hh in range(hp):
                m = carry[hh]
                for slot in range(nslot):
                    m = consume(hh, slot, jj * nslot + slot, m, False)
                    produce(hh, slot, (jj + 1) * nslot + slot)
                out.append(m)
            return tuple(out)

        init = tuple(jnp.full((1, 2 * bq), NEG_BIG, f32) for _ in range(hp))
        carry = lax.fori_loop(0, i, body, init)

        for hh in range(hp):
            m = carry[hh]
            for slot in range(nslot):
                m = consume(hh, slot, i * nslot + slot, m, True)
            acc = acc_ref[hh]
            ot = acc[:LANES] / acc[LANES:LANES + 1]
            odt = ot[:, :bq] - lam * ot[:, bq:]
            ms = jnp.mean(odt * odt, axis=0, keepdims=True)
            y = (odt * lax.rsqrt(ms + EPS)).T * (sw_ref[...] * (1.0 - lambda_init))
            g = g_ref[pl.ds(q_start, bq), cols[hh]].astype(f32)
            o_ref[pl.ds(q_start, bq), cols[hh]] = (y * _silu(g)).astype(o_ref.dtype)
        return 0

    lax.fori_loop(0, seq // bq, q_block, 0)


def _diff_attn(qk, vg, lq1, lk1, lq2, lk2, subln_w, *, batch, seq, heads, lambda_init,
               bq=512, bk=512, hp=4):
    t = qk.shape[0]
    bq = min(bq, seq)
    bk = min(bk, bq)
    hp = math.gcd(hp, heads)
    nslot = bq // bk
    w = hp * LANES
    ng = heads // hp
    key = np.arange(bq)[:, None]
    qry = np.arange(bq)[None, :]
    tri = np.where(key <= qry, 0.0, NEG_BIG).astype(np.float32)
    bias = jnp.asarray(np.concatenate([tri, tri], axis=1).reshape(nslot, bk, 2 * bq))
    vec = lambda a: a.reshape(1, -1).astype(f32)
    small = pl.BlockSpec((1, HEAD_DIM), lambda b, h: (0, 0))
    single = pl.Buffered(1)
    kern = functools.partial(_attn_kernel, bq=bq, bk=bk, hp=hp, lambda_init=lambda_init)
    return pl.pallas_call(
        kern,
        grid=(batch, ng),
        in_specs=[pl.BlockSpec((seq, w), lambda b, h: (b, h), pipeline_mode=single),
                  pl.BlockSpec((seq, w), lambda b, h: (b, ng + h)),
                  pl.BlockSpec((seq, w), lambda b, h: (b, h)),
                  pl.BlockSpec((seq, w), lambda b, h: (b, ng + h), pipeline_mode=single),
                  small, small, small, small,
                  pl.BlockSpec((1, LANES), lambda b, h: (0, 0)),
                  pl.BlockSpec((nslot, bk, 2 * bq), lambda b, h: (0, 0, 0), pipeline_mode=single)],
        out_specs=pl.BlockSpec((seq, w), lambda b, h: (b, h)),
        out_shape=jax.ShapeDtypeStruct((t, heads * LANES), bf16),
        scratch_shapes=[pltpu.VMEM((hp, seq // bk, LANES + BF16_ROWS, bk), bf16),
                        pltpu.VMEM((hp, LANES, 2 * bq), bf16),
                        pltpu.VMEM((hp, nslot, bk, 2 * bq), f32),
                        pltpu.VMEM((hp, nslot, 1, 2 * bq), f32),
                        pltpu.VMEM((hp, LANES + BF16_ROWS, 2 * bq), f32)],
        compiler_params=_params("arbitrary", "arbitrary"),
        name="diff_attn",
    )(qk, qk, vg, vg, vec(lq1), vec(lk1), vec(lq2), vec(lk2), vec(subln_w), bias)


def _ssd_constants(rheads):
    L, P = SSM_CHUNK, SSM_HEAD_DIM
    k = np.arange(LANES)[:, None]
    col = np.arange(rheads * LANES)[None, :]
    seg = (k < 3 * rheads) & (k % rheads == col // LANES)
    col2 = np.arange(2 * rheads * P)[None, :]
    half = rheads * P
    exp = np.where(col2 < half,
                   (k < 2 * rheads) & (k % rheads == col2 // P),
                   (k >= 2 * rheads) & (k < 4 * rheads) & (k % rheads == (col2 - half) // P))
    upper = (np.arange(L)[:, None] <= np.arange(L)[None, :])
    return jnp.asarray(seg, dtype=bf16), jnp.asarray(exp, dtype=bf16), jnp.asarray(upper, dtype=bf16)


def _split_bf16(a, pieces):
    out, rem = [], a
    for _ in range(pieces):
        p = rem.astype(bf16).astype(f32)
        out.append(p)
        rem = rem - p
    return out


def _ssd_kernel(dt_ref, dtb_ref, alog_ref, z_ref, xs_ref, b_ref, c_ref, dskip_ref, nw_ref,
                segoh_ref, expoh_ref, upper_ref,
                o_ref,
                state_ref, cs_ref, dtc_ref, cst_ref, dtt_ref,
                *, groups, rheads, gp):
    c = pl.program_id(1)
    pg = pl.program_id(2)

    @pl.when(c == 0)
    def _():
        state_ref[:, pg] = jnp.zeros((gp,) + state_ref.shape[2:], f32)

    def bookkeeping():
        raw = dt_ref[...] + dtb_ref[...]
        dt = jnp.maximum(raw, 0.0) + jnp.log(1.0 + jnp.exp(-jnp.abs(raw)))
        da_t = (dt * (-jnp.exp(alog_ref[...]))).T
        cs_t = sum(jnp.dot(piece.astype(bf16), upper_ref[...], preferred_element_type=f32)
                   for piece in _split_bf16(da_t, 3))
        cst_ref[...] = cs_t
        dtt_ref[...] = dt.T
        cs = cs_t.T
        for gg in range(groups):
            shift = (LANES - gg * rheads) % LANES
            cs_ref[gg] = pltpu.roll(cs, shift, 1) if shift else cs
            dtc_ref[gg] = pltpu.roll(dt, shift, 1) if shift else dt

    if gp == groups:
        bookkeeping()
    else:
        pl.when(pg == 0)(bookkeeping)

    for gi in range(gp):
        _ssd_group(gi, pg, z_ref, xs_ref, b_ref, c_ref, dskip_ref, nw_ref, segoh_ref, expoh_ref,
                   o_ref, state_ref, cs_ref, dtc_ref, cst_ref, dtt_ref, rheads, gp)


def _ssd_group(gi, pg, z_ref, xs_ref, b_ref, c_ref, dskip_ref, nw_ref, segoh_ref, expoh_ref, o_ref,
               state_ref, cs_ref, dtc_ref, cst_ref, dtt_ref, rheads, gp):
    L = SSM_CHUNK
    R = rheads
    xw = rheads * SSM_HEAD_DIM
    n = SSM_STATE
    g = pg * gp + gi
    xsl = slice(gi * xw, (gi + 1) * xw)
    nsl = slice(gi * n, (gi + 1) * n)

    xs = xs_ref[:, xsl].astype(f32)
    bm16 = b_ref[:, nsl]
    cm16 = c_ref[:, nsl]

    row0 = pl.multiple_of(g * rheads, rheads)
    cs_t = cst_ref[pl.ds(row0, rheads), :]
    dt_t = dtt_ref[pl.ds(row0, rheads), :]
    cs_c = cs_ref[g]
    dt_c = dtc_ref[g]
    lane = lax.broadcasted_iota(jnp.int32, (L, LANES), 1)

    hi, mid, lo = _split_bf16(cs_c, 3)
    packed = jnp.where(lane < R, hi, jnp.where(lane < 2 * R, pltpu.roll(mid, R, 1), pltpu.roll(lo, 2 * R, 1)))
    seg_col = jnp.dot(packed.astype(bf16), segoh_ref[...], preferred_element_type=f32)

    cb = lax.dot_general(cm16, bm16, (((1,), (1,)), ((), ())),
                         preferred_element_type=f32)
    l_idx = lax.broadcasted_iota(jnp.int32, (L, L), 0)
    s_idx = lax.broadcasted_iota(jnp.int32, (L, L), 1)
    causal = l_idx >= s_idx
    lo_half = lane < SSM_HEAD_DIM

    y_parts = []
    for j in range(rheads // 2):
        xj = xs[:, j * LANES:(j + 1) * LANES]
        x_lo = jnp.where(lo_half, xj, 0.0).astype(bf16)
        x_hi = jnp.where(lo_half, 0.0, xj).astype(bf16)
        yj = None
        for r, xr in ((2 * j, x_lo), (2 * j + 1, x_hi)):
            seg = seg_col[:, r * LANES:(r + 1) * LANES] - cs_t[r:r + 1, :]
            decay = jnp.exp(jnp.where(causal, seg, -jnp.inf))
            mr = (cb * decay * dt_t[r:r + 1, :]).astype(bf16)
            part = jnp.dot(mr, xr, preferred_element_type=f32)
            yj = part if yj is None else yj + part
        y_parts.append(yj)
    y_diag = jnp.concatenate(y_parts, axis=1)

    cs_last = cs_c[L - 1:L, :]
    e_hi, e_lo = _split_bf16(jnp.exp(cs_c), 2)
    w_hi, w_lo = _split_bf16(jnp.exp(cs_last - cs_c) * dt_c, 2)
    packed2 = jnp.where(lane < R, e_hi,
                        jnp.where(lane < 2 * R, pltpu.roll(e_lo, R, 1),
                                  jnp.where(lane < 3 * R, pltpu.roll(w_hi, 2 * R, 1),
                                            pltpu.roll(w_lo, 3 * R, 1))))
    spread = jnp.dot(packed2.astype(bf16), expoh_ref[...], preferred_element_type=f32)
    e_cs = spread[:, :xw]
    wgt = spread[:, xw:]

    prev = state_ref[gi, pg]
    y_off = jnp.dot(cm16, prev.astype(bf16), preferred_element_type=f32) * e_cs
    new = jnp.dot(bm16.astype(f32).T.astype(bf16), (xs * wgt).astype(bf16), preferred_element_type=f32)
    state_ref[gi, pg] = prev * e_cs[L - 1:L, :] + new

    y = y_diag + y_off + xs * dskip_ref[:, xsl]
    y = y * z_ref[:, xsl].astype(f32)
    ms = jnp.mean(y * y, axis=-1, keepdims=True)
    o_ref[:, xsl] = (y * lax.rsqrt(ms + EPS) * nw_ref[:, xsl]).astype(o_ref.dtype)


def _ssd(zproj, xconv, bc, dt_raw, dt_bias, a_log, d_skip, norm_w, *, batch, seq, d_inner, groups, gp=8):
    t = zproj.shape[0]
    L = SSM_CHUNK
    nc = seq // L
    heads = d_inner // SSM_HEAD_DIM
    rheads = heads // groups
    assert rheads % 2 == 0 and 4 * rheads <= LANES
    gp = math.gcd(gp, groups)
    xw = rheads * SSM_HEAD_DIM
    n = SSM_STATE
    bw, bn = gp * xw, gp * n
    assert d_inner % bw == 0
    b_blk0 = 0
    c_blk0 = groups // gp
    row = lambda b, c, g: b * nc + c
    pad = lambda a: jnp.pad(a.reshape(1, -1).astype(f32), ((0, 0), (0, LANES - heads)))
    const2 = lambda shape: pl.BlockSpec(shape, lambda b, c, g: (0, 0))
    seg_oh, exp_oh, upper = _ssd_constants(rheads)
    kern = functools.partial(_ssd_kernel, groups=groups, rheads=rheads, gp=gp)
    return pl.pallas_call(
        kern,
        grid=(batch, nc, groups // gp),
        in_specs=[pl.BlockSpec((L, LANES), lambda b, c, g: (row(b, c, g), 0)),
                  const2((1, LANES)),
                  const2((1, LANES)),
                  pl.BlockSpec((L, bw), lambda b, c, g: (row(b, c, g), g)),
                  pl.BlockSpec((L, bw), lambda b, c, g: (row(b, c, g), g)),
                  pl.BlockSpec((L, bn), lambda b, c, g: (row(b, c, g), b_blk0 + g)),
                  pl.BlockSpec((L, bn), lambda b, c, g: (row(b, c, g), c_blk0 + g)),
                  pl.BlockSpec((1, bw), lambda b, c, g: (0, g)),
                  pl.BlockSpec((1, bw), lambda b, c, g: (0, g)),
                  const2((LANES, rheads * LANES)),
                  const2((LANES, 2 * xw)),
                  const2((L, L))],
        out_specs=pl.BlockSpec((L, bw), lambda b, c, g: (row(b, c, g), g)),
        out_shape=jax.ShapeDtypeStruct((t, d_inner), bf16),
        scratch_shapes=[pltpu.VMEM((gp, groups // gp, n, xw), f32),
                        pltpu.VMEM((groups, L, LANES), f32),
                        pltpu.VMEM((groups, L, LANES), f32),
                        pltpu.VMEM((LANES, L), f32),
                        pltpu.VMEM((LANES, L), f32)],
        compiler_params=_params("arbitrary", "arbitrary", "arbitrary"),
        name="ssd_chunk",
    )(dt_raw, pad(dt_bias), pad(a_log), zproj, xconv, bc, bc,
      jnp.repeat(d_skip, SSM_HEAD_DIM).reshape(1, -1), norm_w.reshape(1, -1),
      seg_oh, exp_oh, upper)


def kernel(x, c, positions, norm_w, ada_w, ada_b, attn_w_in, attn_q_norm, attn_k_norm, attn_lambda_q1, attn_lambda_k1, attn_lambda_q2, attn_lambda_k2, attn_subln_w, attn_w_out, ssm_w_in, ssm_conv_w, ssm_conv_b, ssm_dt_bias, ssm_A_log, ssm_D, ssm_norm_w, ssm_w_out):
    batch, seq, d = x.shape
    depth = norm_w.shape[0]
    t = batch * seq
    heads = d // (2 * HEAD_DIM)
    qk_width = heads * 2 * HEAD_DIM
    d_inner = ssm_w_out.shape[1]
    conv_dim = ssm_conv_w.shape[2]
    groups = (conv_dim - d_inner) // (2 * SSM_STATE)
    ssm_heads = ssm_dt_bias.shape[1]
    tm = min(1024, seq)

    c_pad = jnp.pad(c, ((0, 8 - batch), (0, 0)))
    mod = _ada_mod(c_pad, ada_w, ada_b)[:, :batch]
    shift, scale, gate = (mod[:, :, i * d:(i + 1) * d].reshape(depth, batch, 1, d) for i in range(3))

    inv_freq = ROPE_THETA ** (-jnp.arange(0, HEAD_DIM, 2, dtype=f32) / HEAD_DIM)
    freq_row = jnp.tile(inv_freq, LANES // (HEAD_DIM // 2)).reshape(1, LANES)
    cos_t, sin_t = _rope_tables(positions.reshape(t, 1), freq_row)

    main = d_inner + conv_dim
    assert main % LANES == 0 and ssm_heads <= LANES
    w_ssm = jnp.concatenate(
        [ssm_w_in.astype(bf16), jnp.zeros(ssm_w_in.shape[:2] + (LANES - ssm_heads,), bf16)], axis=-1)

    x2 = x.reshape(t, d)
    for layer in range(depth):
        j = layer // 2
        h = _norm_mod(x2, norm_w[layer].reshape(1, d), scale[layer], shift[layer], seq)
        if layer % 2 == 0:
            lambda_init = 0.8 - 0.6 * math.exp(-0.3 * layer)
            rep = qk_width // HEAD_DIM
            qk_w = jnp.concatenate([jnp.tile(attn_q_norm[j], rep) * (HEAD_DIM ** -0.5 * math.log2(math.e)),
                                    jnp.tile(attn_k_norm[j], rep)]).reshape(1, 2 * qk_width)
            tn_a = _tile(qk_width, 1024)
            qk = _proj_qk(h, attn_w_in, j, qk_w, cos_t, sin_t, tm, tn_a)
            vg = _proj(h, attn_w_in, j, attn_w_in.shape[2] - 2 * qk_width, bf16, tm, tn_a,
                       col0=2 * qk_width)
            o = _diff_attn(qk, vg, attn_lambda_q1[j], attn_lambda_k1[j], attn_lambda_q2[j],
                           attn_lambda_k2[j], attn_subln_w[j], batch=batch, seq=seq, heads=heads,
                           lambda_init=lambda_init)
            x2 = _out_proj(o, attn_w_out, j, x2, gate[layer], seq, tm, _tile(d, 1024))
        else:
            tn_c = _tile(math.gcd(d_inner, conv_dim), 1024)
            conv_b = ssm_conv_b[j].reshape(1, -1)
            xconv, zact = _proj_conv(h, w_ssm, j, d_inner, d_inner, ssm_conv_w[j], conv_b, 0, seq, tm, tn_c,
                                     gate_col0=0)
            bc = _proj_conv(h, w_ssm, j, 2 * d_inner, conv_dim - d_inner, ssm_conv_w[j], conv_b, d_inner,
                            seq, tm, tn_c)
            dt_raw = _proj(h, w_ssm, j, LANES, f32, tm, LANES, col0=main)
            y = _ssd(zact, xconv, bc, dt_raw, ssm_dt_bias[j], ssm_A_log[j], ssm_D[j], ssm_norm_w[j],
                     batch=batch, seq=seq, d_inner=d_inner, groups=groups)
            x2 = _out_proj(y, ssm_w_out, j, x2, gate[layer], seq, tm, _tile(d, 512))
    return x2.reshape(batch, seq, d)
```

```python
import functools
import math

import numpy as np
import jax
import jax.numpy as jnp
from jax import lax
from jax.experimental import pallas as pl
from jax.experimental.pallas import tpu as pltpu

EPS = 1e-6
ROPE_THETA = 10000.0
LANES = 128
BF16_ROWS = 16
CONV_HALO = 8
QK_ROW_CHUNKS = 4
CONV_ROW_CHUNKS = 4
HEAD_DIM = 64
SSM_HEAD_DIM = 64
SSM_STATE = 128
SSM_CHUNK = 128
SSM_CONV = 4
VMEM_LIMIT_BYTES = 56 * 1024 * 1024
NEG_BIG = -1e30

f32 = jnp.float32
bf16 = jnp.bfloat16


def _params(*sem):
    return pltpu.CompilerParams(dimension_semantics=sem, vmem_limit_bytes=VMEM_LIMIT_BYTES)


def _silu(x):
    return x * jax.nn.sigmoid(x)


def _tile(n, target):
    if n <= target:
        return n
    best = None
    for cand in range(LANES, target + 1, LANES):
        if n % cand == 0:
            best = cand
    assert best is not None, (n, target)
    return best


def _ada_kernel(c_ref, w_ref, b_ref, o_ref):
    cond = _silu(c_ref[...])
    o_ref[0] = jnp.dot(cond, w_ref[0], preferred_element_type=f32) + b_ref[0]


def _ada_mod(c_pad, ada_w, ada_b, tn=1024):
    depth, d, n = ada_w.shape
    rows = c_pad.shape[0]
    tn = _tile(n, tn)
    return pl.pallas_call(
        _ada_kernel,
        grid=(depth, n // tn),
        in_specs=[pl.BlockSpec((rows, d), lambda l, j: (0, 0)),
                  pl.BlockSpec((1, d, tn), lambda l, j: (l, 0, j)),
                  pl.BlockSpec((1, 1, tn), lambda l, j: (l, 0, j))],
        out_specs=pl.BlockSpec((1, rows, tn), lambda l, j: (l, 0, j)),
        out_shape=jax.ShapeDtypeStruct((depth, rows, n), f32),
        compiler_params=_params("arbitrary", "arbitrary"),
        name="ada_mod",
    )(c_pad, ada_w, ada_b.reshape(depth, 1, n))


def _norm_kernel(x_ref, nw_ref, sc_ref, sh_ref, o_ref):
    x = x_ref[...]
    ms = jnp.mean(x * x, axis=-1, keepdims=True)
    y = x * lax.rsqrt(ms + EPS) * nw_ref[...]
    o_ref[...] = (y * (1.0 + sc_ref[0]) + sh_ref[0]).astype(o_ref.dtype)


def _norm_mod(x2, nw, scale, shift, seq, tm=512):
    t, d = x2.shape
    tm = min(tm, seq)
    tiles_per_seq = seq // tm
    return pl.pallas_call(
        _norm_kernel,
        grid=(t // tm,),
        in_specs=[pl.BlockSpec((tm, d), lambda i: (i, 0)),
                  pl.BlockSpec((1, d), lambda i: (0, 0)),
                  pl.BlockSpec((1, 1, d), lambda i: (i // tiles_per_seq, 0, 0)),
                  pl.BlockSpec((1, 1, d), lambda i: (i // tiles_per_seq, 0, 0))],
        out_specs=pl.BlockSpec((tm, d), lambda i: (i, 0)),
        out_shape=jax.ShapeDtypeStruct((t, d), bf16),
        compiler_params=_params("arbitrary"),
        name="norm_mod",
    )(x2, nw, scale, shift)


def _stage_weight(w_ref, wb_ref):
    @pl.when(pl.program_id(1) == 0)
    def _():
        wb_ref[...] = w_ref[0].astype(bf16)


def _proj_kernel(a_ref, w_ref, o_ref, wb_ref, *, silu):
    _stage_weight(w_ref, wb_ref)
    acc = jnp.dot(a_ref[...], wb_ref[...], preferred_element_type=f32)
    if silu:
        acc = acc / (1.0 + jnp.exp(-acc))
    o_ref[...] = acc.astype(o_ref.dtype)


def _proj(a, w, w_layer, n_out, out_dtype, tm, tn, silu=False, col0=0):
    t, k = a.shape
    assert col0 % tn == 0
    cb0 = col0 // tn
    kern = functools.partial(_proj_kernel, silu=silu)
    return pl.pallas_call(
        kern,
        grid=(n_out // tn, t // tm),
        in_specs=[pl.BlockSpec((tm, k), lambda j, i: (i, 0)),
                  pl.BlockSpec((1, k, tn), lambda j, i: (w_layer, 0, cb0 + j))],
        out_specs=pl.BlockSpec((tm, tn), lambda j, i: (i, j)),
        out_shape=jax.ShapeDtypeStruct((t, n_out), out_dtype),
        scratch_shapes=[pltpu.VMEM((k, tn), bf16)],
        compiler_params=_params("arbitrary", "arbitrary"),
        name="proj",
    )(a, w)


def _proj_qk_kernel(a_ref, w_ref, qw_ref, cos_ref, sin_ref, o_ref, wb_ref):
    _stage_weight(w_ref, wb_ref)
    tm, tn = o_ref.shape
    r_idx = lax.broadcasted_iota(jnp.int32, (LANES, LANES), 0) // HEAD_DIM
    c_idx = lax.broadcasted_iota(jnp.int32, (LANES, LANES), 1) // HEAD_DIM
    group_mean = jnp.where(r_idx == c_idx, 1.0 / HEAD_DIM, 0.0).astype(bf16)
    rc = tm // QK_ROW_CHUNKS
    lane = lax.broadcasted_iota(jnp.int32, (rc, LANES), 1)
    first_half = (lane & (HEAD_DIM // 2)) == 0
    w = wb_ref[...]
    for r in range(QK_ROW_CHUNKS):
        rows = slice(r * rc, (r + 1) * rc)
        acc = jnp.dot(a_ref[rows, :], w, preferred_element_type=f32)
        cos = cos_ref[rows, :]
        sin = sin_ref[rows, :]
        for c in range(tn // LANES):
            sl = slice(c * LANES, (c + 1) * LANES)
            x = acc[:, sl]
            ms = jnp.dot((x * x).astype(bf16), group_mean, preferred_element_type=f32)
            y = x * lax.rsqrt(ms + EPS) * qw_ref[:, sl]
            partner = jnp.where(first_half,
                                pltpu.roll(y, LANES - HEAD_DIM // 2, 1),
                                pltpu.roll(y, HEAD_DIM // 2, 1))
            o_ref[rows, sl] = (y * cos + partner * sin).astype(o_ref.dtype)


def _proj_qk(a, w, w_layer, qk_w, cos_t, sin_t, tm, tn):
    t, k = a.shape
    n = qk_w.shape[1]
    return pl.pallas_call(
        _proj_qk_kernel,
        grid=(n // tn, t // tm),
        in_specs=[pl.BlockSpec((tm, k), lambda j, i: (i, 0)),
                  pl.BlockSpec((1, k, tn), lambda j, i: (w_layer, 0, j)),
                  pl.BlockSpec((1, tn), lambda j, i: (0, j)),
                  pl.BlockSpec((tm, LANES), lambda j, i: (i, 0)),
                  pl.BlockSpec((tm, LANES), lambda j, i: (i, 0))],
        out_specs=pl.BlockSpec((tm, tn), lambda j, i: (i, j)),
        out_shape=jax.ShapeDtypeStruct((t, n), bf16),
        scratch_shapes=[pltpu.VMEM((k, tn), bf16)],
        compiler_params=_params("arbitrary", "arbitrary"),
        name="proj_qk",
    )(a, w, qk_w, cos_t, sin_t)


def _proj_conv_kernel(*refs, tiles_per_seq, with_gate):
    if with_gate:
        a_ref, w_ref, cw_ref, cb_ref, wz_ref, o_ref, oz_ref, halo_ref = refs
    else:
        a_ref, w_ref, cw_ref, cb_ref, o_ref, halo_ref = refs
    i = pl.program_id(1)

    tm, tn = o_ref.shape
    rc = tm // CONV_ROW_CHUNKS
    w = w_ref[0]
    first = (i % tiles_per_seq) == 0
    prev = jnp.where(first, 0.0, halo_ref[...])
    nsl = rc // CONV_HALO
    sub = lax.broadcasted_iota(jnp.int32, (nsl + 1, CONV_HALO, tn), 1)
    w0, w1, w2, w3 = (cw_ref[tap:tap + 1, :] for tap in range(SSM_CONV))

    def shift_rows(slabs, back):
        rot = pltpu.roll(slabs, back, 1)
        above = jnp.concatenate([rot[:1], rot[:-1]], axis=0)
        return jnp.where(sub < back, above, rot)

    for r in range(CONV_ROW_CHUNKS):
        acc = jnp.dot(a_ref[r * rc:(r + 1) * rc, :], w, preferred_element_type=f32)
        slabs = jnp.concatenate([prev, acc], axis=0).reshape(nsl + 1, CONV_HALO, tn)
        x1 = shift_rows(slabs, 1)
        v2 = shift_rows(w1 * slabs + w0 * x1, 2)
        y = (cb_ref[...] + w3 * slabs + w2 * x1 + v2)[1:].reshape(rc, tn)
        o_ref[r * rc:(r + 1) * rc, :] = (y / (1.0 + jnp.exp(-y))).astype(o_ref.dtype)
        prev = acc[rc - CONV_HALO:, :]
        if with_gate:
            z = jnp.dot(a_ref[r * rc:(r + 1) * rc, :], wz_ref[0], preferred_element_type=f32)
            oz_ref[r * rc:(r + 1) * rc, :] = (z / (1.0 + jnp.exp(-z))).astype(oz_ref.dtype)
    halo_ref[...] = prev


def _proj_conv(a, w, w_layer, col0, n, conv_w, conv_b, conv_col0, seq, tm, tn, gate_col0=None):
    t, k = a.shape
    assert col0 % tn == 0 and conv_col0 % tn == 0 and n % tn == 0
    cb0, ccb0 = col0 // tn, conv_col0 // tn
    tiles_per_seq = seq // tm
    with_gate = gate_col0 is not None
    kern = functools.partial(_proj_conv_kernel, tiles_per_seq=tiles_per_seq, with_gate=with_gate)
    in_specs = [pl.BlockSpec((tm, k), lambda j, i: (i, 0)),
                pl.BlockSpec((1, k, tn), lambda j, i: (w_layer, 0, cb0 + j)),
                pl.BlockSpec((SSM_CONV, tn), lambda j, i: (0, ccb0 + j)),
                pl.BlockSpec((1, tn), lambda j, i: (0, ccb0 + j))]
    out_spec = pl.BlockSpec((tm, tn), lambda j, i: (i, j))
    out_shape = jax.ShapeDtypeStruct((t, n), bf16)
    operands = [a, w, conv_w, conv_b]
    if with_gate:
        assert gate_col0 % tn == 0
        gb0 = gate_col0 // tn
        in_specs.append(pl.BlockSpec((1, k, tn), lambda j, i: (w_layer, 0, gb0 + j)))
        operands.append(w)
    return pl.pallas_call(
        kern,
        grid=(n // tn, t // tm),
        in_specs=in_specs,
        out_specs=[out_spec, out_spec] if with_gate else out_spec,
        out_shape=[out_shape, out_shape] if with_gate else out_shape,
        scratch_shapes=[pltpu.VMEM((CONV_HALO, tn), f32)],
        compiler_params=_params("arbitrary", "arbitrary"),
        name="proj_conv",
    )(*operands)


def _outproj_kernel(a_ref, w_ref, x_ref, g_ref, o_ref, wb_ref):
    _stage_weight(w_ref, wb_ref)
    acc = jnp.dot(a_ref[...], wb_ref[...], preferred_element_type=f32)
    o_ref[...] = x_ref[...] + g_ref[0] * acc


def _out_proj(a, w, w_layer, x2, gate, seq, tm, tn):
    t, k = a.shape
    n = w.shape[2]
    tiles_per_seq = seq // tm
    return pl.pallas_call(
        _outproj_kernel,
        grid=(n // tn, t // tm),
        in_specs=[pl.BlockSpec((tm, k), lambda j, i: (i, 0)),
                  pl.BlockSpec((1, k, tn), lambda j, i: (w_layer, 0, j)),
                  pl.BlockSpec((tm, tn), lambda j, i: (i, j)),
                  pl.BlockSpec((1, 1, tn), lambda j, i: (i // tiles_per_seq, 0, j))],
        out_specs=pl.BlockSpec((tm, tn), lambda j, i: (i, j)),
        out_shape=jax.ShapeDtypeStruct((t, n), f32),
        scratch_shapes=[pltpu.VMEM((k, tn), bf16)],
        compiler_params=_params("arbitrary", "arbitrary"),
        name="out_proj",
    )(a, w, x2, gate)


def _rope_table_kernel(pos_ref, freq_ref, cos_ref, sin_ref):
    ang = pos_ref[...].astype(f32) * freq_ref[...]
    lane = lax.broadcasted_iota(jnp.int32, ang.shape, 1)
    first_half = (lane & (HEAD_DIM // 2)) == 0
    s = jnp.sin(ang)
    cos_ref[...] = jnp.cos(ang)
    sin_ref[...] = jnp.where(first_half, -s, s)


def _rope_tables(pos_col, freq_row, tm=1024):
    t = pos_col.shape[0]
    tm = min(tm, t)
    return pl.pallas_call(
        _rope_table_kernel,
        grid=(t // tm,),
        in_specs=[pl.BlockSpec((tm, 1), lambda i: (i, 0)),
                  pl.BlockSpec((1, LANES), lambda i: (0, 0))],
        out_specs=[pl.BlockSpec((tm, LANES), lambda i: (i, 0)),
                   pl.BlockSpec((tm, LANES), lambda i: (i, 0))],
        out_shape=[jax.ShapeDtypeStruct((t, LANES), f32)] * 2,
        compiler_params=_params("arbitrary"),
        name="rope_table",
    )(pos_col, freq_row)


def _attn_kernel(q_ref, k_ref, v_ref, g_ref, lq1_ref, lk1_ref, lq2_ref, lk2_ref, sw_ref, bias_ref, o_ref,
                 vt_ref, qq_ref, s_ref, mx_ref, acc_ref, *, bq, bk, hp, lambda_init):
    seq = q_ref.shape[0]
    nslot = bq // bk
    lam = (jnp.exp(jnp.sum(lq1_ref[...] * lk1_ref[...], axis=-1, keepdims=True))
           - jnp.exp(jnp.sum(lq2_ref[...] * lk2_ref[...], axis=-1, keepdims=True))
           + lambda_init)
    cols = [slice(hh * LANES, (hh + 1) * LANES) for hh in range(hp)]

    ones_rows = jnp.where(lax.broadcasted_iota(jnp.int32, (BF16_ROWS, bk), 0) == 0, 1.0, 0.0).astype(bf16)

    def transpose_v(c, _):
        start = pl.multiple_of(c * bk, bk)
        for hh in range(hp):
            vt_ref[hh, c, 0:LANES, :] = v_ref[pl.ds(start, bk), cols[hh]].astype(f32).T.astype(bf16)
            vt_ref[hh, c, LANES:, :] = ones_rows
        return 0
    lax.fori_loop(0, seq // bk, transpose_v, 0)

    d_row = lax.broadcasted_iota(jnp.int32, (LANES, bq), 0)

    def produce(hh, slot, j):
        start = pl.multiple_of(j * bk, bk)
        st = jnp.dot(k_ref[pl.ds(start, bk), cols[hh]], qq_ref[hh], preferred_element_type=f32)
        s_ref[hh, slot] = st
        mx_ref[hh, slot] = jnp.max(st, axis=0, keepdims=True)

    def consume(hh, slot, j, m, masked):
        st = s_ref[hh, slot]
        if masked:
            st = st + bias_ref[slot]
            blk_max = jnp.max(st, axis=0, keepdims=True)
        else:
            blk_max = mx_ref[hh, slot]
        m_new = jnp.maximum(m, blk_max)
        alpha = jnp.exp2(m - m_new)
        p = jnp.exp2(st - m_new)
        acc_ref[hh] = alpha * acc_ref[hh] + jnp.dot(vt_ref[hh, j], p.astype(bf16),
                                                    preferred_element_type=f32)
        return m_new

    def consume_diagonal(hh, j, m):
        half = bq // 2
        late = (slice(half, bq), slice(bq + half, 2 * bq))
        vt = vt_ref[hh, j]
        st = s_ref[hh, 0, 0:half, :] + bias_ref[0, 0:half, :]
        m1 = jnp.maximum(m, jnp.max(st, axis=0, keepdims=True))
        p = jnp.exp2(st - m1)
        acc = jnp.exp2(m - m1) * acc_ref[hh] + jnp.dot(vt[:, 0:half], p.astype(bf16),
                                                       preferred_element_type=f32)
        st = jnp.concatenate([s_ref[hh, 0, half:, l] + bias_ref[0, half:, l] for l in late], axis=1)
        m1_late = jnp.concatenate([m1[:, l] for l in late], axis=1)
        m2 = jnp.maximum(m1_late, jnp.max(st, axis=0, keepdims=True))
        p = jnp.exp2(st - m2)
        acc_late = (jnp.exp2(m1_late - m2) * jnp.concatenate([acc[:, l] for l in late], axis=1)
                    + jnp.dot(vt[:, half:], p.astype(bf16), preferred_element_type=f32))
        return jnp.concatenate([acc[:, 0:half], acc_late[:, 0:half],
                                acc[:, bq:bq + half], acc_late[:, half:]], axis=1)

    nq = seq // bq

    def start_q_block(hh, i):
        q_start = pl.multiple_of(i * bq, bq)
        qt = q_ref[pl.ds(q_start, bq), cols[hh]].astype(f32).T
        qq_ref[hh] = jnp.concatenate([jnp.where(d_row < HEAD_DIM, qt, 0.0),
                                      jnp.where(d_row >= HEAD_DIM, qt, 0.0)], axis=1).astype(bf16)
        acc_ref[hh] = jnp.zeros(acc_ref.shape[1:], f32)
        for slot in range(nslot):
            produce(hh, slot, slot)

    for hh in range(hp):
        start_q_block(hh, 0)

    def q_block(i, _):
        q_start = pl.multiple_of(i * bq, bq)

        def body(jj, carry):
            out = []
            for hh in range(hp):
                m = carry[hh]
                for slot in range(nslot):
                    m = consume(hh, slot, jj * nslot + slot, m, False)
                    produce(hh, slot, (jj + 1) * nslot + slot)
                out.append(m)
            return tuple(out)

        init = tuple(jnp.full((1, 2 * bq), NEG_BIG, f32) for _ in range(hp))
        carry = lax.fori_loop(0, i, body, init)

        for hh in range(hp):
            m = carry[hh]
            if nslot == 1:
                acc = consume_diagonal(hh, i, m)
            else:
                for slot in range(nslot):
                    m = consume(hh, slot, i * nslot + slot, m, True)
                acc = acc_ref[hh]
            ot = acc[:LANES] / acc[LANES:LANES + 1]
            odt = ot[:, :bq] - lam * ot[:, bq:]
            ms = jnp.mean(odt * odt, axis=0, keepdims=True)
            y = (odt * lax.rsqrt(ms + EPS)).T * (sw_ref[...] * (1.0 - lambda_init))
            g = g_ref[pl.ds(q_start, bq), cols[hh]].astype(f32)
            o_ref[pl.ds(q_start, bq), cols[hh]] = (y * _silu(g)).astype(o_ref.dtype)
            start_q_block(hh, jnp.minimum(i + 1, nq - 1))
        return 0

    lax.fori_loop(0, nq, q_block, 0)


def _diff_attn(qk, vg, lq1, lk1, lq2, lk2, subln_w, *, batch, seq, heads, lambda_init,
               bq=512, bk=512, hp=4):
    t = qk.shape[0]
    bq = min(bq, seq)
    bk = min(bk, bq)
    hp = math.gcd(hp, heads)
    nslot = bq // bk
    w = hp * LANES
    ng = heads // hp
    key = np.arange(bq)[:, None]
    qry = np.arange(bq)[None, :]
    tri = np.where(key <= qry, 0.0, NEG_BIG).astype(np.float32)
    bias = jnp.asarray(np.concatenate([tri, tri], axis=1).reshape(nslot, bk, 2 * bq))
    vec = lambda a: a.reshape(1, -1).astype(f32)
    small = pl.BlockSpec((1, HEAD_DIM), lambda b, h: (0, 0))
    single = pl.Buffered(1)
    kern = functools.partial(_attn_kernel, bq=bq, bk=bk, hp=hp, lambda_init=lambda_init)
    return pl.pallas_call(
        kern,
        grid=(batch, ng),
        in_specs=[pl.BlockSpec((seq, w), lambda b, h: (b, h), pipeline_mode=single),
                  pl.BlockSpec((seq, w), lambda b, h: (b, ng + h)),
                  pl.BlockSpec((seq, w), lambda b, h: (b, h)),
                  pl.BlockSpec((seq, w), lambda b, h: (b, ng + h), pipeline_mode=single),
                  small, small, small, small,
                  pl.BlockSpec((1, LANES), lambda b, h: (0, 0)),
                  pl.BlockSpec((nslot, bk, 2 * bq), lambda b, h: (0, 0, 0), pipeline_mode=single)],
        out_specs=pl.BlockSpec((seq, w), lambda b, h: (b, h)),
        out_shape=jax.ShapeDtypeStruct((t, heads * LANES), bf16),
        scratch_shapes=[pltpu.VMEM((hp, seq // bk, LANES + BF16_ROWS, bk), bf16),
                        pltpu.VMEM((hp, LANES, 2 * bq), bf16),
                        pltpu.VMEM((hp, nslot, bk, 2 * bq), f32),
                        pltpu.VMEM((hp, nslot, 1, 2 * bq), f32),
                        pltpu.VMEM((hp, LANES + BF16_ROWS, 2 * bq), f32)],
        compiler_params=_params("arbitrary", "arbitrary"),
        name="diff_attn",
    )(qk, qk, vg, vg, vec(lq1), vec(lk1), vec(lq2), vec(lk2), vec(subln_w), bias)


def _ssd_constants(rheads):
    L, P = SSM_CHUNK, SSM_HEAD_DIM
    k = np.arange(LANES)[:, None]
    col = np.arange(rheads * LANES)[None, :]
    seg = (k < 3 * rheads) & (k % rheads == col // LANES)
    col2 = np.arange(2 * rheads * P)[None, :]
    half = rheads * P
    exp = np.where(col2 < half,
                   (k < 2 * rheads) & (k % rheads == col2 // P),
                   (k >= 2 * rheads) & (k < 4 * rheads) & (k % rheads == (col2 - half) // P))
    upper = (np.arange(L)[:, None] <= np.arange(L)[None, :])
    return jnp.asarray(seg, dtype=bf16), jnp.asarray(exp, dtype=bf16), jnp.asarray(upper, dtype=bf16)


def _split_bf16(a, pieces):
    out, rem = [], a
    for _ in range(pieces):
        p = rem.astype(bf16).astype(f32)
        out.append(p)
        rem = rem - p
    return out


def _ssd_kernel(dt_ref, dtb_ref, alog_ref, z_ref, xs_ref, b_ref, c_ref, dskip_ref, nw_ref,
                segoh_ref, expoh_ref, upper_ref,
                o_ref,
                state_ref, cs_ref, dtc_ref, cst_ref, dtt_ref,
                *, groups, rheads, gp):
    c = pl.program_id(1)
    pg = pl.program_id(2)

    @pl.when(c == 0)
    def _():
        state_ref[:, pg] = jnp.zeros((gp,) + state_ref.shape[2:], f32)

    def bookkeeping():
        raw = dt_ref[...] + dtb_ref[...]
        dt = jnp.maximum(raw, 0.0) + jnp.log(1.0 + jnp.exp(-jnp.abs(raw)))
        da_t = (dt * (-jnp.exp(alog_ref[...]))).T
        cs_t = sum(jnp.dot(piece.astype(bf16), upper_ref[...], preferred_element_type=f32)
                   for piece in _split_bf16(da_t, 3))
        cst_ref[...] = cs_t
        dtt_ref[...] = dt.T
        cs = cs_t.T
        for gg in range(groups):
            shift = (LANES - gg * rheads) % LANES
            cs_ref[gg] = pltpu.roll(cs, shift, 1) if shift else cs
            dtc_ref[gg] = pltpu.roll(dt, shift, 1) if shift else dt

    if gp == groups:
        bookkeeping()
    else:
        pl.when(pg == 0)(bookkeeping)

    for gi in range(gp):
        _ssd_group(gi, pg, z_ref, xs_ref, b_ref, c_ref, dskip_ref, nw_ref, segoh_ref, expoh_ref,
                   o_ref, state_ref, cs_ref, dtc_ref, cst_ref, dtt_ref, rheads, gp)


def _ssd_group(gi, pg, z_ref, xs_ref, b_ref, c_ref, dskip_ref, nw_ref, segoh_ref, expoh_ref, o_ref,
               state_ref, cs_ref, dtc_ref, cst_ref, dtt_ref, rheads, gp):
    L = SSM_CHUNK
    R = rheads
    xw = rheads * SSM_HEAD_DIM
    n = SSM_STATE
    g = pg * gp + gi
    xsl = slice(gi * xw, (gi + 1) * xw)
    nsl = slice(gi * n, (gi + 1) * n)

    xs = xs_ref[:, xsl].astype(f32)
    bm16 = b_ref[:, nsl]
    cm16 = c_ref[:, nsl]

    row0 = pl.multiple_of(g * rheads, rheads)
    cs_t = cst_ref[pl.ds(row0, rheads), :]
    dt_t = dtt_ref[pl.ds(row0, rheads), :]
    cs_c = cs_ref[g]
    dt_c = dtc_ref[g]
    lane = lax.broadcasted_iota(jnp.int32, (L, LANES), 1)

    hi, mid, lo = _split_bf16(cs_c, 3)
    packed = jnp.where(lane < R, hi, jnp.where(lane < 2 * R, pltpu.roll(mid, R, 1), pltpu.roll(lo, 2 * R, 1)))
    seg_col = jnp.dot(packed.astype(bf16), segoh_ref[...], preferred_element_type=f32)

    cb = lax.dot_general(cm16, bm16, (((1,), (1,)), ((), ())),
                         preferred_element_type=f32)
    l_idx = lax.broadcasted_iota(jnp.int32, (L, L), 0)
    s_idx = lax.broadcasted_iota(jnp.int32, (L, L), 1)
    causal = l_idx >= s_idx
    lo_half = lane < SSM_HEAD_DIM

    y_parts = []
    for j in range(rheads // 2):
        xj = xs[:, j * LANES:(j + 1) * LANES]
        x_lo = jnp.where(lo_half, xj, 0.0).astype(bf16)
        x_hi = jnp.where(lo_half, 0.0, xj).astype(bf16)
        yj = None
        for r, xr in ((2 * j, x_lo), (2 * j + 1, x_hi)):
            seg = seg_col[:, r * LANES:(r + 1) * LANES] - cs_t[r:r + 1, :]
            decay = jnp.exp(jnp.where(causal, seg, -jnp.inf))
            mr = (cb * decay * dt_t[r:r + 1, :]).astype(bf16)
            part = jnp.dot(mr, xr, preferred_element_type=f32)
            yj = part if yj is None else yj + part
        y_parts.append(yj)
    y_diag = jnp.concatenate(y_parts, axis=1)

    cs_last = cs_c[L - 1:L, :]
    e_hi, e_lo = _split_bf16(jnp.exp(cs_c), 2)
    w_hi, w_lo = _split_bf16(jnp.exp(cs_last - cs_c) * dt_c, 2)
    packed2 = jnp.where(lane < R, e_hi,
                        jnp.where(lane < 2 * R, pltpu.roll(e_lo, R, 1),
                                  jnp.where(lane < 3 * R, pltpu.roll(w_hi, 2 * R, 1),
                                            pltpu.roll(w_lo, 3 * R, 1))))
    spread = jnp.dot(packed2.astype(bf16), expoh_ref[...], preferred_element_type=f32)
    e_cs = spread[:, :xw]
    wgt = spread[:, xw:]

    prev = state_ref[gi, pg]
    y_off = jnp.dot(cm16, prev.astype(bf16), preferred_element_type=f32) * e_cs
    new = jnp.dot(bm16.astype(f32).T.astype(bf16), (xs * wgt).astype(bf16), preferred_element_type=f32)
    state_ref[gi, pg] = prev * e_cs[L - 1:L, :] + new

    y = y_diag + y_off + xs * dskip_ref[:, xsl]
    y = y * z_ref[:, xsl].astype(f32)
    ms = jnp.mean(y * y, axis=-1, keepdims=True)
    o_ref[:, xsl] = (y * lax.rsqrt(ms + EPS) * nw_ref[:, xsl]).astype(o_ref.dtype)


def _ssd(zproj, xconv, bc, dt_raw, dt_bias, a_log, d_skip, norm_w, *, batch, seq, d_inner, groups, gp=8):
    t = zproj.shape[0]
    L = SSM_CHUNK
    nc = seq // L
    heads = d_inner // SSM_HEAD_DIM
    rheads = heads // groups
    assert rheads % 2 == 0 and 4 * rheads <= LANES
    gp = math.gcd(gp, groups)
    xw = rheads * SSM_HEAD_DIM
    n = SSM_STATE
    bw, bn = gp * xw, gp * n
    assert d_inner % bw == 0
    b_blk0 = 0
    c_blk0 = groups // gp
    row = lambda b, c, g: b * nc + c
    pad = lambda a: jnp.pad(a.reshape(1, -1).astype(f32), ((0, 0), (0, LANES - heads)))
    const2 = lambda shape: pl.BlockSpec(shape, lambda b, c, g: (0, 0))
    seg_oh, exp_oh, upper = _ssd_constants(rheads)
    kern = functools.partial(_ssd_kernel, groups=groups, rheads=rheads, gp=gp)
    return pl.pallas_call(
        kern,
        grid=(batch, nc, groups // gp),
        in_specs=[pl.BlockSpec((L, LANES), lambda b, c, g: (row(b, c, g), 0)),
                  const2((1, LANES)),
                  const2((1, LANES)),
                  pl.BlockSpec((L, bw), lambda b, c, g: (row(b, c, g), g)),
                  pl.BlockSpec((L, bw), lambda b, c, g: (row(b, c, g), g)),
                  pl.BlockSpec((L, bn), lambda b, c, g: (row(b, c, g), b_blk0 + g)),
                  pl.BlockSpec((L, bn), lambda b, c, g: (row(b, c, g), c_blk0 + g)),
                  pl.BlockSpec((1, bw), lambda b, c, g: (0, g)),
                  pl.BlockSpec((1, bw), lambda b, c, g: (0, g)),
                  const2((LANES, rheads * LANES)),
                  const2((LANES, 2 * xw)),
                  const2((L, L))],
        out_specs=pl.BlockSpec((L, bw), lambda b, c, g: (row(b, c, g), g)),
        out_shape=jax.ShapeDtypeStruct((t, d_inner), bf16),
        scratch_shapes=[pltpu.VMEM((gp, groups // gp, n, xw), f32),
                        pltpu.VMEM((groups, L, LANES), f32),
                        pltpu.VMEM((groups, L, LANES), f32),
                        pltpu.VMEM((LANES, L), f32),
                        pltpu.VMEM((LANES, L), f32)],
        compiler_params=_params("arbitrary", "arbitrary", "arbitrary"),
        name="ssd_chunk",
    )(dt_raw, pad(dt_bias), pad(a_log), zproj, xconv, bc, bc,
      jnp.repeat(d_skip, SSM_HEAD_DIM).reshape(1, -1), norm_w.reshape(1, -1),
      seg_oh, exp_oh, upper)


def kernel(x, c, positions, norm_w, ada_w, ada_b, attn_w_in, attn_q_norm, attn_k_norm, attn_lambda_q1, attn_lambda_k1, attn_lambda_q2, attn_lambda_k2, attn_subln_w, attn_w_out, ssm_w_in, ssm_conv_w, ssm_conv_b, ssm_dt_bias, ssm_A_log, ssm_D, ssm_norm_w, ssm_w_out):
    batch, seq, d = x.shape
    depth = norm_w.shape[0]
    t = batch * seq
    heads = d // (2 * HEAD_DIM)
    qk_width = heads * 2 * HEAD_DIM
    d_inner = ssm_w_out.shape[1]
    conv_dim = ssm_conv_w.shape[2]
    groups = (conv_dim - d_inner) // (2 * SSM_STATE)
    ssm_heads = ssm_dt_bias.shape[1]
    tm = min(1024, seq)

    c_pad = jnp.pad(c, ((0, 8 - batch), (0, 0)))
    mod = _ada_mod(c_pad, ada_w, ada_b)[:, :batch]
    shift, scale, gate = (mod[:, :, i * d:(i + 1) * d].reshape(depth, batch, 1, d) for i in range(3))

    inv_freq = ROPE_THETA ** (-jnp.arange(0, HEAD_DIM, 2, dtype=f32) / HEAD_DIM)
    freq_row = jnp.tile(inv_freq, LANES // (HEAD_DIM // 2)).reshape(1, LANES)
    cos_t, sin_t = _rope_tables(positions.reshape(t, 1), freq_row)

    main = d_inner + conv_dim
    assert main % LANES == 0 and ssm_heads <= LANES
    w_ssm = jnp.concatenate(
        [ssm_w_in.astype(bf16), jnp.zeros(ssm_w_in.shape[:2] + (LANES - ssm_heads,), bf16)], axis=-1)

    x2 = x.reshape(t, d)
    for layer in range(depth):
        j = layer // 2
        h = _norm_mod(x2, norm_w[layer].reshape(1, d), scale[layer], shift[layer], seq)
        if layer % 2 == 0:
            lambda_init = 0.8 - 0.6 * math.exp(-0.3 * layer)
            rep = qk_width // HEAD_DIM
            qk_w = jnp.concatenate([jnp.tile(attn_q_norm[j], rep) * (HEAD_DIM ** -0.5 * math.log2(math.e)),
                                    jnp.tile(attn_k_norm[j], rep)]).reshape(1, 2 * qk_width)
            tn_a = _tile(qk_width, 1024)
            qk = _proj_qk(h, attn_w_in, j, qk_w, cos_t, sin_t, tm, tn_a)
            vg = _proj(h, attn_w_in, j, attn_w_in.shape[2] - 2 * qk_width, bf16, tm, tn_a,
                       col0=2 * qk_width)
            o = _diff_attn(qk, vg, attn_lambda_q1[j], attn_lambda_k1[j], attn_lambda_q2[j],
                           attn_lambda_k2[j], attn_subln_w[j], batch=batch, seq=seq, heads=heads,
                           lambda_init=lambda_init)
            x2 = _out_proj(o, attn_w_out, j, x2, gate[layer], seq, tm, _tile(d, 1024))
        else:
            tn_c = _tile(math.gcd(d_inner, conv_dim), 1024)
            conv_b = ssm_conv_b[j].reshape(1, -1)
            xconv, zact = _proj_conv(h, w_ssm, j, d_inner, d_inner, ssm_conv_w[j], conv_b, 0, seq, tm, tn_c,
                                     gate_col0=0)
            bc = _proj_conv(h, w_ssm, j, 2 * d_inner, conv_dim - d_inner, ssm_conv_w[j], conv_b, d_inner,
                            seq, tm, tn_c)
            dt_raw = _proj(h, w_ssm, j, LANES, f32, tm, LANES, col0=main)
            y = _ssd(zact, xconv, bc, dt_raw, ssm_dt_bias[j], ssm_A_log[j], ssm_D[j], ssm_norm_w[j],
                     batch=batch, seq=seq, d_inner=d_inner, groups=groups)
            x2 = _out_proj(y, ssm_w_out, j, x2, gate[layer], seq, tm, _tile(d, 512))
    return x2.reshape(batch, seq, d)
```

```python
import functools
import math

import numpy as np
import jax
import jax.numpy as jnp
from jax import lax
from jax.experimental import pallas as pl
from jax.experimental.pallas import tpu as pltpu

EPS = 1e-6
ROPE_THETA = 10000.0
LANES = 128
BF16_ROWS = 16
CONV_HALO = 8
QK_ROW_CHUNKS = 4
CONV_ROW_CHUNKS = 4
HEAD_DIM = 64
SSM_HEAD_DIM = 64
SSM_STATE = 128
SSM_CHUNK = 128
SSM_CONV = 4
VMEM_LIMIT_BYTES = 56 * 1024 * 1024
NEG_BIG = -1e30

f32 = jnp.float32
bf16 = jnp.bfloat16


def _params(*sem):
    return pltpu.CompilerParams(dimension_semantics=sem, vmem_limit_bytes=VMEM_LIMIT_BYTES)


def _silu(x):
    return x * jax.nn.sigmoid(x)


def _tile(n, target):
    if n <= target:
        return n
    best = None
    for cand in range(LANES, target + 1, LANES):
        if n % cand == 0:
            best = cand
    assert best is not None, (n, target)
    return best


def _ada_kernel(c_ref, w_ref, b_ref, o_ref):
    cond = _silu(c_ref[...])
    o_ref[0] = jnp.dot(cond, w_ref[0], preferred_element_type=f32) + b_ref[0]


def _ada_mod(c_pad, ada_w, ada_b, tn=1024):
    depth, d, n = ada_w.shape
    rows = c_pad.shape[0]
    tn = _tile(n, tn)
    return pl.pallas_call(
        _ada_kernel,
        grid=(depth, n // tn),
        in_specs=[pl.BlockSpec((rows, d), lambda l, j: (0, 0)),
                  pl.BlockSpec((1, d, tn), lambda l, j: (l, 0, j)),
                  pl.BlockSpec((1, 1, tn), lambda l, j: (l, 0, j))],
        out_specs=pl.BlockSpec((1, rows, tn), lambda l, j: (l, 0, j)),
        out_shape=jax.ShapeDtypeStruct((depth, rows, n), f32),
        compiler_params=_params("arbitrary", "arbitrary"),
        name="ada_mod",
    )(c_pad, ada_w, ada_b.reshape(depth, 1, n))


def _norm_kernel(x_ref, nw_ref, sc_ref, sh_ref, o_ref):
    x = x_ref[...]
    ms = jnp.mean(x * x, axis=-1, keepdims=True)
    y = x * lax.rsqrt(ms + EPS) * nw_ref[...]
    o_ref[...] = (y * (1.0 + sc_ref[0]) + sh_ref[0]).astype(o_ref.dtype)


def _norm_mod(x2, nw, scale, shift, seq, tm=1024):
    t, d = x2.shape
    tm = min(tm, seq)
    tiles_per_seq = seq // tm
    return pl.pallas_call(
        _norm_kernel,
        grid=(t // tm,),
        in_specs=[pl.BlockSpec((tm, d), lambda i: (i, 0)),
                  pl.BlockSpec((1, d), lambda i: (0, 0)),
                  pl.BlockSpec((1, 1, d), lambda i: (i // tiles_per_seq, 0, 0)),
                  pl.BlockSpec((1, 1, d), lambda i: (i // tiles_per_seq, 0, 0))],
        out_specs=pl.BlockSpec((tm, d), lambda i: (i, 0)),
        out_shape=jax.ShapeDtypeStruct((t, d), bf16),
        compiler_params=_params("arbitrary"),
        name="norm_mod",
    )(x2, nw, scale, shift)


def _stage_weight(w_ref, wb_ref):
    @pl.when(pl.program_id(1) == 0)
    def _():
        wb_ref[...] = w_ref[0].astype(bf16)


def _proj_kernel(a_ref, w_ref, o_ref, wb_ref, *, silu):
    _stage_weight(w_ref, wb_ref)
    acc = jnp.dot(a_ref[...], wb_ref[...], preferred_element_type=f32)
    if silu:
        acc = acc / (1.0 + jnp.exp(-acc))
    o_ref[...] = acc.astype(o_ref.dtype)


def _proj(a, w, w_layer, n_out, out_dtype, tm, tn, silu=False, col0=0):
    t, k = a.shape
    assert col0 % tn == 0
    cb0 = col0 // tn
    kern = functools.partial(_proj_kernel, silu=silu)
    return pl.pallas_call(
        kern,
        grid=(n_out // tn, t // tm),
        in_specs=[pl.BlockSpec((tm, k), lambda j, i: (i, 0)),
                  pl.BlockSpec((1, k, tn), lambda j, i: (w_layer, 0, cb0 + j))],
        out_specs=pl.BlockSpec((tm, tn), lambda j, i: (i, j)),
        out_shape=jax.ShapeDtypeStruct((t, n_out), out_dtype),
        scratch_shapes=[pltpu.VMEM((k, tn), bf16)],
        compiler_params=_params("arbitrary", "arbitrary"),
        name="proj",
    )(a, w)


def _proj_qk_kernel(a_ref, w_ref, qw_ref, cos_ref, sin_ref, o_ref, wb_ref):
    _stage_weight(w_ref, wb_ref)
    tm, tn = o_ref.shape
    r_idx = lax.broadcasted_iota(jnp.int32, (LANES, LANES), 0) // HEAD_DIM
    c_idx = lax.broadcasted_iota(jnp.int32, (LANES, LANES), 1) // HEAD_DIM
    group_mean = jnp.where(r_idx == c_idx, 1.0 / HEAD_DIM, 0.0).astype(bf16)
    rc = tm // QK_ROW_CHUNKS
    lane = lax.broadcasted_iota(jnp.int32, (rc, LANES), 1)
    first_half = (lane & (HEAD_DIM // 2)) == 0
    w = wb_ref[...]
    for r in range(QK_ROW_CHUNKS):
        rows = slice(r * rc, (r + 1) * rc)
        acc = jnp.dot(a_ref[rows, :], w, preferred_element_type=f32)
        cos = cos_ref[rows, :]
        sin = sin_ref[rows, :]
        for c in range(tn // LANES):
            sl = slice(c * LANES, (c + 1) * LANES)
            x = acc[:, sl]
            ms = jnp.dot((x * x).astype(bf16), group_mean, preferred_element_type=f32)
            y = x * lax.rsqrt(ms + EPS) * qw_ref[:, sl]
            partner = jnp.where(first_half,
                                pltpu.roll(y, LANES - HEAD_DIM // 2, 1),
                                pltpu.roll(y, HEAD_DIM // 2, 1))
            o_ref[rows, sl] = (y * cos + partner * sin).astype(o_ref.dtype)


def _proj_qk(a, w, w_layer, qk_w, cos_t, sin_t, tm, tn):
    t, k = a.shape
    n = qk_w.shape[1]
    return pl.pallas_call(
        _proj_qk_kernel,
        grid=(n // tn, t // tm),
        in_specs=[pl.BlockSpec((tm, k), lambda j, i: (i, 0)),
                  pl.BlockSpec((1, k, tn), lambda j, i: (w_layer, 0, j)),
                  pl.BlockSpec((1, tn), lambda j, i: (0, j)),
                  pl.BlockSpec((tm, LANES), lambda j, i: (i, 0)),
                  pl.BlockSpec((tm, LANES), lambda j, i: (i, 0))],
        out_specs=pl.BlockSpec((tm, tn), lambda j, i: (i, j)),
        out_shape=jax.ShapeDtypeStruct((t, n), bf16),
        scratch_shapes=[pltpu.VMEM((k, tn), bf16)],
        compiler_params=_params("arbitrary", "arbitrary"),
        name="proj_qk",
    )(a, w, qk_w, cos_t, sin_t)


def _proj_conv_kernel(*refs, tiles_per_seq, with_gate):
    if with_gate:
        a_ref, w_ref, cw_ref, cb_ref, wz_ref, o_ref, oz_ref, halo_ref = refs
    else:
        a_ref, w_ref, cw_ref, cb_ref, o_ref, halo_ref = refs
    i = pl.program_id(1)

    tm, tn = o_ref.shape
    rc = tm // CONV_ROW_CHUNKS
    w = w_ref[0]
    first = (i % tiles_per_seq) == 0
    prev = jnp.where(first, 0.0, halo_ref[...])
    nsl = rc // CONV_HALO
    sub = lax.broadcasted_iota(jnp.int32, (nsl + 1, CONV_HALO, tn), 1)
    w0, w1, w2, w3 = (cw_ref[tap:tap + 1, :] for tap in range(SSM_CONV))

    def shift_rows(slabs, back):
        rot = pltpu.roll(slabs, back, 1)
        above = jnp.concatenate([rot[:1], rot[:-1]], axis=0)
        return jnp.where(sub < back, above, rot)

    for r in range(CONV_ROW_CHUNKS):
        acc = jnp.dot(a_ref[r * rc:(r + 1) * rc, :], w, preferred_element_type=f32)
        slabs = jnp.concatenate([prev, acc], axis=0).reshape(nsl + 1, CONV_HALO, tn)
        x1 = shift_rows(slabs, 1)
        v2 = shift_rows(w1 * slabs + w0 * x1, 2)
        y = (cb_ref[...] + w3 * slabs + w2 * x1 + v2)[1:].reshape(rc, tn)
        o_ref[r * rc:(r + 1) * rc, :] = (y / (1.0 + jnp.exp(-y))).astype(o_ref.dtype)
        prev = acc[rc - CONV_HALO:, :]
        if with_gate:
            z = jnp.dot(a_ref[r * rc:(r + 1) * rc, :], wz_ref[0], preferred_element_type=f32)
            oz_ref[r * rc:(r + 1) * rc, :] = (z / (1.0 + jnp.exp(-z))).astype(oz_ref.dtype)
    halo_ref[...] = prev


def _proj_conv(a, w, w_layer, col0, n, conv_w, conv_b, conv_col0, seq, tm, tn, gate_col0=None):
    t, k = a.shape
    assert col0 % tn == 0 and conv_col0 % tn == 0 and n % tn == 0
    cb0, ccb0 = col0 // tn, conv_col0 // tn
    tiles_per_seq = seq // tm
    with_gate = gate_col0 is not None
    kern = functools.partial(_proj_conv_kernel, tiles_per_seq=tiles_per_seq, with_gate=with_gate)
    in_specs = [pl.BlockSpec((tm, k), lambda j, i: (i, 0)),
                pl.BlockSpec((1, k, tn), lambda j, i: (w_layer, 0, cb0 + j)),
                pl.BlockSpec((SSM_CONV, tn), lambda j, i: (0, ccb0 + j)),
                pl.BlockSpec((1, tn), lambda j, i: (0, ccb0 + j))]
    out_spec = pl.BlockSpec((tm, tn), lambda j, i: (i, j))
    out_shape = jax.ShapeDtypeStruct((t, n), bf16)
    operands = [a, w, conv_w, conv_b]
    if with_gate:
        assert gate_col0 % tn == 0
        gb0 = gate_col0 // tn
        in_specs.append(pl.BlockSpec((1, k, tn), lambda j, i: (w_layer, 0, gb0 + j)))
        operands.append(w)
    return pl.pallas_call(
        kern,
        grid=(n // tn, t // tm),
        in_specs=in_specs,
        out_specs=[out_spec, out_spec] if with_gate else out_spec,
        out_shape=[out_shape, out_shape] if with_gate else out_shape,
        scratch_shapes=[pltpu.VMEM((CONV_HALO, tn), f32)],
        compiler_params=_params("arbitrary", "arbitrary"),
        name="proj_conv",
    )(*operands)


def _outproj_kernel(a_ref, w_ref, x_ref, g_ref, o_ref, wb_ref):
    _stage_weight(w_ref, wb_ref)
    acc = jnp.dot(a_ref[...], wb_ref[...], preferred_element_type=f32)
    o_ref[...] = x_ref[...] + g_ref[0] * acc


def _out_proj(a, w, w_layer, x2, gate, seq, tm, tn):
    t, k = a.shape
    n = w.shape[2]
    tiles_per_seq = seq // tm
    return pl.pallas_call(
        _outproj_kernel,
        grid=(n // tn, t // tm),
        in_specs=[pl.BlockSpec((tm, k), lambda j, i: (i, 0)),
                  pl.BlockSpec((1, k, tn), lambda j, i: (w_layer, 0, j)),
                  pl.BlockSpec((tm, tn), lambda j, i: (i, j)),
                  pl.BlockSpec((1, 1, tn), lambda j, i: (i // tiles_per_seq, 0, j))],
        out_specs=pl.BlockSpec((tm, tn), lambda j, i: (i, j)),
        out_shape=jax.ShapeDtypeStruct((t, n), f32),
        scratch_shapes=[pltpu.VMEM((k, tn), bf16)],
        compiler_params=_params("arbitrary", "arbitrary"),
        name="out_proj",
    )(a, w, x2, gate)


def _rope_table_kernel(pos_ref, freq_ref, cos_ref, sin_ref):
    ang = pos_ref[...].astype(f32) * freq_ref[...]
    lane = lax.broadcasted_iota(jnp.int32, ang.shape, 1)
    first_half = (lane & (HEAD_DIM // 2)) == 0
    s = jnp.sin(ang)
    cos_ref[...] = jnp.cos(ang)
    sin_ref[...] = jnp.where(first_half, -s, s)


def _rope_tables(pos_col, freq_row, tm=1024):
    t = pos_col.shape[0]
    tm = min(tm, t)
    return pl.pallas_call(
        _rope_table_kernel,
        grid=(t // tm,),
        in_specs=[pl.BlockSpec((tm, 1), lambda i: (i, 0)),
                  pl.BlockSpec((1, LANES), lambda i: (0, 0))],
        out_specs=[pl.BlockSpec((tm, LANES), lambda i: (i, 0)),
                   pl.BlockSpec((tm, LANES), lambda i: (i, 0))],
        out_shape=[jax.ShapeDtypeStruct((t, LANES), f32)] * 2,
        compiler_params=_params("arbitrary"),
        name="rope_table",
    )(pos_col, freq_row)


def _attn_kernel(q_ref, k_ref, v_ref, g_ref, lq1_ref, lk1_ref, lq2_ref, lk2_ref, sw_ref, bias_ref, o_ref,
                 vt_ref, qq_ref, s_ref, mx_ref, acc_ref, *, bq, bk, hp, lambda_init):
    seq = q_ref.shape[0]
    nslot = bq // bk
    lam = (jnp.exp(jnp.sum(lq1_ref[...] * lk1_ref[...], axis=-1, keepdims=True))
           - jnp.exp(jnp.sum(lq2_ref[...] * lk2_ref[...], axis=-1, keepdims=True))
           + lambda_init)
    cols = [slice(hh * LANES, (hh + 1) * LANES) for hh in range(hp)]

    ones_rows = jnp.where(lax.broadcasted_iota(jnp.int32, (BF16_ROWS, bk), 0) == 0, 1.0, 0.0).astype(bf16)

    def transpose_v(c, _):
        start = pl.multiple_of(c * bk, bk)
        for hh in range(hp):
            vt_ref[hh, c, 0:LANES, :] = v_ref[pl.ds(start, bk), cols[hh]].astype(f32).T.astype(bf16)
            vt_ref[hh, c, LANES:, :] = ones_rows
        return 0
    lax.fori_loop(0, seq // bk, transpose_v, 0)

    d_row = lax.broadcasted_iota(jnp.int32, (LANES, bq), 0)

    def produce(hh, slot, j):
        start = pl.multiple_of(j * bk, bk)
        st = jnp.dot(k_ref[pl.ds(start, bk), cols[hh]], qq_ref[hh], preferred_element_type=f32)
        s_ref[hh, slot] = st
        mx_ref[hh, slot] = jnp.max(st, axis=0, keepdims=True)

    def consume(hh, slot, j, m, masked):
        st = s_ref[hh, slot]
        if masked:
            st = st + bias_ref[slot]
            blk_max = jnp.max(st, axis=0, keepdims=True)
        else:
            blk_max = mx_ref[hh, slot]
        m_new = jnp.maximum(m, blk_max)
        alpha = jnp.exp2(m - m_new)
        p = jnp.exp2(st - m_new)
        acc_ref[hh] = alpha * acc_ref[hh] + jnp.dot(vt_ref[hh, j], p.astype(bf16),
                                                    preferred_element_type=f32)
        return m_new

    def consume_diagonal(hh, j, m):
        half = bq // 2
        late = (slice(half, bq), slice(bq + half, 2 * bq))
        vt = vt_ref[hh, j]
        st = s_ref[hh, 0, 0:half, :] + bias_ref[0, 0:half, :]
        m1 = jnp.maximum(m, jnp.max(st, axis=0, keepdims=True))
        p = jnp.exp2(st - m1)
        acc = jnp.exp2(m - m1) * acc_ref[hh] + jnp.dot(vt[:, 0:half], p.astype(bf16),
                                                       preferred_element_type=f32)
        st = jnp.concatenate([s_ref[hh, 0, half:, l] + bias_ref[0, half:, l] for l in late], axis=1)
        m1_late = jnp.concatenate([m1[:, l] for l in late], axis=1)
        m2 = jnp.maximum(m1_late, jnp.max(st, axis=0, keepdims=True))
        p = jnp.exp2(st - m2)
        acc_late = (jnp.exp2(m1_late - m2) * jnp.concatenate([acc[:, l] for l in late], axis=1)
                    + jnp.dot(vt[:, half:], p.astype(bf16), preferred_element_type=f32))
        return jnp.concatenate([acc[:, 0:half], acc_late[:, 0:half],
                                acc[:, bq:bq + half], acc_late[:, half:]], axis=1)

    nq = seq // bq

    def start_q_block(hh, i):
        q_start = pl.multiple_of(i * bq, bq)
        qt = q_ref[pl.ds(q_start, bq), cols[hh]].astype(f32).T
        qq_ref[hh] = jnp.concatenate([jnp.where(d_row < HEAD_DIM, qt, 0.0),
                                      jnp.where(d_row >= HEAD_DIM, qt, 0.0)], axis=1).astype(bf16)
        acc_ref[hh] = jnp.zeros(acc_ref.shape[1:], f32)
        for slot in range(nslot):
            produce(hh, slot, slot)

    for hh in range(hp):
        start_q_block(hh, 0)

    def q_block(i, _):
        q_start = pl.multiple_of(i * bq, bq)

        def body(jj, carry):
            out = []
            for hh in range(hp):
                m = carry[hh]
                for slot in range(nslot):
                    m = consume(hh, slot, jj * nslot + slot, m, False)
                    produce(hh, slot, (jj + 1) * nslot + slot)
                out.append(m)
            return tuple(out)

        init = tuple(jnp.full((1, 2 * bq), NEG_BIG, f32) for _ in range(hp))
        carry = lax.fori_loop(0, i, body, init)

        for hh in range(hp):
            m = carry[hh]
            if nslot == 1:
                acc = consume_diagonal(hh, i, m)
            else:
                for slot in range(nslot):
                    m = consume(hh, slot, i * nslot + slot, m, True)
                acc = acc_ref[hh]
            inv_l = 1.0 / acc[LANES:LANES + 1]
            odt = acc[:LANES, :bq] * inv_l[:, :bq] - acc[:LANES, bq:] * (lam * inv_l[:, bq:])
            ms = jnp.mean(odt * odt, axis=0, keepdims=True)
            y = (odt * lax.rsqrt(ms + EPS)).T * (sw_ref[...] * (1.0 - lambda_init))
            g = g_ref[pl.ds(q_start, bq), cols[hh]].astype(f32)
            o_ref[pl.ds(q_start, bq), cols[hh]] = (y * _silu(g)).astype(o_ref.dtype)
            start_q_block(hh, jnp.minimum(i + 1, nq - 1))
        return 0

    lax.fori_loop(0, nq, q_block, 0)


def _diff_attn(qk, vg, lq1, lk1, lq2, lk2, subln_w, *, batch, seq, heads, lambda_init,
               bq=512, bk=512, hp=4):
    t = qk.shape[0]
    bq = min(bq, seq)
    bk = min(bk, bq)
    hp = math.gcd(hp, heads)
    nslot = bq // bk
    w = hp * LANES
    ng = heads // hp
    key = np.arange(bq)[:, None]
    qry = np.arange(bq)[None, :]
    tri = np.where(key <= qry, 0.0, NEG_BIG).astype(np.float32)
    bias = jnp.asarray(np.concatenate([tri, tri], axis=1).reshape(nslot, bk, 2 * bq))
    vec = lambda a: a.reshape(1, -1).astype(f32)
    small = pl.BlockSpec((1, HEAD_DIM), lambda b, h: (0, 0))
    single = pl.Buffered(1)
    kern = functools.partial(_attn_kernel, bq=bq, bk=bk, hp=hp, lambda_init=lambda_init)
    return pl.pallas_call(
        kern,
        grid=(batch, ng),
        in_specs=[pl.BlockSpec((seq, w), lambda b, h: (b, h), pipeline_mode=single),
                  pl.BlockSpec((seq, w), lambda b, h: (b, ng + h)),
                  pl.BlockSpec((seq, w), lambda b, h: (b, h)),
                  pl.BlockSpec((seq, w), lambda b, h: (b, ng + h), pipeline_mode=single),
                  small, small, small, small,
                  pl.BlockSpec((1, LANES), lambda b, h: (0, 0)),
                  pl.BlockSpec((nslot, bk, 2 * bq), lambda b, h: (0, 0, 0), pipeline_mode=single)],
        out_specs=pl.BlockSpec((seq, w), lambda b, h: (b, h)),
        out_shape=jax.ShapeDtypeStruct((t, heads * LANES), bf16),
        scratch_shapes=[pltpu.VMEM((hp, seq // bk, LANES + BF16_ROWS, bk), bf16),
                        pltpu.VMEM((hp, LANES, 2 * bq), bf16),
                        pltpu.VMEM((hp, nslot, bk, 2 * bq), f32),
                        pltpu.VMEM((hp, nslot, 1, 2 * bq), f32),
                        pltpu.VMEM((hp, LANES + BF16_ROWS, 2 * bq), f32)],
        compiler_params=_params("arbitrary", "arbitrary"),
        name="diff_attn",
    )(qk, qk, vg, vg, vec(lq1), vec(lk1), vec(lq2), vec(lk2), vec(subln_w), bias)


def _ssd_constants(rheads):
    L, P = SSM_CHUNK, SSM_HEAD_DIM
    k = np.arange(LANES)[:, None]
    col = np.arange(rheads * LANES)[None, :]
    seg = (k < 3 * rheads) & (k % rheads == col // LANES)
    col2 = np.arange(2 * rheads * P)[None, :]
    half = rheads * P
    exp = np.where(col2 < half,
                   (k < 2 * rheads) & (k % rheads == col2 // P),
                   (k >= 2 * rheads) & (k < 4 * rheads) & (k % rheads == (col2 - half) // P))
    upper = (np.arange(L)[:, None] <= np.arange(L)[None, :])
    return jnp.asarray(seg, dtype=bf16), jnp.asarray(exp, dtype=bf16), jnp.asarray(upper, dtype=bf16)


def _split_bf16(a, pieces):
    out, rem = [], a
    for _ in range(pieces):
        p = rem.astype(bf16).astype(f32)
        out.append(p)
        rem = rem - p
    return out


def _ssd_kernel(dt_ref, dtb_ref, alog_ref, z_ref, xs_ref, b_ref, c_ref, dskip_ref, nw_ref,
                segoh_ref, expoh_ref, upper_ref,
                o_ref,
                state_ref, cs_ref, dtc_ref, cst_ref, dtt_ref,
                *, groups, rheads, gp):
    c = pl.program_id(1)
    pg = pl.program_id(2)

    @pl.when(c == 0)
    def _():
        state_ref[:, pg] = jnp.zeros((gp,) + state_ref.shape[2:], f32)

    def bookkeeping():
        raw = dt_ref[...] + dtb_ref[...]
        dt = jnp.maximum(raw, 0.0) + jnp.log(1.0 + jnp.exp(-jnp.abs(raw)))
        da_t = (dt * (-jnp.exp(alog_ref[...]))).T
        cs_t = sum(jnp.dot(piece.astype(bf16), upper_ref[...], preferred_element_type=f32)
                   for piece in _split_bf16(da_t, 3))
        cst_ref[...] = cs_t
        dtt_ref[...] = dt.T
        cs = cs_t.T
        for gg in range(groups):
            shift = (LANES - gg * rheads) % LANES
            cs_ref[gg] = pltpu.roll(cs, shift, 1) if shift else cs
            dtc_ref[gg] = pltpu.roll(dt, shift, 1) if shift else dt

    if gp == groups:
        bookkeeping()
    else:
        pl.when(pg == 0)(bookkeeping)

    for gi in range(gp):
        _ssd_group(gi, pg, z_ref, xs_ref, b_ref, c_ref, dskip_ref, nw_ref, segoh_ref, expoh_ref,
                   o_ref, state_ref, cs_ref, dtc_ref, cst_ref, dtt_ref, rheads, gp)


def _ssd_group(gi, pg, z_ref, xs_ref, b_ref, c_ref, dskip_ref, nw_ref, segoh_ref, expoh_ref, o_ref,
               state_ref, cs_ref, dtc_ref, cst_ref, dtt_ref, rheads, gp):
    L = SSM_CHUNK
    R = rheads
    xw = rheads * SSM_HEAD_DIM
    n = SSM_STATE
    g = pg * gp + gi
    xsl = slice(gi * xw, (gi + 1) * xw)
    nsl = slice(gi * n, (gi + 1) * n)

    xs = xs_ref[:, xsl].astype(f32)
    bm16 = b_ref[:, nsl]
    cm16 = c_ref[:, nsl]

    row0 = pl.multiple_of(g * rheads, rheads)
    cs_t = cst_ref[pl.ds(row0, rheads), :]
    dt_t = dtt_ref[pl.ds(row0, rheads), :]
    cs_c = cs_ref[g]
    dt_c = dtc_ref[g]
    lane = lax.broadcasted_iota(jnp.int32, (L, LANES), 1)

    hi, mid, lo = _split_bf16(cs_c, 3)
    packed = jnp.where(lane < R, hi, jnp.where(lane < 2 * R, pltpu.roll(mid, R, 1), pltpu.roll(lo, 2 * R, 1)))
    seg_col = jnp.dot(packed.astype(bf16), segoh_ref[...], preferred_element_type=f32)

    cb = lax.dot_general(cm16, bm16, (((1,), (1,)), ((), ())),
                         preferred_element_type=f32)
    l_idx = lax.broadcasted_iota(jnp.int32, (L, L), 0)
    s_idx = lax.broadcasted_iota(jnp.int32, (L, L), 1)
    causal = l_idx >= s_idx
    lo_half = lane < SSM_HEAD_DIM

    y_parts = []
    for j in range(rheads // 2):
        xj = xs[:, j * LANES:(j + 1) * LANES]
        x_lo = jnp.where(lo_half, xj, 0.0).astype(bf16)
        x_hi = jnp.where(lo_half, 0.0, xj).astype(bf16)
        yj = None
        for r, xr in ((2 * j, x_lo), (2 * j + 1, x_hi)):
            seg = seg_col[:, r * LANES:(r + 1) * LANES] - cs_t[r:r + 1, :]
            decay = jnp.exp(jnp.where(causal, seg, -jnp.inf))
            mr = (cb * decay * dt_t[r:r + 1, :]).astype(bf16)
            part = jnp.dot(mr, xr, preferred_element_type=f32)
            yj = part if yj is None else yj + part
        y_parts.append(yj)
    y_diag = jnp.concatenate(y_parts, axis=1)

    cs_last = cs_c[L - 1:L, :]
    e_hi, e_lo = _split_bf16(jnp.exp(cs_c), 2)
    w_hi, w_lo = _split_bf16(jnp.exp(cs_last - cs_c) * dt_c, 2)
    packed2 = jnp.where(lane < R, e_hi,
                        jnp.where(lane < 2 * R, pltpu.roll(e_lo, R, 1),
                                  jnp.where(lane < 3 * R, pltpu.roll(w_hi, 2 * R, 1),
                                            pltpu.roll(w_lo, 3 * R, 1))))
    spread = jnp.dot(packed2.astype(bf16), expoh_ref[...], preferred_element_type=f32)
    e_cs = spread[:, :xw]
    wgt = spread[:, xw:]

    prev = state_ref[gi, pg]
    y_off = jnp.dot(cm16, prev.astype(bf16), preferred_element_type=f32) * e_cs
    new = jnp.dot(bm16.astype(f32).T.astype(bf16), (xs * wgt).astype(bf16), preferred_element_type=f32)
    state_ref[gi, pg] = prev * e_cs[L - 1:L, :] + new

    y = y_diag + y_off + xs * dskip_ref[:, xsl]
    y = y * z_ref[:, xsl].astype(f32)
    ms = jnp.mean(y * y, axis=-1, keepdims=True)
    o_ref[:, xsl] = (y * lax.rsqrt(ms + EPS) * nw_ref[:, xsl]).astype(o_ref.dtype)


def _ssd(zproj, xconv, bc, dt_raw, dt_bias, a_log, d_skip, norm_w, *, batch, seq, d_inner, groups, gp=8):
    t = zproj.shape[0]
    L = SSM_CHUNK
    nc = seq // L
    heads = d_inner // SSM_HEAD_DIM
    rheads = heads // groups
    assert rheads % 2 == 0 and 4 * rheads <= LANES
    gp = math.gcd(gp, groups)
    xw = rheads * SSM_HEAD_DIM
    n = SSM_STATE
    bw, bn = gp * xw, gp * n
    assert d_inner % bw == 0
    b_blk0 = 0
    c_blk0 = groups // gp
    row = lambda b, c, g: b * nc + c
    pad = lambda a: jnp.pad(a.reshape(1, -1).astype(f32), ((0, 0), (0, LANES - heads)))
    const2 = lambda shape: pl.BlockSpec(shape, lambda b, c, g: (0, 0))
    seg_oh, exp_oh, upper = _ssd_constants(rheads)
    kern = functools.partial(_ssd_kernel, groups=groups, rheads=rheads, gp=gp)
    return pl.pallas_call(
        kern,
        grid=(batch, nc, groups // gp),
        in_specs=[pl.BlockSpec((L, LANES), lambda b, c, g: (row(b, c, g), 0)),
                  const2((1, LANES)),
                  const2((1, LANES)),
                  pl.BlockSpec((L, bw), lambda b, c, g: (row(b, c, g), g)),
                  pl.BlockSpec((L, bw), lambda b, c, g: (row(b, c, g), g)),
                  pl.BlockSpec((L, bn), lambda b, c, g: (row(b, c, g), b_blk0 + g)),
                  pl.BlockSpec((L, bn), lambda b, c, g: (row(b, c, g), c_blk0 + g)),
                  pl.BlockSpec((1, bw), lambda b, c, g: (0, g)),
                  pl.BlockSpec((1, bw), lambda b, c, g: (0, g)),
                  const2((LANES, rheads * LANES)),
                  const2((LANES, 2 * xw)),
                  const2((L, L))],
        out_specs=pl.BlockSpec((L, bw), lambda b, c, g: (row(b, c, g), g)),
        out_shape=jax.ShapeDtypeStruct((t, d_inner), bf16),
        scratch_shapes=[pltpu.VMEM((gp, groups // gp, n, xw), f32),
                        pltpu.VMEM((groups, L, LANES), f32),
                        pltpu.VMEM((groups, L, LANES), f32),
                        pltpu.VMEM((LANES, L), f32),
                        pltpu.VMEM((LANES, L), f32)],
        compiler_params=_params("arbitrary", "arbitrary", "arbitrary"),
        name="ssd_chunk",
    )(dt_raw, pad(dt_bias), pad(a_log), zproj, xconv, bc, bc,
      jnp.repeat(d_skip, SSM_HEAD_DIM).reshape(1, -1), norm_w.reshape(1, -1),
      seg_oh, exp_oh, upper)


def kernel(x, c, positions, norm_w, ada_w, ada_b, attn_w_in, attn_q_norm, attn_k_norm, attn_lambda_q1, attn_lambda_k1, attn_lambda_q2, attn_lambda_k2, attn_subln_w, attn_w_out, ssm_w_in, ssm_conv_w, ssm_conv_b, ssm_dt_bias, ssm_A_log, ssm_D, ssm_norm_w, ssm_w_out):
    batch, seq, d = x.shape
    depth = norm_w.shape[0]
    t = batch * seq
    heads = d // (2 * HEAD_DIM)
    qk_width = heads * 2 * HEAD_DIM
    d_inner = ssm_w_out.shape[1]
    conv_dim = ssm_conv_w.shape[2]
    groups = (conv_dim - d_inner) // (2 * SSM_STATE)
    ssm_heads = ssm_dt_bias.shape[1]
    tm = min(1024, seq)

    c_pad = jnp.pad(c, ((0, 8 - batch), (0, 0)))
    mod = _ada_mod(c_pad, ada_w, ada_b)[:, :batch]
    shift, scale, gate = (mod[:, :, i * d:(i + 1) * d].reshape(depth, batch, 1, d) for i in range(3))

    inv_freq = ROPE_THETA ** (-jnp.arange(0, HEAD_DIM, 2, dtype=f32) / HEAD_DIM)
    freq_row = jnp.tile(inv_freq, LANES // (HEAD_DIM // 2)).reshape(1, LANES)
    cos_t, sin_t = _rope_tables(positions.reshape(t, 1), freq_row)

    main = d_inner + conv_dim
    assert main % LANES == 0 and ssm_heads <= LANES
    w_ssm = ssm_w_in.astype(bf16)
    w_ssm_dt = jnp.pad(w_ssm[:, :, main:], ((0, 0), (0, 0), (0, LANES - ssm_heads)))

    x2 = x.reshape(t, d)
    for layer in range(depth):
        j = layer // 2
        h = _norm_mod(x2, norm_w[layer].reshape(1, d), scale[layer], shift[layer], seq)
        if layer % 2 == 0:
            lambda_init = 0.8 - 0.6 * math.exp(-0.3 * layer)
            rep = qk_width // HEAD_DIM
            qk_w = jnp.concatenate([jnp.tile(attn_q_norm[j], rep) * (HEAD_DIM ** -0.5 * math.log2(math.e)),
                                    jnp.tile(attn_k_norm[j], rep)]).reshape(1, 2 * qk_width)
            tn_a = _tile(qk_width, 1024)
            qk = _proj_qk(h, attn_w_in, j, qk_w, cos_t, sin_t, tm, tn_a)
            vg = _proj(h, attn_w_in, j, attn_w_in.shape[2] - 2 * qk_width, bf16, tm, tn_a,
                       col0=2 * qk_width)
            o = _diff_attn(qk, vg, attn_lambda_q1[j], attn_lambda_k1[j], attn_lambda_q2[j],
                           attn_lambda_k2[j], attn_subln_w[j], batch=batch, seq=seq, heads=heads,
                           lambda_init=lambda_init)
            x2 = _out_proj(o, attn_w_out, j, x2, gate[layer], seq, tm, _tile(d, 1024))
        else:
            tn_c = _tile(math.gcd(d_inner, conv_dim), 1024)
            conv_b = ssm_conv_b[j].reshape(1, -1)
            xconv, zact = _proj_conv(h, w_ssm, j, d_inner, d_inner, ssm_conv_w[j], conv_b, 0, seq, tm, tn_c,
                                     gate_col0=0)
            bc = _proj_conv(h, w_ssm, j, 2 * d_inner, conv_dim - d_inner, ssm_conv_w[j], conv_b, d_inner,
                            seq, tm, tn_c)
            dt_raw = _proj(h, w_ssm_dt, j, LANES, f32, tm, LANES)
            y = _ssd(zact, xconv, bc, dt_raw, ssm_dt_bias[j], ssm_A_log[j], ssm_D[j], ssm_norm_w[j],
                     batch=batch, seq=seq, d_inner=d_inner, groups=groups)
            x2 = _out_proj(y, ssm_w_out, j, x2, gate[layer], seq, tm, _tile(d, 512))
    return x2.reshape(batch, seq, d)
```

```python
import functools
import math

import numpy as np
import jax
import jax.numpy as jnp
from jax import lax
from jax.experimental import pallas as pl
from jax.experimental.pallas import tpu as pltpu

EPS = 1e-6
ROPE_THETA = 10000.0
LANES = 128
BF16_ROWS = 16
CONV_HALO = 8
QK_ROW_CHUNKS = 4
CONV_ROW_CHUNKS = 4
HEAD_DIM = 64
SSM_HEAD_DIM = 64
SSM_STATE = 128
SSM_CHUNK = 128
SSM_CONV = 4
VMEM_LIMIT_BYTES = 56 * 1024 * 1024
NEG_BIG = -1e30

f32 = jnp.float32
bf16 = jnp.bfloat16


def _params(*sem):
    return pltpu.CompilerParams(dimension_semantics=sem, vmem_limit_bytes=VMEM_LIMIT_BYTES)


def _silu(x):
    return x * jax.nn.sigmoid(x)


def _tile(n, target):
    if n <= target:
        return n
    best = None
    for cand in range(LANES, target + 1, LANES):
        if n % cand == 0:
            best = cand
    assert best is not None, (n, target)
    return best


def _ada_kernel(c_ref, w_ref, b_ref, o_ref):
    cond = _silu(c_ref[...])
    o_ref[0] = jnp.dot(cond, w_ref[0], preferred_element_type=f32) + b_ref[0]


def _ada_mod(c_pad, ada_w, ada_b, tn=1024):
    depth, d, n = ada_w.shape
    rows = c_pad.shape[0]
    tn = _tile(n, tn)
    return pl.pallas_call(
        _ada_kernel,
        grid=(depth, n // tn),
        in_specs=[pl.BlockSpec((rows, d), lambda l, j: (0, 0)),
                  pl.BlockSpec((1, d, tn), lambda l, j: (l, 0, j)),
                  pl.BlockSpec((1, 1, tn), lambda l, j: (l, 0, j))],
        out_specs=pl.BlockSpec((1, rows, tn), lambda l, j: (l, 0, j)),
        out_shape=jax.ShapeDtypeStruct((depth, rows, n), f32),
        compiler_params=_params("arbitrary", "arbitrary"),
        name="ada_mod",
    )(c_pad, ada_w, ada_b.reshape(depth, 1, n))


def _norm_kernel(x_ref, nw_ref, sc_ref, sh_ref, o_ref):
    x = x_ref[...]
    ms = jnp.mean(x * x, axis=-1, keepdims=True)
    y = x * lax.rsqrt(ms + EPS) * nw_ref[...]
    o_ref[...] = (y * (1.0 + sc_ref[0]) + sh_ref[0]).astype(o_ref.dtype)


def _norm_mod(x2, nw, scale, shift, seq, tm=1024):
    t, d = x2.shape
    tm = min(tm, seq)
    tiles_per_seq = seq // tm
    return pl.pallas_call(
        _norm_kernel,
        grid=(t // tm,),
        in_specs=[pl.BlockSpec((tm, d), lambda i: (i, 0)),
                  pl.BlockSpec((1, d), lambda i: (0, 0)),
                  pl.BlockSpec((1, 1, d), lambda i: (i // tiles_per_seq, 0, 0)),
                  pl.BlockSpec((1, 1, d), lambda i: (i // tiles_per_seq, 0, 0))],
        out_specs=pl.BlockSpec((tm, d), lambda i: (i, 0)),
        out_shape=jax.ShapeDtypeStruct((t, d), bf16),
        compiler_params=_params("arbitrary"),
        name="norm_mod",
    )(x2, nw, scale, shift)


def _stage_weight(w_ref, wb_ref):
    @pl.when(pl.program_id(1) == 0)
    def _():
        wb_ref[...] = w_ref[0].astype(bf16)


def _proj_kernel(a_ref, w_ref, o_ref, wb_ref, *, silu):
    _stage_weight(w_ref, wb_ref)
    acc = jnp.dot(a_ref[...], wb_ref[...], preferred_element_type=f32)
    if silu:
        acc = acc / (1.0 + jnp.exp(-acc))
    o_ref[...] = acc.astype(o_ref.dtype)


def _proj(a, w, w_layer, n_out, out_dtype, tm, tn, silu=False, col0=0):
    t, k = a.shape
    assert col0 % tn == 0
    cb0 = col0 // tn
    kern = functools.partial(_proj_kernel, silu=silu)
    return pl.pallas_call(
        kern,
        grid=(n_out // tn, t // tm),
        in_specs=[pl.BlockSpec((tm, k), lambda j, i: (i, 0)),
                  pl.BlockSpec((1, k, tn), lambda j, i: (w_layer, 0, cb0 + j))],
        out_specs=pl.BlockSpec((tm, tn), lambda j, i: (i, j)),
        out_shape=jax.ShapeDtypeStruct((t, n_out), out_dtype),
        scratch_shapes=[pltpu.VMEM((k, tn), bf16)],
        compiler_params=_params("arbitrary", "arbitrary"),
        name="proj",
    )(a, w)


def _proj_qk_kernel(a_ref, w_ref, qw_ref, cos_ref, sin_ref, o_ref, wb_ref):
    _stage_weight(w_ref, wb_ref)
    tm, tn = o_ref.shape
    r_idx = lax.broadcasted_iota(jnp.int32, (LANES, LANES), 0) // HEAD_DIM
    c_idx = lax.broadcasted_iota(jnp.int32, (LANES, LANES), 1) // HEAD_DIM
    group_mean = jnp.where(r_idx == c_idx, 1.0 / HEAD_DIM, 0.0).astype(bf16)
    rc = tm // QK_ROW_CHUNKS
    lane = lax.broadcasted_iota(jnp.int32, (rc, LANES), 1)
    first_half = (lane & (HEAD_DIM // 2)) == 0
    w = wb_ref[...]
    for r in range(QK_ROW_CHUNKS):
        rows = slice(r * rc, (r + 1) * rc)
        acc = jnp.dot(a_ref[rows, :], w, preferred_element_type=f32)
        cos = cos_ref[rows, :]
        sin = sin_ref[rows, :]
        for c in range(tn // LANES):
            sl = slice(c * LANES, (c + 1) * LANES)
            x = acc[:, sl]
            ms = jnp.dot((x * x).astype(bf16), group_mean, preferred_element_type=f32)
            y = x * lax.rsqrt(ms + EPS) * qw_ref[:, sl]
            partner = jnp.where(first_half,
                                pltpu.roll(y, LANES - HEAD_DIM // 2, 1),
                                pltpu.roll(y, HEAD_DIM // 2, 1))
            o_ref[rows, sl] = (y * cos + partner * sin).astype(o_ref.dtype)


def _proj_qk(a, w, w_layer, qk_w, cos_t, sin_t, tm, tn):
    t, k = a.shape
    n = qk_w.shape[1]
    return pl.pallas_call(
        _proj_qk_kernel,
        grid=(n // tn, t // tm),
        in_specs=[pl.BlockSpec((tm, k), lambda j, i: (i, 0)),
                  pl.BlockSpec((1, k, tn), lambda j, i: (w_layer, 0, j)),
                  pl.BlockSpec((1, tn), lambda j, i: (0, j)),
                  pl.BlockSpec((tm, LANES), lambda j, i: (i, 0)),
                  pl.BlockSpec((tm, LANES), lambda j, i: (i, 0))],
        out_specs=pl.BlockSpec((tm, tn), lambda j, i: (i, j)),
        out_shape=jax.ShapeDtypeStruct((t, n), bf16),
        scratch_shapes=[pltpu.VMEM((k, tn), bf16)],
        compiler_params=_params("arbitrary", "arbitrary"),
        name="proj_qk",
    )(a, w, qk_w, cos_t, sin_t)


def _proj_conv_kernel(*refs, tiles_per_seq, with_gate):
    if with_gate:
        a_ref, w_ref, cw_ref, cb_ref, wz_ref, o_ref, oz_ref, halo_ref = refs
    else:
        a_ref, w_ref, cw_ref, cb_ref, o_ref, halo_ref = refs
    i = pl.program_id(1)

    tm, tn = o_ref.shape
    rc = tm // CONV_ROW_CHUNKS
    w = w_ref[0]
    first = (i % tiles_per_seq) == 0
    prev = jnp.where(first, 0.0, halo_ref[...])
    nsl = rc // CONV_HALO
    sub = lax.broadcasted_iota(jnp.int32, (nsl + 1, CONV_HALO, tn), 1)
    w0, w1, w2, w3 = (cw_ref[tap:tap + 1, :] for tap in range(SSM_CONV))

    def shift_rows(slabs, back):
        rot = pltpu.roll(slabs, back, 1)
        above = jnp.concatenate([rot[:1], rot[:-1]], axis=0)
        return jnp.where(sub < back, above, rot)

    for r in range(CONV_ROW_CHUNKS):
        acc = jnp.dot(a_ref[r * rc:(r + 1) * rc, :], w, preferred_element_type=f32)
        slabs = jnp.concatenate([prev, acc], axis=0).reshape(nsl + 1, CONV_HALO, tn)
        x1 = shift_rows(slabs, 1)
        v2 = shift_rows(w1 * slabs + w0 * x1, 2)
        y = (cb_ref[...] + w3 * slabs + w2 * x1 + v2)[1:].reshape(rc, tn)
        o_ref[r * rc:(r + 1) * rc, :] = (y / (1.0 + jnp.exp(-y))).astype(o_ref.dtype)
        prev = acc[rc - CONV_HALO:, :]
        if with_gate:
            z = jnp.dot(a_ref[r * rc:(r + 1) * rc, :], wz_ref[0], preferred_element_type=f32)
            oz_ref[r * rc:(r + 1) * rc, :] = (z / (1.0 + jnp.exp(-z))).astype(oz_ref.dtype)
    halo_ref[...] = prev


def _proj_conv(a, w, w_layer, col0, n, conv_w, conv_b, conv_col0, seq, tm, tn, gate_col0=None):
    t, k = a.shape
    assert col0 % tn == 0 and conv_col0 % tn == 0 and n % tn == 0
    cb0, ccb0 = col0 // tn, conv_col0 // tn
    tiles_per_seq = seq // tm
    with_gate = gate_col0 is not None
    kern = functools.partial(_proj_conv_kernel, tiles_per_seq=tiles_per_seq, with_gate=with_gate)
    in_specs = [pl.BlockSpec((tm, k), lambda j, i: (i, 0)),
                pl.BlockSpec((1, k, tn), lambda j, i: (w_layer, 0, cb0 + j)),
                pl.BlockSpec((SSM_CONV, tn), lambda j, i: (0, ccb0 + j)),
                pl.BlockSpec((1, tn), lambda j, i: (0, ccb0 + j))]
    out_spec = pl.BlockSpec((tm, tn), lambda j, i: (i, j))
    out_shape = jax.ShapeDtypeStruct((t, n), bf16)
    operands = [a, w, conv_w, conv_b]
    if with_gate:
        assert gate_col0 % tn == 0
        gb0 = gate_col0 // tn
        in_specs.append(pl.BlockSpec((1, k, tn), lambda j, i: (w_layer, 0, gb0 + j)))
        operands.append(w)
    return pl.pallas_call(
        kern,
        grid=(n // tn, t // tm),
        in_specs=in_specs,
        out_specs=[out_spec, out_spec] if with_gate else out_spec,
        out_shape=[out_shape, out_shape] if with_gate else out_shape,
        scratch_shapes=[pltpu.VMEM((CONV_HALO, tn), f32)],
        compiler_params=_params("arbitrary", "arbitrary"),
        name="proj_conv",
    )(*operands)


def _outproj_kernel(a_ref, w_ref, x_ref, g_ref, o_ref, wb_ref):
    _stage_weight(w_ref, wb_ref)
    acc = jnp.dot(a_ref[...], wb_ref[...], preferred_element_type=f32)
    o_ref[...] = x_ref[...] + g_ref[0] * acc


def _out_proj(a, w, w_layer, x2, gate, seq, tm, tn):
    t, k = a.shape
    n = w.shape[2]
    tiles_per_seq = seq // tm
    return pl.pallas_call(
        _outproj_kernel,
        grid=(n // tn, t // tm),
        in_specs=[pl.BlockSpec((tm, k), lambda j, i: (i, 0)),
                  pl.BlockSpec((1, k, tn), lambda j, i: (w_layer, 0, j)),
                  pl.BlockSpec((tm, tn), lambda j, i: (i, j)),
                  pl.BlockSpec((1, 1, tn), lambda j, i: (i // tiles_per_seq, 0, j))],
        out_specs=pl.BlockSpec((tm, tn), lambda j, i: (i, j)),
        out_shape=jax.ShapeDtypeStruct((t, n), f32),
        scratch_shapes=[pltpu.VMEM((k, tn), bf16)],
        compiler_params=_params("arbitrary", "arbitrary"),
        name="out_proj",
    )(a, w, x2, gate)


def _rope_table_kernel(pos_ref, freq_ref, cos_ref, sin_ref):
    ang = pos_ref[...].astype(f32) * freq_ref[...]
    lane = lax.broadcasted_iota(jnp.int32, ang.shape, 1)
    first_half = (lane & (HEAD_DIM // 2)) == 0
    s = jnp.sin(ang)
    cos_ref[...] = jnp.cos(ang)
    sin_ref[...] = jnp.where(first_half, -s, s)


def _rope_tables(pos_col, freq_row, tm=1024):
    t = pos_col.shape[0]
    tm = min(tm, t)
    return pl.pallas_call(
        _rope_table_kernel,
        grid=(t // tm,),
        in_specs=[pl.BlockSpec((tm, 1), lambda i: (i, 0)),
                  pl.BlockSpec((1, LANES), lambda i: (0, 0))],
        out_specs=[pl.BlockSpec((tm, LANES), lambda i: (i, 0)),
                   pl.BlockSpec((tm, LANES), lambda i: (i, 0))],
        out_shape=[jax.ShapeDtypeStruct((t, LANES), f32)] * 2,
        compiler_params=_params("arbitrary"),
        name="rope_table",
    )(pos_col, freq_row)


def _attn_kernel(q_ref, k_ref, v_ref, g_ref, lq1_ref, lk1_ref, lq2_ref, lk2_ref, sw_ref, bias_ref, o_ref,
                 vt_ref, qq_ref, s_ref, mx_ref, acc_ref, *, bq, bk, hp, lambda_init):
    seq = q_ref.shape[0]
    nslot = bq // bk
    lam = (jnp.exp(jnp.sum(lq1_ref[...] * lk1_ref[...], axis=-1, keepdims=True))
           - jnp.exp(jnp.sum(lq2_ref[...] * lk2_ref[...], axis=-1, keepdims=True))
           + lambda_init)
    cols = [slice(hh * LANES, (hh + 1) * LANES) for hh in range(hp)]

    ones_rows = jnp.where(lax.broadcasted_iota(jnp.int32, (BF16_ROWS, bk), 0) == 0, 1.0, 0.0).astype(bf16)

    def transpose_v(c, _):
        start = pl.multiple_of(c * bk, bk)
        for hh in range(hp):
            vt_ref[hh, c, 0:LANES, :] = v_ref[pl.ds(start, bk), cols[hh]].astype(f32).T.astype(bf16)
            vt_ref[hh, c, LANES:, :] = ones_rows
        return 0
    lax.fori_loop(0, seq // bk, transpose_v, 0)

    d_row = lax.broadcasted_iota(jnp.int32, (LANES, bq), 0)

    def produce(hh, slot, j):
        start = pl.multiple_of(j * bk, bk)
        st = jnp.dot(k_ref[pl.ds(start, bk), cols[hh]], qq_ref[hh], preferred_element_type=f32)
        s_ref[hh, slot] = st
        mx_ref[hh, slot] = jnp.max(st, axis=0, keepdims=True)

    def consume(hh, slot, j, m, masked):
        st = s_ref[hh, slot]
        if masked:
            st = st + bias_ref[slot]
            blk_max = jnp.max(st, axis=0, keepdims=True)
        else:
            blk_max = mx_ref[hh, slot]
        m_new = jnp.maximum(m, blk_max)
        alpha = jnp.exp2(m - m_new)
        p = jnp.exp2(st - m_new)
        acc_ref[hh] = alpha * acc_ref[hh] + jnp.dot(vt_ref[hh, j], p.astype(bf16),
                                                    preferred_element_type=f32)
        return m_new

    def consume_diagonal(hh, j, m):
        half = bq // 2
        late = (slice(half, bq), slice(bq + half, 2 * bq))
        vt = vt_ref[hh, j]
        st = s_ref[hh, 0, 0:half, :] + bias_ref[0, 0:half, :]
        m1 = jnp.maximum(m, jnp.max(st, axis=0, keepdims=True))
        p = jnp.exp2(st - m1)
        acc = jnp.exp2(m - m1) * acc_ref[hh] + jnp.dot(vt[:, 0:half], p.astype(bf16),
                                                       preferred_element_type=f32)
        st = jnp.concatenate([s_ref[hh, 0, half:, l] + bias_ref[0, half:, l] for l in late], axis=1)
        m1_late = jnp.concatenate([m1[:, l] for l in late], axis=1)
        m2 = jnp.maximum(m1_late, jnp.max(st, axis=0, keepdims=True))
        p = jnp.exp2(st - m2)
        acc_late = (jnp.exp2(m1_late - m2) * jnp.concatenate([acc[:, l] for l in late], axis=1)
                    + jnp.dot(vt[:, half:], p.astype(bf16), preferred_element_type=f32))
        return jnp.concatenate([acc[:, 0:half], acc_late[:, 0:half],
                                acc[:, bq:bq + half], acc_late[:, half:]], axis=1)

    nq = seq // bq

    def start_q_block(hh, i):
        q_start = pl.multiple_of(i * bq, bq)
        qt = q_ref[pl.ds(q_start, bq), cols[hh]].astype(f32).T
        qq_ref[hh] = jnp.concatenate([jnp.where(d_row < HEAD_DIM, qt, 0.0),
                                      jnp.where(d_row >= HEAD_DIM, qt, 0.0)], axis=1).astype(bf16)
        acc_ref[hh] = jnp.zeros(acc_ref.shape[1:], f32)
        for slot in range(nslot):
            produce(hh, slot, slot)

    for hh in range(hp):
        start_q_block(hh, 0)

    def q_block(i, _):
        q_start = pl.multiple_of(i * bq, bq)

        def body(jj, carry):
            out = []
            for hh in range(hp):
                m = carry[hh]
                for slot in range(nslot):
                    m = consume(hh, slot, jj * nslot + slot, m, False)
                    produce(hh, slot, (jj + 1) * nslot + slot)
                out.append(m)
            return tuple(out)

        init = tuple(jnp.full((1, 2 * bq), NEG_BIG, f32) for _ in range(hp))
        carry = lax.fori_loop(0, i, body, init)

        for hh in range(hp):
            m = carry[hh]
            if nslot == 1:
                acc = consume_diagonal(hh, i, m)
            else:
                for slot in range(nslot):
                    m = consume(hh, slot, i * nslot + slot, m, True)
                acc = acc_ref[hh]
            inv_l = 1.0 / acc[LANES:LANES + 1]
            odt = acc[:LANES, :bq] * inv_l[:, :bq] - acc[:LANES, bq:] * (lam * inv_l[:, bq:])
            ms = jnp.mean(odt * odt, axis=0, keepdims=True)
            y = (odt * lax.rsqrt(ms + EPS)).T * (sw_ref[...] * (1.0 - lambda_init))
            g = g_ref[pl.ds(q_start, bq), cols[hh]].astype(f32)
            o_ref[pl.ds(q_start, bq), cols[hh]] = (y * _silu(g)).astype(o_ref.dtype)
            start_q_block(hh, jnp.minimum(i + 1, nq - 1))
        return 0

    lax.fori_loop(0, nq, q_block, 0)


def _diff_attn(qk, vg, lq1, lk1, lq2, lk2, subln_w, *, batch, seq, heads, lambda_init,
               bq=512, bk=512, hp=4):
    t = qk.shape[0]
    bq = min(bq, seq)
    bk = min(bk, bq)
    hp = math.gcd(hp, heads)
    nslot = bq // bk
    w = hp * LANES
    ng = heads // hp
    key = np.arange(bq)[:, None]
    qry = np.arange(bq)[None, :]
    tri = np.where(key <= qry, 0.0, NEG_BIG).astype(np.float32)
    bias = jnp.asarray(np.concatenate([tri, tri], axis=1).reshape(nslot, bk, 2 * bq))
    vec = lambda a: a.reshape(1, -1).astype(f32)
    small = pl.BlockSpec((1, HEAD_DIM), lambda b, h: (0, 0))
    single = pl.Buffered(1)
    kern = functools.partial(_attn_kernel, bq=bq, bk=bk, hp=hp, lambda_init=lambda_init)
    return pl.pallas_call(
        kern,
        grid=(batch, ng),
        in_specs=[pl.BlockSpec((seq, w), lambda b, h: (b, h), pipeline_mode=single),
                  pl.BlockSpec((seq, w), lambda b, h: (b, ng + h)),
                  pl.BlockSpec((seq, w), lambda b, h: (b, h)),
                  pl.BlockSpec((seq, w), lambda b, h: (b, ng + h), pipeline_mode=single),
                  small, small, small, small,
                  pl.BlockSpec((1, LANES), lambda b, h: (0, 0)),
                  pl.BlockSpec((nslot, bk, 2 * bq), lambda b, h: (0, 0, 0), pipeline_mode=single)],
        out_specs=pl.BlockSpec((seq, w), lambda b, h: (b, h)),
        out_shape=jax.ShapeDtypeStruct((t, heads * LANES), bf16),
        scratch_shapes=[pltpu.VMEM((hp, seq // bk, LANES + BF16_ROWS, bk), bf16),
                        pltpu.VMEM((hp, LANES, 2 * bq), bf16),
                        pltpu.VMEM((hp, nslot, bk, 2 * bq), f32),
                        pltpu.VMEM((hp, nslot, 1, 2 * bq), f32),
                        pltpu.VMEM((hp, LANES + BF16_ROWS, 2 * bq), f32)],
        compiler_params=_params("arbitrary", "arbitrary"),
        name="diff_attn",
    )(qk, qk, vg, vg, vec(lq1), vec(lk1), vec(lq2), vec(lk2), vec(subln_w), bias)


def _ssd_constants(rheads):
    L, P = SSM_CHUNK, SSM_HEAD_DIM
    k = np.arange(LANES)[:, None]
    col = np.arange(rheads * LANES)[None, :]
    seg = (k < 3 * rheads) & (k % rheads == col // LANES)
    col2 = np.arange(2 * rheads * P)[None, :]
    half = rheads * P
    exp = np.where(col2 < half,
                   (k < 2 * rheads) & (k % rheads == col2 // P),
                   (k >= 2 * rheads) & (k < 4 * rheads) & (k % rheads == (col2 - half) // P))
    upper = (np.arange(L)[:, None] <= np.arange(L)[None, :])
    return jnp.asarray(seg, dtype=bf16), jnp.asarray(exp, dtype=bf16), jnp.asarray(upper, dtype=bf16)


def _split_bf16(a, pieces):
    out, rem = [], a
    for _ in range(pieces):
        p = rem.astype(bf16).astype(f32)
        out.append(p)
        rem = rem - p
    return out


def _ssd_kernel(dt_ref, dtb_ref, alog_ref, z_ref, xs_ref, b_ref, c_ref, dskip_ref, nw_ref,
                segoh_ref, expoh_ref, upper_ref,
                o_ref,
                state_ref, cs_ref, dtc_ref, cslt_ref,
                *, groups, rheads, gp):
    c = pl.program_id(1)
    pg = pl.program_id(2)

    @pl.when(c == 0)
    def _():
        state_ref[:, pg] = jnp.zeros((gp,) + state_ref.shape[2:], f32)

    def bookkeeping():
        raw = dt_ref[...] + dtb_ref[...]
        dt = jnp.maximum(raw, 0.0) + jnp.log(1.0 + jnp.exp(-jnp.abs(raw)))
        da_t = (dt * (-jnp.exp(alog_ref[...]))).T
        cs_t = sum(jnp.dot(piece.astype(bf16), upper_ref[...], preferred_element_type=f32)
                   for piece in _split_bf16(da_t, 3))
        cslt_ref[...] = cs_t - jnp.log(dt.T)
        cs = cs_t.T
        for gg in range(groups):
            shift = (LANES - gg * rheads) % LANES
            cs_ref[gg] = pltpu.roll(cs, shift, 1) if shift else cs
            dtc_ref[gg] = pltpu.roll(dt, shift, 1) if shift else dt

    if gp == groups:
        bookkeeping()
    else:
        pl.when(pg == 0)(bookkeeping)

    for gi in range(gp):
        _ssd_group(gi, pg, z_ref, xs_ref, b_ref, c_ref, dskip_ref, nw_ref, segoh_ref, expoh_ref,
                   o_ref, state_ref, cs_ref, dtc_ref, cslt_ref, rheads, gp)


def _ssd_group(gi, pg, z_ref, xs_ref, b_ref, c_ref, dskip_ref, nw_ref, segoh_ref, expoh_ref, o_ref,
               state_ref, cs_ref, dtc_ref, cslt_ref, rheads, gp):
    L = SSM_CHUNK
    R = rheads
    xw = rheads * SSM_HEAD_DIM
    n = SSM_STATE
    g = pg * gp + gi
    xsl = slice(gi * xw, (gi + 1) * xw)
    nsl = slice(gi * n, (gi + 1) * n)

    xs = xs_ref[:, xsl].astype(f32)
    bm16 = b_ref[:, nsl]
    cm16 = c_ref[:, nsl]

    row0 = pl.multiple_of(g * rheads, rheads)
    csl_t = cslt_ref[pl.ds(row0, rheads), :]
    cs_c = cs_ref[g]
    dt_c = dtc_ref[g]
    lane = lax.broadcasted_iota(jnp.int32, (L, LANES), 1)

    hi, mid, lo = _split_bf16(cs_c, 3)
    packed = jnp.where(lane < R, hi, jnp.where(lane < 2 * R, pltpu.roll(mid, R, 1), pltpu.roll(lo, 2 * R, 1)))
    seg_col = jnp.dot(packed.astype(bf16), segoh_ref[...], preferred_element_type=f32)

    cb = lax.dot_general(cm16, bm16, (((1,), (1,)), ((), ())),
                         preferred_element_type=f32)
    l_idx = lax.broadcasted_iota(jnp.int32, (L, L), 0)
    s_idx = lax.broadcasted_iota(jnp.int32, (L, L), 1)
    causal = l_idx >= s_idx
    lo_half = lane < SSM_HEAD_DIM

    y_parts = []
    for j in range(rheads // 2):
        xj = xs[:, j * LANES:(j + 1) * LANES]
        x_lo = jnp.where(lo_half, xj, 0.0).astype(bf16)
        x_hi = jnp.where(lo_half, 0.0, xj).astype(bf16)
        yj = None
        for r, xr in ((2 * j, x_lo), (2 * j + 1, x_hi)):
            seg = seg_col[:, r * LANES:(r + 1) * LANES] - csl_t[r:r + 1, :]
            decay_dt = jnp.exp(jnp.where(causal, seg, -jnp.inf))
            mr = (cb * decay_dt).astype(bf16)
            part = jnp.dot(mr, xr, preferred_element_type=f32)
            yj = part if yj is None else yj + part
        y_parts.append(yj)
    y_diag = jnp.concatenate(y_parts, axis=1)

    cs_last = cs_c[L - 1:L, :]
    e_hi, e_lo = _split_bf16(jnp.exp(cs_c), 2)
    w_hi, w_lo = _split_bf16(jnp.exp(cs_last - cs_c) * dt_c, 2)
    packed2 = jnp.where(lane < R, e_hi,
                        jnp.where(lane < 2 * R, pltpu.roll(e_lo, R, 1),
                                  jnp.where(lane < 3 * R, pltpu.roll(w_hi, 2 * R, 1),
                                            pltpu.roll(w_lo, 3 * R, 1))))
    spread = jnp.dot(packed2.astype(bf16), expoh_ref[...], preferred_element_type=f32)
    e_cs = spread[:, :xw]
    wgt = spread[:, xw:]

    prev = state_ref[gi, pg]
    y_off = jnp.dot(cm16, prev.astype(bf16), preferred_element_type=f32) * e_cs
    new = jnp.dot(bm16.astype(f32).T.astype(bf16), (xs * wgt).astype(bf16), preferred_element_type=f32)
    state_ref[gi, pg] = prev * e_cs[L - 1:L, :] + new

    y = y_diag + y_off + xs * dskip_ref[:, xsl]
    y = y * z_ref[:, xsl].astype(f32)
    ms = jnp.mean(y * y, axis=-1, keepdims=True)
    o_ref[:, xsl] = (y * lax.rsqrt(ms + EPS) * nw_ref[:, xsl]).astype(o_ref.dtype)


def _ssd(zproj, xconv, bc, dt_raw, dt_bias, a_log, d_skip, norm_w, *, batch, seq, d_inner, groups, gp=8):
    t = zproj.shape[0]
    L = SSM_CHUNK
    nc = seq // L
    heads = d_inner // SSM_HEAD_DIM
    rheads = heads // groups
    assert rheads % 2 == 0 and 4 * rheads <= LANES
    gp = math.gcd(gp, groups)
    xw = rheads * SSM_HEAD_DIM
    n = SSM_STATE
    bw, bn = gp * xw, gp * n
    assert d_inner % bw == 0
    b_blk0 = 0
    c_blk0 = groups // gp
    row = lambda b, c, g: b * nc + c
    pad = lambda a: jnp.pad(a.reshape(1, -1).astype(f32), ((0, 0), (0, LANES - heads)))
    const2 = lambda shape: pl.BlockSpec(shape, lambda b, c, g: (0, 0))
    seg_oh, exp_oh, upper = _ssd_constants(rheads)
    kern = functools.partial(_ssd_kernel, groups=groups, rheads=rheads, gp=gp)
    return pl.pallas_call(
        kern,
        grid=(batch, nc, groups // gp),
        in_specs=[pl.BlockSpec((L, LANES), lambda b, c, g: (row(b, c, g), 0)),
                  const2((1, LANES)),
                  const2((1, LANES)),
                  pl.BlockSpec((L, bw), lambda b, c, g: (row(b, c, g), g)),
                  pl.BlockSpec((L, bw), lambda b, c, g: (row(b, c, g), g)),
                  pl.BlockSpec((L, bn), lambda b, c, g: (row(b, c, g), b_blk0 + g)),
                  pl.BlockSpec((L, bn), lambda b, c, g: (row(b, c, g), c_blk0 + g)),
                  pl.BlockSpec((1, bw), lambda b, c, g: (0, g)),
                  pl.BlockSpec((1, bw), lambda b, c, g: (0, g)),
                  const2((LANES, rheads * LANES)),
                  const2((LANES, 2 * xw)),
                  const2((L, L))],
        out_specs=pl.BlockSpec((L, bw), lambda b, c, g: (row(b, c, g), g)),
        out_shape=jax.ShapeDtypeStruct((t, d_inner), bf16),
        scratch_shapes=[pltpu.VMEM((gp, groups // gp, n, xw), f32),
                        pltpu.VMEM((groups, L, LANES), f32),
                        pltpu.VMEM((groups, L, LANES), f32),
                        pltpu.VMEM((LANES, L), f32)],
        compiler_params=_params("arbitrary", "arbitrary", "arbitrary"),
        name="ssd_chunk",
    )(dt_raw, pad(dt_bias), pad(a_log), zproj, xconv, bc, bc,
      jnp.repeat(d_skip, SSM_HEAD_DIM).reshape(1, -1), norm_w.reshape(1, -1),
      seg_oh, exp_oh, upper)


def kernel(x, c, positions, norm_w, ada_w, ada_b, attn_w_in, attn_q_norm, attn_k_norm, attn_lambda_q1, attn_lambda_k1, attn_lambda_q2, attn_lambda_k2, attn_subln_w, attn_w_out, ssm_w_in, ssm_conv_w, ssm_conv_b, ssm_dt_bias, ssm_A_log, ssm_D, ssm_norm_w, ssm_w_out):
    batch, seq, d = x.shape
    depth = norm_w.shape[0]
    t = batch * seq
    heads = d // (2 * HEAD_DIM)
    qk_width = heads * 2 * HEAD_DIM
    d_inner = ssm_w_out.shape[1]
    conv_dim = ssm_conv_w.shape[2]
    groups = (conv_dim - d_inner) // (2 * SSM_STATE)
    ssm_heads = ssm_dt_bias.shape[1]
    tm = min(1024, seq)

    c_pad = jnp.pad(c, ((0, 8 - batch), (0, 0)))
    mod = _ada_mod(c_pad, ada_w, ada_b)[:, :batch]
    shift, scale, gate = (mod[:, :, i * d:(i + 1) * d].reshape(depth, batch, 1, d) for i in range(3))

    inv_freq = ROPE_THETA ** (-jnp.arange(0, HEAD_DIM, 2, dtype=f32) / HEAD_DIM)
    freq_row = jnp.tile(inv_freq, LANES // (HEAD_DIM // 2)).reshape(1, LANES)
    cos_t, sin_t = _rope_tables(positions.reshape(t, 1), freq_row)

    main = d_inner + conv_dim
    assert main % LANES == 0 and ssm_heads <= LANES
    w_ssm = ssm_w_in.astype(bf16)
    w_ssm_dt = jnp.pad(w_ssm[:, :, main:], ((0, 0), (0, 0), (0, LANES - ssm_heads)))

    x2 = x.reshape(t, d)
    for layer in range(depth):
        j = layer // 2
        h = _norm_mod(x2, norm_w[layer].reshape(1, d), scale[layer], shift[layer], seq)
        if layer % 2 == 0:
            lambda_init = 0.8 - 0.6 * math.exp(-0.3 * layer)
            rep = qk_width // HEAD_DIM
            qk_w = jnp.concatenate([jnp.tile(attn_q_norm[j], rep) * (HEAD_DIM ** -0.5 * math.log2(math.e)),
                                    jnp.tile(attn_k_norm[j], rep)]).reshape(1, 2 * qk_width)
            tn_a = _tile(qk_width, 1024)
            qk = _proj_qk(h, attn_w_in, j, qk_w, cos_t, sin_t, tm, tn_a)
            vg = _proj(h, attn_w_in, j, attn_w_in.shape[2] - 2 * qk_width, bf16, tm, tn_a,
                       col0=2 * qk_width)
            o = _diff_attn(qk, vg, attn_lambda_q1[j], attn_lambda_k1[j], attn_lambda_q2[j],
                           attn_lambda_k2[j], attn_subln_w[j], batch=batch, seq=seq, heads=heads,
                           lambda_init=lambda_init)
            x2 = _out_proj(o, attn_w_out, j, x2, gate[layer], seq, tm, _tile(d, 1024))
        else:
            tn_c = _tile(math.gcd(d_inner, conv_dim), 1024)
            conv_b = ssm_conv_b[j].reshape(1, -1)
            xconv, zact = _proj_conv(h, w_ssm, j, d_inner, d_inner, ssm_conv_w[j], conv_b, 0, seq, tm, tn_c,
                                     gate_col0=0)
            bc = _proj_conv(h, w_ssm, j, 2 * d_inner, conv_dim - d_inner, ssm_conv_w[j], conv_b, d_inner,
                            seq, tm, tn_c)
            dt_raw = _proj(h, w_ssm_dt, j, LANES, f32, tm, LANES)
            y = _ssd(zact, xconv, bc, dt_raw, ssm_dt_bias[j], ssm_A_log[j], ssm_D[j], ssm_norm_w[j],
                     batch=batch, seq=seq, d_inner=d_inner, groups=groups)
            x2 = _out_proj(y, ssm_w_out, j, x2, gate[layer], seq, tm, _tile(d, 512))
    return x2.reshape(batch, seq, d)
```

```python
import functools
import math

import numpy as np
import jax
import jax.numpy as jnp
from jax import lax
from jax.experimental import pallas as pl
from jax.experimental.pallas import tpu as pltpu

EPS = 1e-6
ROPE_THETA = 10000.0
LANES = 128
BF16_ROWS = 16
CONV_HALO = 8
QK_ROW_CHUNKS = 4
CONV_ROW_CHUNKS = 4
HEAD_DIM = 64
SSM_HEAD_DIM = 64
SSM_STATE = 128
SSM_CHUNK = 128
SSM_CONV = 4
VMEM_LIMIT_BYTES = 56 * 1024 * 1024
NEG_BIG = -1e30

f32 = jnp.float32
bf16 = jnp.bfloat16


def _params(*sem):
    return pltpu.CompilerParams(dimension_semantics=sem, vmem_limit_bytes=VMEM_LIMIT_BYTES)


def _silu(x):
    return x * jax.nn.sigmoid(x)


def _tile(n, target):
    if n <= target:
        return n
    best = None
    for cand in range(LANES, target + 1, LANES):
        if n % cand == 0:
            best = cand
    assert best is not None, (n, target)
    return best


def _ada_kernel(c_ref, w_ref, b_ref, o_ref):
    cond = _silu(c_ref[...])
    o_ref[0] = jnp.dot(cond, w_ref[0], preferred_element_type=f32) + b_ref[0]


def _ada_mod(c_pad, ada_w, ada_b, tn=1024):
    depth, d, n = ada_w.shape
    rows = c_pad.shape[0]
    tn = _tile(n, tn)
    return pl.pallas_call(
        _ada_kernel,
        grid=(depth, n // tn),
        in_specs=[pl.BlockSpec((rows, d), lambda l, j: (0, 0)),
                  pl.BlockSpec((1, d, tn), lambda l, j: (l, 0, j)),
                  pl.BlockSpec((1, 1, tn), lambda l, j: (l, 0, j))],
        out_specs=pl.BlockSpec((1, rows, tn), lambda l, j: (l, 0, j)),
        out_shape=jax.ShapeDtypeStruct((depth, rows, n), f32),
        compiler_params=_params("arbitrary", "arbitrary"),
        name="ada_mod",
    )(c_pad, ada_w, ada_b.reshape(depth, 1, n))


def _norm_kernel(x_ref, nw_ref, sc_ref, sh_ref, o_ref):
    x = x_ref[...]
    ms = jnp.mean(x * x, axis=-1, keepdims=True)
    y = x * lax.rsqrt(ms + EPS) * nw_ref[...]
    o_ref[...] = (y * (1.0 + sc_ref[0]) + sh_ref[0]).astype(o_ref.dtype)


def _norm_mod(x2, nw, scale, shift, seq, tm=1024):
    t, d = x2.shape
    tm = min(tm, seq)
    tiles_per_seq = seq // tm
    return pl.pallas_call(
        _norm_kernel,
        grid=(t // tm,),
        in_specs=[pl.BlockSpec((tm, d), lambda i: (i, 0)),
                  pl.BlockSpec((1, d), lambda i: (0, 0)),
                  pl.BlockSpec((1, 1, d), lambda i: (i // tiles_per_seq, 0, 0)),
                  pl.BlockSpec((1, 1, d), lambda i: (i // tiles_per_seq, 0, 0))],
        out_specs=pl.BlockSpec((tm, d), lambda i: (i, 0)),
        out_shape=jax.ShapeDtypeStruct((t, d), bf16),
        compiler_params=_params("arbitrary"),
        name="norm_mod",
    )(x2, nw, scale, shift)


def _stage_weight(w_ref, wb_ref):
    @pl.when(pl.program_id(1) == 0)
    def _():
        wb_ref[...] = w_ref[0].astype(bf16)


def _proj_kernel(a_ref, w_ref, o_ref, wb_ref, *, silu):
    _stage_weight(w_ref, wb_ref)
    acc = jnp.dot(a_ref[...], wb_ref[...], preferred_element_type=f32)
    if silu:
        acc = acc / (1.0 + jnp.exp(-acc))
    o_ref[...] = acc.astype(o_ref.dtype)


def _proj(a, w, w_layer, n_out, out_dtype, tm, tn, silu=False, col0=0):
    t, k = a.shape
    assert col0 % tn == 0
    cb0 = col0 // tn
    kern = functools.partial(_proj_kernel, silu=silu)
    return pl.pallas_call(
        kern,
        grid=(n_out // tn, t // tm),
        in_specs=[pl.BlockSpec((tm, k), lambda j, i: (i, 0)),
                  pl.BlockSpec((1, k, tn), lambda j, i: (w_layer, 0, cb0 + j))],
        out_specs=pl.BlockSpec((tm, tn), lambda j, i: (i, j)),
        out_shape=jax.ShapeDtypeStruct((t, n_out), out_dtype),
        scratch_shapes=[pltpu.VMEM((k, tn), bf16)],
        compiler_params=_params("arbitrary", "arbitrary"),
        name="proj",
    )(a, w)


def _proj_qk_kernel(a_ref, w_ref, qw_ref, cos_ref, sin_ref, o_ref, wb_ref):
    _stage_weight(w_ref, wb_ref)
    tm, tn = o_ref.shape
    r_idx = lax.broadcasted_iota(jnp.int32, (LANES, LANES), 0) // HEAD_DIM
    c_idx = lax.broadcasted_iota(jnp.int32, (LANES, LANES), 1) // HEAD_DIM
    group_mean = jnp.where(r_idx == c_idx, 1.0 / HEAD_DIM, 0.0).astype(bf16)
    rc = tm // QK_ROW_CHUNKS
    lane = lax.broadcasted_iota(jnp.int32, (rc, LANES), 1)
    first_half = (lane & (HEAD_DIM // 2)) == 0
    w = wb_ref[...]
    for r in range(QK_ROW_CHUNKS):
        rows = slice(r * rc, (r + 1) * rc)
        acc = jnp.dot(a_ref[rows, :], w, preferred_element_type=f32)
        cos = cos_ref[rows, :]
        sin = sin_ref[rows, :]
        for c in range(tn // LANES):
            sl = slice(c * LANES, (c + 1) * LANES)
            x = acc[:, sl]
            ms = jnp.dot((x * x).astype(bf16), group_mean, preferred_element_type=f32)
            y = x * lax.rsqrt(ms + EPS) * qw_ref[:, sl]
            partner = jnp.where(first_half,
                                pltpu.roll(y, LANES - HEAD_DIM // 2, 1),
                                pltpu.roll(y, HEAD_DIM // 2, 1))
            o_ref[rows, sl] = (y * cos + partner * sin).astype(o_ref.dtype)


def _proj_qk(a, w, w_layer, qk_w, cos_t, sin_t, tm, tn):
    t, k = a.shape
    n = qk_w.shape[1]
    return pl.pallas_call(
        _proj_qk_kernel,
        grid=(n // tn, t // tm),
        in_specs=[pl.BlockSpec((tm, k), lambda j, i: (i, 0)),
                  pl.BlockSpec((1, k, tn), lambda j, i: (w_layer, 0, j)),
                  pl.BlockSpec((1, tn), lambda j, i: (0, j)),
                  pl.BlockSpec((tm, LANES), lambda j, i: (i, 0)),
                  pl.BlockSpec((tm, LANES), lambda j, i: (i, 0))],
        out_specs=pl.BlockSpec((tm, tn), lambda j, i: (i, j)),
        out_shape=jax.ShapeDtypeStruct((t, n), bf16),
        scratch_shapes=[pltpu.VMEM((k, tn), bf16)],
        compiler_params=_params("arbitrary", "arbitrary"),
        name="proj_qk",
    )(a, w, qk_w, cos_t, sin_t)


def _proj_conv_kernel(*refs, tiles_per_seq, with_gate):
    if with_gate:
        a_ref, w_ref, cw_ref, cb_ref, wz_ref, o_ref, oz_ref, halo_ref = refs
    else:
        a_ref, w_ref, cw_ref, cb_ref, o_ref, halo_ref = refs
    i = pl.program_id(1)

    tm, tn = o_ref.shape
    rc = tm // CONV_ROW_CHUNKS
    w = w_ref[0]
    first = (i % tiles_per_seq) == 0
    prev = jnp.where(first, 0.0, halo_ref[...])
    nsl = rc // CONV_HALO
    sub = lax.broadcasted_iota(jnp.int32, (nsl + 1, CONV_HALO, tn), 1)
    w0, w1, w2, w3 = (cw_ref[tap:tap + 1, :] for tap in range(SSM_CONV))

    def shift_rows(slabs, back):
        rot = pltpu.roll(slabs, back, 1)
        above = jnp.concatenate([rot[:1], rot[:-1]], axis=0)
        return jnp.where(sub < back, above, rot)

    for r in range(CONV_ROW_CHUNKS):
        acc = jnp.dot(a_ref[r * rc:(r + 1) * rc, :], w, preferred_element_type=f32)
        slabs = jnp.concatenate([prev, acc], axis=0).reshape(nsl + 1, CONV_HALO, tn)
        x1 = shift_rows(slabs, 1)
        v2 = shift_rows(w1 * slabs + w0 * x1, 2)
        y = (cb_ref[...] + w3 * slabs + w2 * x1 + v2)[1:].reshape(rc, tn)
        o_ref[r * rc:(r + 1) * rc, :] = (y / (1.0 + jnp.exp(-y))).astype(o_ref.dtype)
        prev = acc[rc - CONV_HALO:, :]
        if with_gate:
            z = jnp.dot(a_ref[r * rc:(r + 1) * rc, :], wz_ref[0], preferred_element_type=f32)
            oz_ref[r * rc:(r + 1) * rc, :] = (z / (1.0 + jnp.exp(-z))).astype(oz_ref.dtype)
    halo_ref[...] = prev


def _proj_conv(a, w, w_layer, col0, n, conv_w, conv_b, conv_col0, seq, tm, tn, gate_col0=None):
    t, k = a.shape
    assert col0 % tn == 0 and conv_col0 % tn == 0 and n % tn == 0
    cb0, ccb0 = col0 // tn, conv_col0 // tn
    tiles_per_seq = seq // tm
    with_gate = gate_col0 is not None
    kern = functools.partial(_proj_conv_kernel, tiles_per_seq=tiles_per_seq, with_gate=with_gate)
    in_specs = [pl.BlockSpec((tm, k), lambda j, i: (i, 0)),
                pl.BlockSpec((1, k, tn), lambda j, i: (w_layer, 0, cb0 + j)),
                pl.BlockSpec((SSM_CONV, tn), lambda j, i: (0, ccb0 + j)),
                pl.BlockSpec((1, tn), lambda j, i: (0, ccb0 + j))]
    out_spec = pl.BlockSpec((tm, tn), lambda j, i: (i, j))
    out_shape = jax.ShapeDtypeStruct((t, n), bf16)
    operands = [a, w, conv_w, conv_b]
    if with_gate:
        assert gate_col0 % tn == 0
        gb0 = gate_col0 // tn
        in_specs.append(pl.BlockSpec((1, k, tn), lambda j, i: (w_layer, 0, gb0 + j)))
        operands.append(w)
    return pl.pallas_call(
        kern,
        grid=(n // tn, t // tm),
        in_specs=in_specs,
        out_specs=[out_spec, out_spec] if with_gate else out_spec,
        out_shape=[out_shape, out_shape] if with_gate else out_shape,
        scratch_shapes=[pltpu.VMEM((CONV_HALO, tn), f32)],
        compiler_params=_params("arbitrary", "arbitrary"),
        name="proj_conv",
    )(*operands)


def _outproj_kernel(a_ref, w_ref, x_ref, g_ref, o_ref, wb_ref):
    _stage_weight(w_ref, wb_ref)
    acc = jnp.dot(a_ref[...], wb_ref[...], preferred_element_type=f32)
    o_ref[...] = x_ref[...] + g_ref[0] * acc


def _out_proj(a, w, w_layer, x2, gate, seq, tm, tn):
    t, k = a.shape
    n = w.shape[2]
    tiles_per_seq = seq // tm
    return pl.pallas_call(
        _outproj_kernel,
        grid=(n // tn, t // tm),
        in_specs=[pl.BlockSpec((tm, k), lambda j, i: (i, 0)),
                  pl.BlockSpec((1, k, tn), lambda j, i: (w_layer, 0, j), pipeline_mode=pl.Buffered(1)),
                  pl.BlockSpec((tm, tn), lambda j, i: (i, j)),
                  pl.BlockSpec((1, 1, tn), lambda j, i: (i // tiles_per_seq, 0, j))],
        out_specs=pl.BlockSpec((tm, tn), lambda j, i: (i, j)),
        out_shape=jax.ShapeDtypeStruct((t, n), f32),
        scratch_shapes=[pltpu.VMEM((k, tn), bf16)],
        compiler_params=_params("arbitrary", "arbitrary"),
        name="out_proj",
    )(a, w, x2, gate)


def _rope_table_kernel(pos_ref, freq_ref, cos_ref, sin_ref):
    ang = pos_ref[...].astype(f32) * freq_ref[...]
    lane = lax.broadcasted_iota(jnp.int32, ang.shape, 1)
    first_half = (lane & (HEAD_DIM // 2)) == 0
    s = jnp.sin(ang)
    cos_ref[...] = jnp.cos(ang)
    sin_ref[...] = jnp.where(first_half, -s, s)


def _rope_tables(pos_col, freq_row, tm=1024):
    t = pos_col.shape[0]
    tm = min(tm, t)
    return pl.pallas_call(
        _rope_table_kernel,
        grid=(t // tm,),
        in_specs=[pl.BlockSpec((tm, 1), lambda i: (i, 0)),
                  pl.BlockSpec((1, LANES), lambda i: (0, 0))],
        out_specs=[pl.BlockSpec((tm, LANES), lambda i: (i, 0)),
                   pl.BlockSpec((tm, LANES), lambda i: (i, 0))],
        out_shape=[jax.ShapeDtypeStruct((t, LANES), f32)] * 2,
        compiler_params=_params("arbitrary"),
        name="rope_table",
    )(pos_col, freq_row)


def _attn_kernel(q_ref, k_ref, v_ref, g_ref, lq1_ref, lk1_ref, lq2_ref, lk2_ref, sw_ref, bias_ref, o_ref,
                 vt_ref, qq_ref, s_ref, mx_ref, acc_ref, *, bq, bk, hp, lambda_init):
    seq = q_ref.shape[0]
    nslot = bq // bk
    lam = (jnp.exp(jnp.sum(lq1_ref[...] * lk1_ref[...], axis=-1, keepdims=True))
           - jnp.exp(jnp.sum(lq2_ref[...] * lk2_ref[...], axis=-1, keepdims=True))
           + lambda_init)
    cols = [slice(hh * LANES, (hh + 1) * LANES) for hh in range(hp)]

    ones_rows = jnp.where(lax.broadcasted_iota(jnp.int32, (BF16_ROWS, bk), 0) == 0, 1.0, 0.0).astype(bf16)

    def transpose_v(c, _):
        start = pl.multiple_of(c * bk, bk)
        for hh in range(hp):
            vt_ref[hh, c, 0:LANES, :] = v_ref[pl.ds(start, bk), cols[hh]].astype(f32).T.astype(bf16)
            vt_ref[hh, c, LANES:, :] = ones_rows
        return 0
    lax.fori_loop(0, seq // bk, transpose_v, 0)

    d_row = lax.broadcasted_iota(jnp.int32, (LANES, bq), 0)

    def produce(hh, slot, j):
        start = pl.multiple_of(j * bk, bk)
        st = jnp.dot(k_ref[pl.ds(start, bk), cols[hh]], qq_ref[hh], preferred_element_type=f32)
        s_ref[hh, slot] = st
        mx_ref[hh, slot] = jnp.max(st, axis=0, keepdims=True)

    def consume(hh, slot, j, m, masked):
        st = s_ref[hh, slot]
        if masked:
            st = st + bias_ref[slot]
            blk_max = jnp.max(st, axis=0, keepdims=True)
        else:
            blk_max = mx_ref[hh, slot]
        m_new = jnp.maximum(m, blk_max)
        alpha = jnp.exp2(m - m_new)
        p = jnp.exp2(st - m_new)
        acc_ref[hh] = alpha * acc_ref[hh] + jnp.dot(vt_ref[hh, j], p.astype(bf16),
                                                    preferred_element_type=f32)
        return m_new

    def consume_diagonal(hh, j, m):
        half = bq // 2
        late = (slice(half, bq), slice(bq + half, 2 * bq))
        vt = vt_ref[hh, j]
        st = s_ref[hh, 0, 0:half, :] + bias_ref[0, 0:half, :]
        m1 = jnp.maximum(m, jnp.max(st, axis=0, keepdims=True))
        p = jnp.exp2(st - m1)
        acc = jnp.exp2(m - m1) * acc_ref[hh] + jnp.dot(vt[:, 0:half], p.astype(bf16),
                                                       preferred_element_type=f32)
        st = jnp.concatenate([s_ref[hh, 0, half:, l] + bias_ref[0, half:, l] for l in late], axis=1)
        m1_late = jnp.concatenate([m1[:, l] for l in late], axis=1)
        m2 = jnp.maximum(m1_late, jnp.max(st, axis=0, keepdims=True))
        p = jnp.exp2(st - m2)
        acc_late = (jnp.exp2(m1_late - m2) * jnp.concatenate([acc[:, l] for l in late], axis=1)
                    + jnp.dot(vt[:, half:], p.astype(bf16), preferred_element_type=f32))
        return jnp.concatenate([acc[:, 0:half], acc_late[:, 0:half],
                                acc[:, bq:bq + half], acc_late[:, half:]], axis=1)

    nq = seq // bq

    def start_q_block(hh, i):
        q_start = pl.multiple_of(i * bq, bq)
        qt = q_ref[pl.ds(q_start, bq), cols[hh]].astype(f32).T
        qq_ref[hh] = jnp.concatenate([jnp.where(d_row < HEAD_DIM, qt, 0.0),
                                      jnp.where(d_row >= HEAD_DIM, qt, 0.0)], axis=1).astype(bf16)
        acc_ref[hh] = jnp.zeros(acc_ref.shape[1:], f32)
        for slot in range(nslot):
            produce(hh, slot, slot)

    for hh in range(hp):
        start_q_block(hh, 0)

    def q_block(i, _):
        q_start = pl.multiple_of(i * bq, bq)

        def body(jj, carry):
            out = []
            for hh in range(hp):
                m = carry[hh]
                for slot in range(nslot):
                    m = consume(hh, slot, jj * nslot + slot, m, False)
                    produce(hh, slot, (jj + 1) * nslot + slot)
                out.append(m)
            return tuple(out)

        init = tuple(jnp.full((1, 2 * bq), NEG_BIG, f32) for _ in range(hp))
        carry = lax.fori_loop(0, i, body, init)

        for hh in range(hp):
            m = carry[hh]
            if nslot == 1:
                acc = consume_diagonal(hh, i, m)
            else:
                for slot in range(nslot):
                    m = consume(hh, slot, i * nslot + slot, m, True)
                acc = acc_ref[hh]
            inv_l = 1.0 / acc[LANES:LANES + 1]
            odt = acc[:LANES, :bq] * inv_l[:, :bq] - acc[:LANES, bq:] * (lam * inv_l[:, bq:])
            ms = jnp.mean(odt * odt, axis=0, keepdims=True)
            y = (odt * lax.rsqrt(ms + EPS)).T * (sw_ref[...] * (1.0 - lambda_init))
            g = g_ref[pl.ds(q_start, bq), cols[hh]].astype(f32)
            o_ref[pl.ds(q_start, bq), cols[hh]] = (y * _silu(g)).astype(o_ref.dtype)
            start_q_block(hh, jnp.minimum(i + 1, nq - 1))
        return 0

    lax.fori_loop(0, nq, q_block, 0)


def _diff_attn(qk, vg, lq1, lk1, lq2, lk2, subln_w, *, batch, seq, heads, lambda_init,
               bq=512, bk=512, hp=4):
    t = qk.shape[0]
    bq = min(bq, seq)
    bk = min(bk, bq)
    hp = math.gcd(hp, heads)
    nslot = bq // bk
    w = hp * LANES
    ng = heads // hp
    key = np.arange(bq)[:, None]
    qry = np.arange(bq)[None, :]
    tri = np.where(key <= qry, 0.0, NEG_BIG).astype(np.float32)
    bias = jnp.asarray(np.concatenate([tri, tri], axis=1).reshape(nslot, bk, 2 * bq))
    vec = lambda a: a.reshape(1, -1).astype(f32)
    small = pl.BlockSpec((1, HEAD_DIM), lambda b, h: (0, 0))
    single = pl.Buffered(1)
    kern = functools.partial(_attn_kernel, bq=bq, bk=bk, hp=hp, lambda_init=lambda_init)
    return pl.pallas_call(
        kern,
        grid=(batch, ng),
        in_specs=[pl.BlockSpec((seq, w), lambda b, h: (b, h), pipeline_mode=single),
                  pl.BlockSpec((seq, w), lambda b, h: (b, ng + h)),
                  pl.BlockSpec((seq, w), lambda b, h: (b, h)),
                  pl.BlockSpec((seq, w), lambda b, h: (b, ng + h), pipeline_mode=single),
                  small, small, small, small,
                  pl.BlockSpec((1, LANES), lambda b, h: (0, 0)),
                  pl.BlockSpec((nslot, bk, 2 * bq), lambda b, h: (0, 0, 0), pipeline_mode=single)],
        out_specs=pl.BlockSpec((seq, w), lambda b, h: (b, h)),
        out_shape=jax.ShapeDtypeStruct((t, heads * LANES), bf16),
        scratch_shapes=[pltpu.VMEM((hp, seq // bk, LANES + BF16_ROWS, bk), bf16),
                        pltpu.VMEM((hp, LANES, 2 * bq), bf16),
                        pltpu.VMEM((hp, nslot, bk, 2 * bq), f32),
                        pltpu.VMEM((hp, nslot, 1, 2 * bq), f32),
                        pltpu.VMEM((hp, LANES + BF16_ROWS, 2 * bq), f32)],
        compiler_params=_params("arbitrary", "arbitrary"),
        name="diff_attn",
    )(qk, qk, vg, vg, vec(lq1), vec(lk1), vec(lq2), vec(lk2), vec(subln_w), bias)


def _ssd_constants(rheads):
    L, P = SSM_CHUNK, SSM_HEAD_DIM
    k = np.arange(LANES)[:, None]
    col = np.arange(rheads * LANES)[None, :]
    seg = (k < 3 * rheads) & (k % rheads == col // LANES)
    col2 = np.arange(2 * rheads * P)[None, :]
    half = rheads * P
    exp = np.where(col2 < half,
                   (k < 2 * rheads) & (k % rheads == col2 // P),
                   (k >= 2 * rheads) & (k < 4 * rheads) & (k % rheads == (col2 - half) // P))
    upper = (np.arange(L)[:, None] <= np.arange(L)[None, :])
    return jnp.asarray(seg, dtype=bf16), jnp.asarray(exp, dtype=bf16), jnp.asarray(upper, dtype=bf16)


def _split_bf16(a, pieces):
    out, rem = [], a
    for _ in range(pieces):
        p = rem.astype(bf16).astype(f32)
        out.append(p)
        rem = rem - p
    return out


def _ssd_kernel(dt_ref, dtb_ref, alog_ref, z_ref, xs_ref, b_ref, c_ref, dskip_ref, nw_ref,
                segoh_ref, expoh_ref, upper_ref,
                o_ref,
                state_ref, cs_ref, dtc_ref, cslt_ref,
                *, groups, rheads, gp):
    c = pl.program_id(1)
    pg = pl.program_id(2)

    @pl.when(c == 0)
    def _():
        state_ref[:, pg] = jnp.zeros((gp,) + state_ref.shape[2:], f32)

    def bookkeeping():
        raw = dt_ref[...] + dtb_ref[...]
        dt = jnp.maximum(raw, 0.0) + jnp.log(1.0 + jnp.exp(-jnp.abs(raw)))
        da_t = (dt * (-jnp.exp(alog_ref[...]))).T
        cs_t = sum(jnp.dot(piece.astype(bf16), upper_ref[...], preferred_element_type=f32)
                   for piece in _split_bf16(da_t, 3))
        cslt_ref[...] = cs_t - jnp.log(dt.T)
        cs = cs_t.T
        for gg in range(groups):
            shift = (LANES - gg * rheads) % LANES
            cs_ref[gg] = pltpu.roll(cs, shift, 1) if shift else cs
            dtc_ref[gg] = pltpu.roll(dt, shift, 1) if shift else dt

    if gp == groups:
        bookkeeping()
    else:
        pl.when(pg == 0)(bookkeeping)

    for gi in range(gp):
        _ssd_group(gi, pg, z_ref, xs_ref, b_ref, c_ref, dskip_ref, nw_ref, segoh_ref, expoh_ref,
                   o_ref, state_ref, cs_ref, dtc_ref, cslt_ref, rheads, gp)


def _ssd_group(gi, pg, z_ref, xs_ref, b_ref, c_ref, dskip_ref, nw_ref, segoh_ref, expoh_ref, o_ref,
               state_ref, cs_ref, dtc_ref, cslt_ref, rheads, gp):
    L = SSM_CHUNK
    R = rheads
    xw = rheads * SSM_HEAD_DIM
    n = SSM_STATE
    g = pg * gp + gi
    xsl = slice(gi * xw, (gi + 1) * xw)
    nsl = slice(gi * n, (gi + 1) * n)

    xs = xs_ref[:, xsl].astype(f32)
    bm16 = b_ref[:, nsl]
    cm16 = c_ref[:, nsl]

    row0 = pl.multiple_of(g * rheads, rheads)
    csl_t = cslt_ref[pl.ds(row0, rheads), :]
    cs_c = cs_ref[g]
    dt_c = dtc_ref[g]
    lane = lax.broadcasted_iota(jnp.int32, (L, LANES), 1)

    hi, mid, lo = _split_bf16(cs_c, 3)
    packed = jnp.where(lane < R, hi, jnp.where(lane < 2 * R, pltpu.roll(mid, R, 1), pltpu.roll(lo, 2 * R, 1)))
    seg_col = jnp.dot(packed.astype(bf16), segoh_ref[...], preferred_element_type=f32)

    cb = lax.dot_general(cm16, bm16, (((1,), (1,)), ((), ())),
                         preferred_element_type=f32)
    l_idx = lax.broadcasted_iota(jnp.int32, (L, L), 0)
    s_idx = lax.broadcasted_iota(jnp.int32, (L, L), 1)
    causal = l_idx >= s_idx
    lo_half = lane < SSM_HEAD_DIM

    y_parts = []
    for j in range(rheads // 2):
        xj = xs[:, j * LANES:(j + 1) * LANES]
        x_lo = jnp.where(lo_half, xj, 0.0).astype(bf16)
        x_hi = jnp.where(lo_half, 0.0, xj).astype(bf16)
        yj = None
        for r, xr in ((2 * j, x_lo), (2 * j + 1, x_hi)):
            seg = seg_col[:, r * LANES:(r + 1) * LANES] - csl_t[r:r + 1, :]
            decay_dt = jnp.exp(jnp.where(causal, seg, -jnp.inf))
            mr = (cb * decay_dt).astype(bf16)
            part = jnp.dot(mr, xr, preferred_element_type=f32)
            yj = part if yj is None else yj + part
        y_parts.append(yj)
    y_diag = jnp.concatenate(y_parts, axis=1)

    cs_last = cs_c[L - 1:L, :]
    e_hi, e_lo = _split_bf16(jnp.exp(cs_c), 2)
    w_hi, w_lo = _split_bf16(jnp.exp(cs_last - cs_c) * dt_c, 2)
    packed2 = jnp.where(lane < R, e_hi,
                        jnp.where(lane < 2 * R, pltpu.roll(e_lo, R, 1),
                                  jnp.where(lane < 3 * R, pltpu.roll(w_hi, 2 * R, 1),
                                            pltpu.roll(w_lo, 3 * R, 1))))
    spread = jnp.dot(packed2.astype(bf16), expoh_ref[...], preferred_element_type=f32)
    e_cs = spread[:, :xw]
    wgt = spread[:, xw:]

    prev = state_ref[gi, pg]
    y_off = jnp.dot(cm16, prev.astype(bf16), preferred_element_type=f32) * e_cs
    new = jnp.dot(bm16.astype(f32).T.astype(bf16), (xs * wgt).astype(bf16), preferred_element_type=f32)
    state_ref[gi, pg] = prev * e_cs[L - 1:L, :] + new

    y = y_diag + y_off + xs * dskip_ref[:, xsl]
    y = y * z_ref[:, xsl].astype(f32)
    ms = jnp.mean(y * y, axis=-1, keepdims=True)
    o_ref[:, xsl] = (y * lax.rsqrt(ms + EPS) * nw_ref[:, xsl]).astype(o_ref.dtype)


def _ssd(zproj, xconv, bc, dt_raw, dt_bias, a_log, d_skip, norm_w, *, batch, seq, d_inner, groups, gp=8):
    t = zproj.shape[0]
    L = SSM_CHUNK
    nc = seq // L
    heads = d_inner // SSM_HEAD_DIM
    rheads = heads // groups
    assert rheads % 2 == 0 and 4 * rheads <= LANES
    gp = math.gcd(gp, groups)
    xw = rheads * SSM_HEAD_DIM
    n = SSM_STATE
    bw, bn = gp * xw, gp * n
    assert d_inner % bw == 0
    b_blk0 = 0
    c_blk0 = groups // gp
    row = lambda b, c, g: b * nc + c
    pad = lambda a: jnp.pad(a.reshape(1, -1).astype(f32), ((0, 0), (0, LANES - heads)))
    const2 = lambda shape: pl.BlockSpec(shape, lambda b, c, g: (0, 0))
    seg_oh, exp_oh, upper = _ssd_constants(rheads)
    kern = functools.partial(_ssd_kernel, groups=groups, rheads=rheads, gp=gp)
    return pl.pallas_call(
        kern,
        grid=(batch, nc, groups // gp),
        in_specs=[pl.BlockSpec((L, LANES), lambda b, c, g: (row(b, c, g), 0)),
                  const2((1, LANES)),
                  const2((1, LANES)),
                  pl.BlockSpec((L, bw), lambda b, c, g: (row(b, c, g), g)),
                  pl.BlockSpec((L, bw), lambda b, c, g: (row(b, c, g), g)),
                  pl.BlockSpec((L, bn), lambda b, c, g: (row(b, c, g), b_blk0 + g)),
                  pl.BlockSpec((L, bn), lambda b, c, g: (row(b, c, g), c_blk0 + g)),
                  pl.BlockSpec((1, bw), lambda b, c, g: (0, g)),
                  pl.BlockSpec((1, bw), lambda b, c, g: (0, g)),
                  const2((LANES, rheads * LANES)),
                  const2((LANES, 2 * xw)),
                  const2((L, L))],
        out_specs=pl.BlockSpec((L, bw), lambda b, c, g: (row(b, c, g), g)),
        out_shape=jax.ShapeDtypeStruct((t, d_inner), bf16),
        scratch_shapes=[pltpu.VMEM((gp, groups // gp, n, xw), f32),
                        pltpu.VMEM((groups, L, LANES), f32),
                        pltpu.VMEM((groups, L, LANES), f32),
                        pltpu.VMEM((LANES, L), f32)],
        compiler_params=_params("arbitrary", "arbitrary", "arbitrary"),
        name="ssd_chunk",
    )(dt_raw, pad(dt_bias), pad(a_log), zproj, xconv, bc, bc,
      jnp.repeat(d_skip, SSM_HEAD_DIM).reshape(1, -1), norm_w.reshape(1, -1),
      seg_oh, exp_oh, upper)


def kernel(x, c, positions, norm_w, ada_w, ada_b, attn_w_in, attn_q_norm, attn_k_norm, attn_lambda_q1, attn_lambda_k1, attn_lambda_q2, attn_lambda_k2, attn_subln_w, attn_w_out, ssm_w_in, ssm_conv_w, ssm_conv_b, ssm_dt_bias, ssm_A_log, ssm_D, ssm_norm_w, ssm_w_out):
    batch, seq, d = x.shape
    depth = norm_w.shape[0]
    t = batch * seq
    heads = d // (2 * HEAD_DIM)
    qk_width = heads * 2 * HEAD_DIM
    d_inner = ssm_w_out.shape[1]
    conv_dim = ssm_conv_w.shape[2]
    groups = (conv_dim - d_inner) // (2 * SSM_STATE)
    ssm_heads = ssm_dt_bias.shape[1]
    tm = min(1024, seq)

    c_pad = jnp.pad(c, ((0, 8 - batch), (0, 0)))
    mod = _ada_mod(c_pad, ada_w, ada_b)[:, :batch]
    shift, scale, gate = (mod[:, :, i * d:(i + 1) * d].reshape(depth, batch, 1, d) for i in range(3))

    inv_freq = ROPE_THETA ** (-jnp.arange(0, HEAD_DIM, 2, dtype=f32) / HEAD_DIM)
    freq_row = jnp.tile(inv_freq, LANES // (HEAD_DIM // 2)).reshape(1, LANES)
    cos_t, sin_t = _rope_tables(positions.reshape(t, 1), freq_row)

    main = d_inner + conv_dim
    assert main % LANES == 0 and ssm_heads <= LANES
    w_ssm = ssm_w_in.astype(bf16)
    w_ssm_dt = jnp.pad(w_ssm[:, :, main:], ((0, 0), (0, 0), (0, LANES - ssm_heads)))

    x2 = x.reshape(t, d)
    for layer in range(depth):
        j = layer // 2
        h = _norm_mod(x2, norm_w[layer].reshape(1, d), scale[layer], shift[layer], seq)
        if layer % 2 == 0:
            lambda_init = 0.8 - 0.6 * math.exp(-0.3 * layer)
            rep = qk_width // HEAD_DIM
            qk_w = jnp.concatenate([jnp.tile(attn_q_norm[j], rep) * (HEAD_DIM ** -0.5 * math.log2(math.e)),
                                    jnp.tile(attn_k_norm[j], rep)]).reshape(1, 2 * qk_width)
            tn_a = _tile(qk_width, 1024)
            qk = _proj_qk(h, attn_w_in, j, qk_w, cos_t, sin_t, tm, tn_a)
            vg = _proj(h, attn_w_in, j, attn_w_in.shape[2] - 2 * qk_width, bf16, tm, tn_a,
                       col0=2 * qk_width)
            o = _diff_attn(qk, vg, attn_lambda_q1[j], attn_lambda_k1[j], attn_lambda_q2[j],
                           attn_lambda_k2[j], attn_subln_w[j], batch=batch, seq=seq, heads=heads,
                           lambda_init=lambda_init)
            x2 = _out_proj(o, attn_w_out, j, x2, gate[layer], seq, tm, _tile(d, 1024))
        else:
            tn_c = _tile(math.gcd(d_inner, conv_dim), 1024)
            conv_b = ssm_conv_b[j].reshape(1, -1)
            xconv, zact = _proj_conv(h, w_ssm, j, d_inner, d_inner, ssm_conv_w[j], conv_b, 0, seq, tm, tn_c,
                                     gate_col0=0)
            bc = _proj_conv(h, w_ssm, j, 2 * d_inner, conv_dim - d_inner, ssm_conv_w[j], conv_b, d_inner,
                            seq, tm, tn_c)
            dt_raw = _proj(h, w_ssm_dt, j, LANES, f32, tm, LANES)
            y = _ssd(zact, xconv, bc, dt_raw, ssm_dt_bias[j], ssm_A_log[j], ssm_D[j], ssm_norm_w[j],
                     batch=batch, seq=seq, d_inner=d_inner, groups=groups)
            x2 = _out_proj(y, ssm_w_out, j, x2, gate[layer], seq, min(512, seq), _tile(d, 1024))
    return x2.reshape(batch, seq, d)
```

```python
import functools
import math

import numpy as np
import jax
import jax.numpy as jnp
from jax import lax
from jax.experimental import pallas as pl
from jax.experimental.pallas import tpu as pltpu

EPS = 1e-6
ROPE_THETA = 10000.0
LANES = 128
BF16_ROWS = 16
CONV_HALO = 8
QK_ROW_CHUNKS = 4
CONV_ROW_CHUNKS = 4
HEAD_DIM = 64
SSM_HEAD_DIM = 64
SSM_STATE = 128
SSM_CHUNK = 128
SSM_CONV = 4
VMEM_LIMIT_BYTES = 56 * 1024 * 1024
NEG_BIG = -1e30

f32 = jnp.float32
bf16 = jnp.bfloat16


def _params(*sem):
    return pltpu.CompilerParams(dimension_semantics=sem, vmem_limit_bytes=VMEM_LIMIT_BYTES)


def _silu(x):
    return x * jax.nn.sigmoid(x)


def _tile(n, target):
    if n <= target:
        return n
    best = None
    for cand in range(LANES, target + 1, LANES):
        if n % cand == 0:
            best = cand
    assert best is not None, (n, target)
    return best


def _ada_kernel(c_ref, w_ref, b_ref, o_ref):
    cond = _silu(c_ref[...])
    o_ref[0] = jnp.dot(cond, w_ref[0], preferred_element_type=f32) + b_ref[0]


def _ada_mod(c_pad, ada_w, ada_b, tn=1024):
    depth, d, n = ada_w.shape
    rows = c_pad.shape[0]
    tn = _tile(n, tn)
    return pl.pallas_call(
        _ada_kernel,
        grid=(depth, n // tn),
        in_specs=[pl.BlockSpec((rows, d), lambda l, j: (0, 0)),
                  pl.BlockSpec((1, d, tn), lambda l, j: (l, 0, j)),
                  pl.BlockSpec((1, 1, tn), lambda l, j: (l, 0, j))],
        out_specs=pl.BlockSpec((1, rows, tn), lambda l, j: (l, 0, j)),
        out_shape=jax.ShapeDtypeStruct((depth, rows, n), f32),
        compiler_params=_params("arbitrary", "arbitrary"),
        name="ada_mod",
    )(c_pad, ada_w, ada_b.reshape(depth, 1, n))


def _norm_kernel(x_ref, nw_ref, sc_ref, sh_ref, o_ref):
    x = x_ref[...]
    ms = jnp.mean(x * x, axis=-1, keepdims=True)
    y = x * lax.rsqrt(ms + EPS) * nw_ref[...]
    o_ref[...] = (y * (1.0 + sc_ref[0]) + sh_ref[0]).astype(o_ref.dtype)


def _norm_mod(x2, nw, scale, shift, seq, tm=1024):
    t, d = x2.shape
    tm = min(tm, seq)
    tiles_per_seq = seq // tm
    return pl.pallas_call(
        _norm_kernel,
        grid=(t // tm,),
        in_specs=[pl.BlockSpec((tm, d), lambda i: (i, 0)),
                  pl.BlockSpec((1, d), lambda i: (0, 0)),
                  pl.BlockSpec((1, 1, d), lambda i: (i // tiles_per_seq, 0, 0)),
                  pl.BlockSpec((1, 1, d), lambda i: (i // tiles_per_seq, 0, 0))],
        out_specs=pl.BlockSpec((tm, d), lambda i: (i, 0)),
        out_shape=jax.ShapeDtypeStruct((t, d), bf16),
        compiler_params=_params("arbitrary"),
        name="norm_mod",
    )(x2, nw, scale, shift)


def _stage_weight(w_ref, wb_ref):
    @pl.when(pl.program_id(1) == 0)
    def _():
        wb_ref[...] = w_ref[0].astype(bf16)


def _proj_kernel(a_ref, w_ref, o_ref, wb_ref, *, silu):
    _stage_weight(w_ref, wb_ref)
    acc = jnp.dot(a_ref[...], wb_ref[...], preferred_element_type=f32)
    if silu:
        acc = acc / (1.0 + jnp.exp(-acc))
    o_ref[...] = acc.astype(o_ref.dtype)


def _proj(a, w, w_layer, n_out, out_dtype, tm, tn, silu=False, col0=0):
    t, k = a.shape
    assert col0 % tn == 0
    cb0 = col0 // tn
    kern = functools.partial(_proj_kernel, silu=silu)
    return pl.pallas_call(
        kern,
        grid=(n_out // tn, t // tm),
        in_specs=[pl.BlockSpec((tm, k), lambda j, i: (i, 0)),
                  pl.BlockSpec((1, k, tn), lambda j, i: (w_layer, 0, cb0 + j))],
        out_specs=pl.BlockSpec((tm, tn), lambda j, i: (i, j)),
        out_shape=jax.ShapeDtypeStruct((t, n_out), out_dtype),
        scratch_shapes=[pltpu.VMEM((k, tn), bf16)],
        compiler_params=_params("arbitrary", "arbitrary"),
        name="proj",
    )(a, w)


def _proj_qk_kernel(a_ref, w_ref, qw_ref, cos_ref, sin_ref, o_ref, wb_ref):
    _stage_weight(w_ref, wb_ref)
    tm, tn = o_ref.shape
    r_idx = lax.broadcasted_iota(jnp.int32, (LANES, LANES), 0) // HEAD_DIM
    c_idx = lax.broadcasted_iota(jnp.int32, (LANES, LANES), 1) // HEAD_DIM
    group_mean = jnp.where(r_idx == c_idx, 1.0 / HEAD_DIM, 0.0).astype(bf16)
    rc = tm // QK_ROW_CHUNKS
    lane = lax.broadcasted_iota(jnp.int32, (rc, LANES), 1)
    first_half = (lane & (HEAD_DIM // 2)) == 0
    w = wb_ref[...]
    for r in range(QK_ROW_CHUNKS):
        rows = slice(r * rc, (r + 1) * rc)
        acc = jnp.dot(a_ref[rows, :], w, preferred_element_type=f32)
        cos = cos_ref[rows, :]
        sin = sin_ref[rows, :]
        for c in range(tn // LANES):
            sl = slice(c * LANES, (c + 1) * LANES)
            x = acc[:, sl]
            ms = jnp.dot((x * x).astype(bf16), group_mean, preferred_element_type=f32)
            y = x * lax.rsqrt(ms + EPS) * qw_ref[:, sl]
            partner = jnp.where(first_half,
                                pltpu.roll(y, LANES - HEAD_DIM // 2, 1),
                                pltpu.roll(y, HEAD_DIM // 2, 1))
            o_ref[rows, sl] = (y * cos + partner * sin).astype(o_ref.dtype)


def _proj_qk(a, w, w_layer, qk_w, cos_t, sin_t, tm, tn):
    t, k = a.shape
    n = qk_w.shape[1]
    return pl.pallas_call(
        _proj_qk_kernel,
        grid=(n // tn, t // tm),
        in_specs=[pl.BlockSpec((tm, k), lambda j, i: (i, 0)),
                  pl.BlockSpec((1, k, tn), lambda j, i: (w_layer, 0, j)),
                  pl.BlockSpec((1, tn), lambda j, i: (0, j)),
                  pl.BlockSpec((tm, LANES), lambda j, i: (i, 0)),
                  pl.BlockSpec((tm, LANES), lambda j, i: (i, 0))],
        out_specs=pl.BlockSpec((tm, tn), lambda j, i: (i, j)),
        out_shape=jax.ShapeDtypeStruct((t, n), bf16),
        scratch_shapes=[pltpu.VMEM((k, tn), bf16)],
        compiler_params=_params("arbitrary", "arbitrary"),
        name="proj_qk",
    )(a, w, qk_w, cos_t, sin_t)


def _proj_conv_kernel(*refs, tiles_per_seq, with_gate):
    if with_gate:
        a_ref, w_ref, cw_ref, cb_ref, wz_ref, o_ref, oz_ref, halo_ref = refs
    else:
        a_ref, w_ref, cw_ref, cb_ref, o_ref, halo_ref = refs
    i = pl.program_id(1)

    tm, tn = o_ref.shape
    rc = tm // CONV_ROW_CHUNKS
    w = w_ref[0]
    first = (i % tiles_per_seq) == 0
    prev = jnp.where(first, 0.0, halo_ref[...])
    nsl = rc // CONV_HALO
    sub = lax.broadcasted_iota(jnp.int32, (nsl + 1, CONV_HALO, tn), 1)
    w0, w1, w2, w3 = (cw_ref[tap:tap + 1, :] for tap in range(SSM_CONV))

    def shift_rows(slabs, back):
        rot = pltpu.roll(slabs, back, 1)
        above = jnp.concatenate([rot[:1], rot[:-1]], axis=0)
        return jnp.where(sub < back, above, rot)

    for r in range(CONV_ROW_CHUNKS):
        acc = jnp.dot(a_ref[r * rc:(r + 1) * rc, :], w, preferred_element_type=f32)
        slabs = jnp.concatenate([prev, acc], axis=0).reshape(nsl + 1, CONV_HALO, tn)
        x1 = shift_rows(slabs, 1)
        v2 = shift_rows(w1 * slabs + w0 * x1, 2)
        y = (cb_ref[...] + w3 * slabs + w2 * x1 + v2)[1:].reshape(rc, tn)
        o_ref[r * rc:(r + 1) * rc, :] = (y / (1.0 + jnp.exp(-y))).astype(o_ref.dtype)
        prev = acc[rc - CONV_HALO:, :]
        if with_gate:
            z = jnp.dot(a_ref[r * rc:(r + 1) * rc, :], wz_ref[0], preferred_element_type=f32)
            oz_ref[r * rc:(r + 1) * rc, :] = (z / (1.0 + jnp.exp(-z))).astype(oz_ref.dtype)
    halo_ref[...] = prev


def _proj_conv(a, w, w_layer, col0, n, conv_w, conv_b, conv_col0, seq, tm, tn, gate_col0=None):
    t, k = a.shape
    assert col0 % tn == 0 and conv_col0 % tn == 0 and n % tn == 0
    cb0, ccb0 = col0 // tn, conv_col0 // tn
    tiles_per_seq = seq // tm
    with_gate = gate_col0 is not None
    kern = functools.partial(_proj_conv_kernel, tiles_per_seq=tiles_per_seq, with_gate=with_gate)
    in_specs = [pl.BlockSpec((tm, k), lambda j, i: (i, 0)),
                pl.BlockSpec((1, k, tn), lambda j, i: (w_layer, 0, cb0 + j)),
                pl.BlockSpec((SSM_CONV, tn), lambda j, i: (0, ccb0 + j)),
                pl.BlockSpec((1, tn), lambda j, i: (0, ccb0 + j))]
    out_spec = pl.BlockSpec((tm, tn), lambda j, i: (i, j))
    out_shape = jax.ShapeDtypeStruct((t, n), bf16)
    operands = [a, w, conv_w, conv_b]
    if with_gate:
        assert gate_col0 % tn == 0
        gb0 = gate_col0 // tn
        in_specs.append(pl.BlockSpec((1, k, tn), lambda j, i: (w_layer, 0, gb0 + j)))
        operands.append(w)
    return pl.pallas_call(
        kern,
        grid=(n // tn, t // tm),
        in_specs=in_specs,
        out_specs=[out_spec, out_spec] if with_gate else out_spec,
        out_shape=[out_shape, out_shape] if with_gate else out_shape,
        scratch_shapes=[pltpu.VMEM((CONV_HALO, tn), f32)],
        compiler_params=_params("arbitrary", "arbitrary"),
        name="proj_conv",
    )(*operands)


def _outproj_kernel(a_ref, w_ref, x_ref, g_ref, o_ref, wb_ref):
    _stage_weight(w_ref, wb_ref)
    acc = jnp.dot(a_ref[...], wb_ref[...], preferred_element_type=f32)
    o_ref[...] = x_ref[...] + g_ref[0] * acc


def _out_proj(a, w, w_layer, x2, gate, seq, tm, tn):
    t, k = a.shape
    n = w.shape[2]
    tiles_per_seq = seq // tm
    return pl.pallas_call(
        _outproj_kernel,
        grid=(n // tn, t // tm),
        in_specs=[pl.BlockSpec((tm, k), lambda j, i: (i, 0)),
                  pl.BlockSpec((1, k, tn), lambda j, i: (w_layer, 0, j)),
                  pl.BlockSpec((tm, tn), lambda j, i: (i, j)),
                  pl.BlockSpec((1, 1, tn), lambda j, i: (i // tiles_per_seq, 0, j))],
        out_specs=pl.BlockSpec((tm, tn), lambda j, i: (i, j)),
        out_shape=jax.ShapeDtypeStruct((t, n), f32),
        scratch_shapes=[pltpu.VMEM((k, tn), bf16)],
        compiler_params=_params("arbitrary", "arbitrary"),
        name="out_proj",
    )(a, w, x2, gate)


def _rope_table_kernel(pos_ref, freq_ref, cos_ref, sin_ref):
    ang = pos_ref[...].astype(f32) * freq_ref[...]
    lane = lax.broadcasted_iota(jnp.int32, ang.shape, 1)
    first_half = (lane & (HEAD_DIM // 2)) == 0
    s = jnp.sin(ang)
    cos_ref[...] = jnp.cos(ang)
    sin_ref[...] = jnp.where(first_half, -s, s)


def _rope_tables(pos_col, freq_row, tm=1024):
    t = pos_col.shape[0]
    tm = min(tm, t)
    return pl.pallas_call(
        _rope_table_kernel,
        grid=(t // tm,),
        in_specs=[pl.BlockSpec((tm, 1), lambda i: (i, 0)),
                  pl.BlockSpec((1, LANES), lambda i: (0, 0))],
        out_specs=[pl.BlockSpec((tm, LANES), lambda i: (i, 0)),
                   pl.BlockSpec((tm, LANES), lambda i: (i, 0))],
        out_shape=[jax.ShapeDtypeStruct((t, LANES), f32)] * 2,
        compiler_params=_params("arbitrary"),
        name="rope_table",
    )(pos_col, freq_row)


def _attn_kernel(q_ref, k_ref, v_ref, g_ref, lq1_ref, lk1_ref, lq2_ref, lk2_ref, sw_ref, bias_ref, o_ref,
                 vt_ref, qq_ref, s_ref, mx_ref, acc_ref, *, bq, bk, hp, lambda_init):
    seq = q_ref.shape[0]
    nslot = bq // bk
    lam = (jnp.exp(jnp.sum(lq1_ref[...] * lk1_ref[...], axis=-1, keepdims=True))
           - jnp.exp(jnp.sum(lq2_ref[...] * lk2_ref[...], axis=-1, keepdims=True))
           + lambda_init)
    cols = [slice(hh * LANES, (hh + 1) * LANES) for hh in range(hp)]

    ones_rows = jnp.where(lax.broadcasted_iota(jnp.int32, (BF16_ROWS, bk), 0) == 0, 1.0, 0.0).astype(bf16)

    for c in range(seq // bk):
        for hh in range(hp):
            vt_ref[hh, c, 0:LANES, :] = v_ref[c * bk:(c + 1) * bk, cols[hh]].astype(f32).T.astype(bf16)
            vt_ref[hh, c, LANES:, :] = ones_rows

    d_row = lax.broadcasted_iota(jnp.int32, (LANES, bq), 0)

    def produce(hh, slot, j):
        start = pl.multiple_of(j * bk, bk)
        st = jnp.dot(k_ref[pl.ds(start, bk), cols[hh]], qq_ref[hh], preferred_element_type=f32)
        s_ref[hh, slot] = st
        mx_ref[hh, slot] = jnp.max(st, axis=0, keepdims=True)

    def consume(hh, slot, j, m, masked):
        st = s_ref[hh, slot]
        if masked:
            st = st + bias_ref[slot]
            blk_max = jnp.max(st, axis=0, keepdims=True)
        else:
            blk_max = mx_ref[hh, slot]
        m_new = jnp.maximum(m, blk_max)
        alpha = jnp.exp2(m - m_new)
        p = jnp.exp2(st - m_new)
        acc_ref[hh] = alpha * acc_ref[hh] + jnp.dot(vt_ref[hh, j], p.astype(bf16),
                                                    preferred_element_type=f32)
        return m_new

    def consume_diagonal(hh, j, m):
        half = bq // 2
        late = (slice(half, bq), slice(bq + half, 2 * bq))
        vt = vt_ref[hh, j]
        st = s_ref[hh, 0, 0:half, :] + bias_ref[0, 0:half, :]
        m1 = jnp.maximum(m, jnp.max(st, axis=0, keepdims=True))
        p = jnp.exp2(st - m1)
        acc = jnp.exp2(m - m1) * acc_ref[hh] + jnp.dot(vt[:, 0:half], p.astype(bf16),
                                                       preferred_element_type=f32)
        st = jnp.concatenate([s_ref[hh, 0, half:, l] + bias_ref[0, half:, l] for l in late], axis=1)
        m1_late = jnp.concatenate([m1[:, l] for l in late], axis=1)
        m2 = jnp.maximum(m1_late, jnp.max(st, axis=0, keepdims=True))
        p = jnp.exp2(st - m2)
        acc_late = (jnp.exp2(m1_late - m2) * jnp.concatenate([acc[:, l] for l in late], axis=1)
                    + jnp.dot(vt[:, half:], p.astype(bf16), preferred_element_type=f32))
        return jnp.concatenate([acc[:, 0:half], acc_late[:, 0:half],
                                acc[:, bq:bq + half], acc_late[:, half:]], axis=1)

    nq = seq // bq

    def start_q_block(hh, i):
        q_start = pl.multiple_of(i * bq, bq)
        qt = q_ref[pl.ds(q_start, bq), cols[hh]].astype(f32).T
        qq_ref[hh] = jnp.concatenate([jnp.where(d_row < HEAD_DIM, qt, 0.0),
                                      jnp.where(d_row >= HEAD_DIM, qt, 0.0)], axis=1).astype(bf16)
        acc_ref[hh] = jnp.zeros(acc_ref.shape[1:], f32)
        for slot in range(nslot):
            produce(hh, slot, slot)

    for hh in range(hp):
        start_q_block(hh, 0)

    def q_block(i, _):
        q_start = pl.multiple_of(i * bq, bq)

        def body(jj, carry):
            out = []
            for hh in range(hp):
                m = carry[hh]
                for slot in range(nslot):
                    m = consume(hh, slot, jj * nslot + slot, m, False)
                    produce(hh, slot, (jj + 1) * nslot + slot)
                out.append(m)
            return tuple(out)

        init = tuple(jnp.full((1, 2 * bq), NEG_BIG, f32) for _ in range(hp))
        carry = lax.fori_loop(0, i, body, init)

        for hh in range(hp):
            m = carry[hh]
            if nslot == 1:
                acc = consume_diagonal(hh, i, m)
            else:
                for slot in range(nslot):
                    m = consume(hh, slot, i * nslot + slot, m, True)
                acc = acc_ref[hh]
            inv_l = 1.0 / acc[LANES:LANES + 1]
            odt = acc[:LANES, :bq] * inv_l[:, :bq] - acc[:LANES, bq:] * (lam * inv_l[:, bq:])
            ms = jnp.mean(odt * odt, axis=0, keepdims=True)
            y = (odt * lax.rsqrt(ms + EPS)).T * (sw_ref[...] * (1.0 - lambda_init))
            g = g_ref[pl.ds(q_start, bq), cols[hh]].astype(f32)
            o_ref[pl.ds(q_start, bq), cols[hh]] = (y * _silu(g)).astype(o_ref.dtype)
            start_q_block(hh, jnp.minimum(i + 1, nq - 1))
        return 0

    lax.fori_loop(0, nq, q_block, 0)


def _diff_attn(qk, vg, lq1, lk1, lq2, lk2, subln_w, *, batch, seq, heads, lambda_init,
               bq=512, bk=512, hp=4):
    t = qk.shape[0]
    bq = min(bq, seq)
    bk = min(bk, bq)
    hp = math.gcd(hp, heads)
    nslot = bq // bk
    w = hp * LANES
    ng = heads // hp
    key = np.arange(bq)[:, None]
    qry = np.arange(bq)[None, :]
    tri = np.where(key <= qry, 0.0, NEG_BIG).astype(np.float32)
    bias = jnp.asarray(np.concatenate([tri, tri], axis=1).reshape(nslot, bk, 2 * bq))
    vec = lambda a: a.reshape(1, -1).astype(f32)
    small = pl.BlockSpec((1, HEAD_DIM), lambda b, h: (0, 0))
    single = pl.Buffered(1)
    kern = functools.partial(_attn_kernel, bq=bq, bk=bk, hp=hp, lambda_init=lambda_init)
    return pl.pallas_call(
        kern,
        grid=(batch, ng),
        in_specs=[pl.BlockSpec((seq, w), lambda b, h: (b, h), pipeline_mode=single),
                  pl.BlockSpec((seq, w), lambda b, h: (b, ng + h)),
                  pl.BlockSpec((seq, w), lambda b, h: (b, h)),
                  pl.BlockSpec((seq, w), lambda b, h: (b, ng + h), pipeline_mode=single),
                  small, small, small, small,
                  pl.BlockSpec((1, LANES), lambda b, h: (0, 0)),
                  pl.BlockSpec((nslot, bk, 2 * bq), lambda b, h: (0, 0, 0), pipeline_mode=single)],
        out_specs=pl.BlockSpec((seq, w), lambda b, h: (b, h)),
        out_shape=jax.ShapeDtypeStruct((t, heads * LANES), bf16),
        scratch_shapes=[pltpu.VMEM((hp, seq // bk, LANES + BF16_ROWS, bk), bf16),
                        pltpu.VMEM((hp, LANES, 2 * bq), bf16),
                        pltpu.VMEM((hp, nslot, bk, 2 * bq), f32),
                        pltpu.VMEM((hp, nslot, 1, 2 * bq), f32),
                        pltpu.VMEM((hp, LANES + BF16_ROWS, 2 * bq), f32)],
        compiler_params=_params("arbitrary", "arbitrary"),
        name="diff_attn",
    )(qk, qk, vg, vg, vec(lq1), vec(lk1), vec(lq2), vec(lk2), vec(subln_w), bias)


def _ssd_constants(rheads):
    L, P = SSM_CHUNK, SSM_HEAD_DIM
    k = np.arange(LANES)[:, None]
    col = np.arange(rheads * LANES)[None, :]
    seg = (k < 3 * rheads) & (k % rheads == col // LANES)
    col2 = np.arange(2 * rheads * P)[None, :]
    half = rheads * P
    exp = np.where(col2 < half,
                   (k < 2 * rheads) & (k % rheads == col2 // P),
                   (k >= 2 * rheads) & (k < 4 * rheads) & (k % rheads == (col2 - half) // P))
    upper = (np.arange(L)[:, None] <= np.arange(L)[None, :])
    return jnp.asarray(seg, dtype=bf16), jnp.asarray(exp, dtype=bf16), jnp.asarray(upper, dtype=bf16)


def _split_bf16(a, pieces):
    out, rem = [], a
    for _ in range(pieces):
        p = rem.astype(bf16).astype(f32)
        out.append(p)
        rem = rem - p
    return out


def _ssd_kernel(dt_ref, dtb_ref, alog_ref, z_ref, xs_ref, b_ref, c_ref, dskip_ref, nw_ref,
                segoh_ref, expoh_ref, upper_ref,
                o_ref,
                state_ref, cs_ref, dtc_ref, cslt_ref,
                *, groups, rheads, gp):
    c = pl.program_id(1)
    pg = pl.program_id(2)

    @pl.when(c == 0)
    def _():
        state_ref[:, pg] = jnp.zeros((gp,) + state_ref.shape[2:], f32)

    def bookkeeping():
        raw = dt_ref[...] + dtb_ref[...]
        dt = jnp.maximum(raw, 0.0) + jnp.log(1.0 + jnp.exp(-jnp.abs(raw)))
        da_t = (dt * (-jnp.exp(alog_ref[...]))).T
        cs_t = sum(jnp.dot(piece.astype(bf16), upper_ref[...], preferred_element_type=f32)
                   for piece in _split_bf16(da_t, 3))
        cslt_ref[...] = cs_t - jnp.log(dt.T)
        cs = cs_t.T
        for gg in range(groups):
            shift = (LANES - gg * rheads) % LANES
            cs_ref[gg] = pltpu.roll(cs, shift, 1) if shift else cs
            dtc_ref[gg] = pltpu.roll(dt, shift, 1) if shift else dt

    if gp == groups:
        bookkeeping()
    else:
        pl.when(pg == 0)(bookkeeping)

    for gi in range(gp):
        _ssd_group(gi, pg, z_ref, xs_ref, b_ref, c_ref, dskip_ref, nw_ref, segoh_ref, expoh_ref,
                   o_ref, state_ref, cs_ref, dtc_ref, cslt_ref, rheads, gp)


def _ssd_group(gi, pg, z_ref, xs_ref, b_ref, c_ref, dskip_ref, nw_ref, segoh_ref, expoh_ref, o_ref,
               state_ref, cs_ref, dtc_ref, cslt_ref, rheads, gp):
    L = SSM_CHUNK
    R = rheads
    xw = rheads * SSM_HEAD_DIM
    n = SSM_STATE
    g = pg * gp + gi
    xsl = slice(gi * xw, (gi + 1) * xw)
    nsl = slice(gi * n, (gi + 1) * n)

    xs = xs_ref[:, xsl].astype(f32)
    bm16 = b_ref[:, nsl]
    cm16 = c_ref[:, nsl]

    row0 = pl.multiple_of(g * rheads, rheads)
    csl_t = cslt_ref[pl.ds(row0, rheads), :]
    cs_c = cs_ref[g]
    dt_c = dtc_ref[g]
    lane = lax.broadcasted_iota(jnp.int32, (L, LANES), 1)

    hi, mid, lo = _split_bf16(cs_c, 3)
    packed = jnp.where(lane < R, hi, jnp.where(lane < 2 * R, pltpu.roll(mid, R, 1), pltpu.roll(lo, 2 * R, 1)))
    seg_col = jnp.dot(packed.astype(bf16), segoh_ref[...], preferred_element_type=f32)

    cb = lax.dot_general(cm16, bm16, (((1,), (1,)), ((), ())),
                         preferred_element_type=f32)
    l_idx = lax.broadcasted_iota(jnp.int32, (L, L), 0)
    s_idx = lax.broadcasted_iota(jnp.int32, (L, L), 1)
    causal = l_idx >= s_idx
    lo_half = lane < SSM_HEAD_DIM

    y_parts = []
    for j in range(rheads // 2):
        xj = xs[:, j * LANES:(j + 1) * LANES]
        x_lo = jnp.where(lo_half, xj, 0.0).astype(bf16)
        x_hi = jnp.where(lo_half, 0.0, xj).astype(bf16)
        yj = None
        for r, xr in ((2 * j, x_lo), (2 * j + 1, x_hi)):
            seg = seg_col[:, r * LANES:(r + 1) * LANES] - csl_t[r:r + 1, :]
            decay_dt = jnp.exp(jnp.where(causal, seg, -jnp.inf))
            mr = (cb * decay_dt).astype(bf16)
            part = jnp.dot(mr, xr, preferred_element_type=f32)
            yj = part if yj is None else yj + part
        y_parts.append(yj)
    y_diag = jnp.concatenate(y_parts, axis=1)

    cs_last = cs_c[L - 1:L, :]
    e_hi, e_lo = _split_bf16(jnp.exp(cs_c), 2)
    w_hi, w_lo = _split_bf16(jnp.exp(cs_last - cs_c) * dt_c, 2)
    packed2 = jnp.where(lane < R, e_hi,
                        jnp.where(lane < 2 * R, pltpu.roll(e_lo, R, 1),
                                  jnp.where(lane < 3 * R, pltpu.roll(w_hi, 2 * R, 1),
                                            pltpu.roll(w_lo, 3 * R, 1))))
    spread = jnp.dot(packed2.astype(bf16), expoh_ref[...], preferred_element_type=f32)
    e_cs = spread[:, :xw]
    wgt = spread[:, xw:]

    prev = state_ref[gi, pg]
    y_off = jnp.dot(cm16, prev.astype(bf16), preferred_element_type=f32) * e_cs
    new = jnp.dot(bm16.astype(f32).T.astype(bf16), (xs * wgt).astype(bf16), preferred_element_type=f32)
    state_ref[gi, pg] = prev * e_cs[L - 1:L, :] + new

    y = y_diag + y_off + xs * dskip_ref[:, xsl]
    y = y * z_ref[:, xsl].astype(f32)
    ms = jnp.mean(y * y, axis=-1, keepdims=True)
    o_ref[:, xsl] = (y * lax.rsqrt(ms + EPS) * nw_ref[:, xsl]).astype(o_ref.dtype)


def _ssd(zproj, xconv, bc, dt_raw, dt_bias, a_log, d_skip, norm_w, *, batch, seq, d_inner, groups, gp=8):
    t = zproj.shape[0]
    L = SSM_CHUNK
    nc = seq // L
    heads = d_inner // SSM_HEAD_DIM
    rheads = heads // groups
    assert rheads % 2 == 0 and 4 * rheads <= LANES
    gp = math.gcd(gp, groups)
    xw = rheads * SSM_HEAD_DIM
    n = SSM_STATE
    bw, bn = gp * xw, gp * n
    assert d_inner % bw == 0
    b_blk0 = 0
    c_blk0 = groups // gp
    row = lambda b, c, g: b * nc + c
    pad = lambda a: jnp.pad(a.reshape(1, -1).astype(f32), ((0, 0), (0, LANES - heads)))
    const2 = lambda shape: pl.BlockSpec(shape, lambda b, c, g: (0, 0))
    seg_oh, exp_oh, upper = _ssd_constants(rheads)
    kern = functools.partial(_ssd_kernel, groups=groups, rheads=rheads, gp=gp)
    return pl.pallas_call(
        kern,
        grid=(batch, nc, groups // gp),
        in_specs=[pl.BlockSpec((L, LANES), lambda b, c, g: (row(b, c, g), 0)),
                  const2((1, LANES)),
                  const2((1, LANES)),
                  pl.BlockSpec((L, bw), lambda b, c, g: (row(b, c, g), g)),
                  pl.BlockSpec((L, bw), lambda b, c, g: (row(b, c, g), g)),
                  pl.BlockSpec((L, bn), lambda b, c, g: (row(b, c, g), b_blk0 + g)),
                  pl.BlockSpec((L, bn), lambda b, c, g: (row(b, c, g), c_blk0 + g)),
                  pl.BlockSpec((1, bw), lambda b, c, g: (0, g)),
                  pl.BlockSpec((1, bw), lambda b, c, g: (0, g)),
                  const2((LANES, rheads * LANES)),
                  const2((LANES, 2 * xw)),
                  const2((L, L))],
        out_specs=pl.BlockSpec((L, bw), lambda b, c, g: (row(b, c, g), g)),
        out_shape=jax.ShapeDtypeStruct((t, d_inner), bf16),
        scratch_shapes=[pltpu.VMEM((gp, groups // gp, n, xw), f32),
                        pltpu.VMEM((groups, L, LANES), f32),
                        pltpu.VMEM((groups, L, LANES), f32),
                        pltpu.VMEM((LANES, L), f32)],
        compiler_params=_params("arbitrary", "arbitrary", "arbitrary"),
        name="ssd_chunk",
    )(dt_raw, pad(dt_bias), pad(a_log), zproj, xconv, bc, bc,
      jnp.repeat(d_skip, SSM_HEAD_DIM).reshape(1, -1), norm_w.reshape(1, -1),
      seg_oh, exp_oh, upper)


def kernel(x, c, positions, norm_w, ada_w, ada_b, attn_w_in, attn_q_norm, attn_k_norm, attn_lambda_q1, attn_lambda_k1, attn_lambda_q2, attn_lambda_k2, attn_subln_w, attn_w_out, ssm_w_in, ssm_conv_w, ssm_conv_b, ssm_dt_bias, ssm_A_log, ssm_D, ssm_norm_w, ssm_w_out):
    batch, seq, d = x.shape
    depth = norm_w.shape[0]
    t = batch * seq
    heads = d // (2 * HEAD_DIM)
    qk_width = heads * 2 * HEAD_DIM
    d_inner = ssm_w_out.shape[1]
    conv_dim = ssm_conv_w.shape[2]
    groups = (conv_dim - d_inner) // (2 * SSM_STATE)
    ssm_heads = ssm_dt_bias.shape[1]
    tm = min(1024, seq)

    c_pad = jnp.pad(c, ((0, 8 - batch), (0, 0)))
    mod = _ada_mod(c_pad, ada_w, ada_b)[:, :batch]
    shift, scale, gate = (mod[:, :, i * d:(i + 1) * d].reshape(depth, batch, 1, d) for i in range(3))

    inv_freq = ROPE_THETA ** (-jnp.arange(0, HEAD_DIM, 2, dtype=f32) / HEAD_DIM)
    freq_row = jnp.tile(inv_freq, LANES // (HEAD_DIM // 2)).reshape(1, LANES)
    cos_t, sin_t = _rope_tables(positions.reshape(t, 1), freq_row)

    main = d_inner + conv_dim
    assert main % LANES == 0 and ssm_heads <= LANES
    w_ssm = ssm_w_in.astype(bf16)
    w_ssm_dt = jnp.pad(w_ssm[:, :, main:], ((0, 0), (0, 0), (0, LANES - ssm_heads)))

    x2 = x.reshape(t, d)
    for layer in range(depth):
        j = layer // 2
        h = _norm_mod(x2, norm_w[layer].reshape(1, d), scale[layer], shift[layer], seq)
        if layer % 2 == 0:
            lambda_init = 0.8 - 0.6 * math.exp(-0.3 * layer)
            rep = qk_width // HEAD_DIM
            qk_w = jnp.concatenate([jnp.tile(attn_q_norm[j], rep) * (HEAD_DIM ** -0.5 * math.log2(math.e)),
                                    jnp.tile(attn_k_norm[j], rep)]).reshape(1, 2 * qk_width)
            tn_a = _tile(qk_width, 1024)
            qk = _proj_qk(h, attn_w_in, j, qk_w, cos_t, sin_t, tm, tn_a)
            vg = _proj(h, attn_w_in, j, attn_w_in.shape[2] - 2 * qk_width, bf16, tm, tn_a,
                       col0=2 * qk_width)
            o = _diff_attn(qk, vg, attn_lambda_q1[j], attn_lambda_k1[j], attn_lambda_q2[j],
                           attn_lambda_k2[j], attn_subln_w[j], batch=batch, seq=seq, heads=heads,
                           lambda_init=lambda_init)
            x2 = _out_proj(o, attn_w_out, j, x2, gate[layer], seq, tm, _tile(d, 1024))
        else:
            tn_c = _tile(math.gcd(d_inner, conv_dim), 1024)
            conv_b = ssm_conv_b[j].reshape(1, -1)
            xconv, zact = _proj_conv(h, w_ssm, j, d_inner, d_inner, ssm_conv_w[j], conv_b, 0, seq, tm, tn_c,
                                     gate_col0=0)
            bc = _proj_conv(h, w_ssm, j, 2 * d_inner, conv_dim - d_inner, ssm_conv_w[j], conv_b, d_inner,
                            seq, tm, tn_c)
            dt_raw = _proj(h, w_ssm_dt, j, LANES, f32, tm, LANES)
            y = _ssd(zact, xconv, bc, dt_raw, ssm_dt_bias[j], ssm_A_log[j], ssm_D[j], ssm_norm_w[j],
                     batch=batch, seq=seq, d_inner=d_inner, groups=groups)
            x2 = _out_proj(y, ssm_w_out, j, x2, gate[layer], seq, tm, _tile(d, 512))
    return x2.reshape(batch, seq, d)
```
